```python
import jax
import jax.numpy as jnp
from jax import lax
import numpy as np

D_MODEL = 1024
BATCH = 16
SEQ = 4096
DEPTH = 4

HEAD_DIM = 64
H_TOTAL = D_MODEL // HEAD_DIM
H_RET = H_TOTAL // 4
H_HGRN = (H_TOTAL - H_RET) // 2
H_FOX = H_TOTAL - H_RET - H_HGRN
D_HGRN = H_HGRN * HEAD_DIM
D_FOX = H_FOX * HEAD_DIM
D_RET = H_RET * HEAD_DIM
D_MIX = D_HGRN + D_FOX + D_RET
IN_COLS = 4 * D_HGRN + 4 * D_FOX + H_FOX + 4 * D_RET
D_FF = 2816
CONV_W = 3
CHUNK = 64
Q_BLOCK = 128
ROPE_BASE = 10000.0
EPS = 1e-6
MIN_FORGET = 1e-12
MASK_VALUE = -1e30

kernel_name = "hymba_style_hgrn2_fox_retnet_convffn"


def rmsnorm(x, gain):
    xf = x.astype(jnp.float32)
    y = xf * lax.rsqrt(jnp.mean(xf * xf, axis=-1, keepdims=True) + EPS)
    return (y * gain.astype(jnp.float32)).astype(x.dtype)


def heads(a, n_heads):
    return a.astype(jnp.float32).reshape(a.shape[0], a.shape[1], n_heads, HEAD_DIM)


def to_chunks(a):
    b, t, h, d = a.shape
    return a.reshape(b, t // CHUNK, CHUNK, h, d).transpose(1, 0, 3, 2, 4)


def from_chunks(a):
    n, b, h, c, d = a.shape
    return a.transpose(1, 0, 3, 2, 4).reshape(b, n * c, h, d)


def rotary_tables(t):
    inv_freq = 1.0 / (ROPE_BASE ** (jnp.arange(0, HEAD_DIM, 2, dtype=jnp.float32) / HEAD_DIM))
    ang = jnp.arange(t, dtype=jnp.float32)[:, None] * inv_freq[None, :]
    return jnp.cos(ang), jnp.sin(ang)


def apply_rotary(x, cos, sin):
    x1, x2 = jnp.split(x, 2, axis=-1)
    c = cos[None, :, None, :]
    s = sin[None, :, None, :]
    return jnp.concatenate([x1 * c - x2 * s, x1 * s + x2 * c], axis=-1)


def hgrn2_mixer(q_in, f_in, i_in, g_in, lb, out_gain):
    b = q_in.shape[0]
    q = heads(q_in, H_HGRN)
    v = heads(i_in, H_HGRN)
    z = heads(f_in, H_HGRN)
    lb = lb.astype(jnp.float32).reshape(H_HGRN, HEAD_DIM)
    f = lb + (1.0 - lb) * jax.nn.sigmoid(z)
    log_f = jnp.log(jnp.maximum(f, MIN_FORGET))
    k = (1.0 - lb) * jax.nn.sigmoid(-z)
    causal = jnp.tril(jnp.ones((CHUNK, CHUNK), dtype=bool))[:, :, None]

    def step(S, inp):
        qc, kc, vc, lfc = inp
        cum = jnp.cumsum(lfc, axis=2)
        diff = cum[:, :, :, None, :] - cum[:, :, None, :, :]
        decay = jnp.where(causal, jnp.exp(jnp.where(causal, diff, 0.0)), 0.0)
        scores = jnp.einsum('bhtk,bhtsk,bhsk->bhts', qc, decay, kc)
        o = (jnp.einsum('bhts,bhsv->bhtv', scores, vc)
             + jnp.einsum('bhtk,bhkv->bhtv', qc * jnp.exp(cum), S))
        last = cum[:, :, -1:, :]
        S = (jnp.exp(last[:, :, 0, :, None]) * S
             + jnp.einsum('bhsk,bhsv->bhkv', kc * jnp.exp(last - cum), vc))
        return S, o

    S0 = jnp.zeros((b, H_HGRN, HEAD_DIM, HEAD_DIM), jnp.float32)
    _, o = lax.scan(step, S0, (to_chunks(q), to_chunks(k), to_chunks(v), to_chunks(log_f)))
    o = from_chunks(o)
    o = rmsnorm(o, out_gain) * jax.nn.silu(heads(g_in, H_HGRN))
    return o.reshape(b, -1, D_HGRN)


def fox_mixer(q_in, k_in, v_in, f_in, g_in, b_f, q_gain, k_gain):
    b, t, _ = q_in.shape
    q = rmsnorm(heads(q_in, H_FOX), q_gain).transpose(0, 2, 1, 3)
    k = rmsnorm(heads(k_in, H_FOX), k_gain).transpose(0, 2, 1, 3)
    v = heads(v_in, H_FOX).transpose(0, 2, 1, 3)
    log_f = jax.nn.log_sigmoid(f_in.astype(jnp.float32) + b_f.astype(jnp.float32))
    F = jnp.cumsum(log_f, axis=1).transpose(0, 2, 1)
    scale = HEAD_DIM ** -0.5
    outs = []
    for blk in range(t // Q_BLOCK):
        t0, t1 = blk * Q_BLOCK, (blk + 1) * Q_BLOCK
        logits = (jnp.einsum('bhtd,bhsd->bhts', q[:, :, t0:t1], k[:, :, :t1]) * scale
                  + (F[:, :, t0:t1, None] - F[:, :, None, :t1]))
        mask = (t0 + jnp.arange(Q_BLOCK))[:, None] >= jnp.arange(t1)[None, :]
        p = jax.nn.softmax(jnp.where(mask, logits, MASK_VALUE), axis=-1)
        outs.append(jnp.einsum('bhts,bhsd->bhtd', p, v[:, :, :t1]))
    o = jnp.concatenate(outs, axis=2).transpose(0, 2, 1, 3)
    o = o * jax.nn.sigmoid(heads(g_in, H_FOX))
    return o.reshape(b, t, D_FOX)


def retention_mixer(q_in, k_in, v_in, g_in, out_gain, cos, sin):
    b = q_in.shape[0]
    q = apply_rotary(heads(q_in, H_RET), cos, sin)
    k = apply_rotary(heads(k_in, H_RET), cos, sin) * (HEAD_DIM ** -0.5)
    v = heads(v_in, H_RET)
    log_gamma = jnp.log1p(-jnp.exp2(-5.0 - jnp.arange(H_RET, dtype=jnp.float32)))
    pos = jnp.arange(CHUNK, dtype=jnp.float32)
    rel = pos[:, None] - pos[None, :]
    intra = jnp.where(rel >= 0, jnp.exp(log_gamma[:, None, None] * jnp.maximum(rel, 0.0)), 0.0)
    q_scale = jnp.exp(log_gamma[:, None] * (pos + 1.0))[:, :, None]
    k_scale = jnp.exp(log_gamma[:, None] * (CHUNK - 1.0 - pos))[:, :, None]
    chunk_decay = jnp.exp(log_gamma * CHUNK)[:, None, None]

    def step(R, inp):
        qc, kc, vc = inp
        scores = jnp.einsum('bhtd,bhsd->bhts', qc, kc) * intra
        o = (jnp.einsum('bhts,bhsv->bhtv', scores, vc)
             + jnp.einsum('bhtd,bhdv->bhtv', qc * q_scale, R))
        R = chunk_decay * R + jnp.einsum('bhsd,bhsv->bhdv', kc * k_scale, vc)
        return R, o

    R0 = jnp.zeros((b, H_RET, HEAD_DIM, HEAD_DIM), jnp.float32)
    _, o = lax.scan(step, R0, (to_chunks(q), to_chunks(k), to_chunks(v)))
    o = from_chunks(o)
    o = rmsnorm(o, out_gain) * jax.nn.silu(heads(g_in, H_RET))
    return o.reshape(b, -1, D_RET)


def conv_gated_mlp(h, w_up, conv_w, conv_b, w_down):
    t = h.shape[1]
    u = h @ w_up
    u_pad = jnp.pad(u, ((0, 0), (CONV_W - 1, 0), (0, 0)))
    acc = conv_b
    for j in range(CONV_W):
        acc = acc + conv_w[j] * u_pad[:, j:j + t]
    gate, val = jnp.split(acc, 2, axis=-1)
    return (jax.nn.silu(gate) * val) @ w_down


def setup_inputs(seed: int = 0) -> dict:
    key = jax.random.key(seed)
    ks = jax.random.split(key, 20)
    f32 = jnp.float32

    def gain(k, shape):
        return 1.0 + 0.02 * jax.random.normal(k, shape, f32)

    x = jax.random.normal(ks[0], (BATCH, SEQ, D_MODEL), f32)
    w_in = jax.random.normal(ks[1], (DEPTH, D_MODEL, IN_COLS), f32) * D_MODEL ** -0.5
    b_fox_f = jax.random.uniform(ks[2], (DEPTH, H_FOX), f32, 1.0, 4.0)
    fox_q_gain = gain(ks[3], (DEPTH, HEAD_DIM))
    fox_k_gain = gain(ks[4], (DEPTH, HEAD_DIM))
    hgrn_lb = 1.0 + 0.1 * jax.random.normal(ks[5], (DEPTH, D_HGRN), f32)
    hgrn_out_gain = gain(ks[6], (DEPTH, H_HGRN, HEAD_DIM))
    ret_out_gain = gain(ks[7], (DEPTH, H_RET, HEAD_DIM))
    w_out = jax.random.normal(ks[8], (DEPTH, D_MIX, D_MODEL), f32) * D_MIX ** -0.5
    g_mix_pre = gain(ks[9], (DEPTH, D_MODEL))
    g_mix_post = gain(ks[10], (DEPTH, D_MODEL))
    w_up = jax.random.normal(ks[11], (DEPTH, D_MODEL, 2 * D_FF), f32) * D_MODEL ** -0.5
    conv_w = 0.2 * jax.random.normal(ks[12], (DEPTH, CONV_W, 2 * D_FF), f32)
    conv_w = conv_w.at[:, CONV_W - 1, :].add(1.0)
    conv_b = 0.02 * jax.random.normal(ks[13], (DEPTH, 2 * D_FF), f32)
    w_down = jax.random.normal(ks[14], (DEPTH, D_FF, D_MODEL), f32) * D_FF ** -0.5
    g_ffn_pre = gain(ks[15], (DEPTH, D_MODEL))
    g_ffn_post = gain(ks[16], (DEPTH, D_MODEL))
    return {"x": x, "w_in": w_in, "b_fox_f": b_fox_f, "fox_q_gain": fox_q_gain,
            "fox_k_gain": fox_k_gain, "hgrn_lb": hgrn_lb, "hgrn_out_gain": hgrn_out_gain,
            "ret_out_gain": ret_out_gain, "w_out": w_out, "g_mix_pre": g_mix_pre,
            "g_mix_post": g_mix_post, "w_up": w_up, "conv_w": conv_w, "conv_b": conv_b,
            "w_down": w_down, "g_ffn_pre": g_ffn_pre, "g_ffn_post": g_ffn_post}


def reference(x, w_in, b_fox_f, fox_q_gain, fox_k_gain, hgrn_lb, hgrn_out_gain, ret_out_gain,
              w_out, g_mix_pre, g_mix_post, w_up, conv_w, conv_b, w_down, g_ffn_pre, g_ffn_post):
    t = x.shape[1]
    widths = [D_HGRN] * 4 + [D_FOX] * 3 + [H_FOX] + [D_FOX] + [D_RET] * 4
    splits = []
    off = 0
    for w in widths[:-1]:
        off += w
        splits.append(off)
    p_lb = jax.nn.softmax(hgrn_lb.astype(jnp.float32), axis=0)
    lower_bounds = jnp.cumsum(p_lb, axis=0) - p_lb[0]
    cos, sin = rotary_tables(t)
    for l in range(DEPTH):
        h = rmsnorm(x, g_mix_pre[l])
        (a_q, a_f, a_i, a_g,
         b_q, b_k, b_v, b_f, b_g,
         c_q, c_k, c_v, c_g) = jnp.split(h @ w_in[l], splits, axis=-1)
        o_a = hgrn2_mixer(a_q, a_f, a_i, a_g, lower_bounds[l], hgrn_out_gain[l])
        o_b = fox_mixer(b_q, b_k, b_v, b_f, b_g, b_fox_f[l], fox_q_gain[l], fox_k_gain[l])
        o_c = retention_mixer(c_q, c_k, c_v, c_g, ret_out_gain[l], cos, sin)
        mixed = jnp.concatenate([o_a, o_b, o_c], axis=-1).astype(x.dtype) @ w_out[l]
        x = x + rmsnorm(mixed, g_mix_post[l])
        h = rmsnorm(x, g_ffn_pre[l])
        x = x + rmsnorm(conv_gated_mlp(h, w_up[l], conv_w[l], conv_b[l], w_down[l]), g_ffn_post[l])
    return x
```

```python
import functools

import jax
import jax.numpy as jnp
from jax import lax
from jax.experimental import pallas as pl
from jax.experimental.pallas import tpu as pltpu

F32 = jnp.float32
BF16 = jnp.bfloat16

D_MODEL = 1024
HEAD_DIM = 64
H_HGRN, H_FOX, H_RET = 6, 6, 4
D_HGRN, D_FOX, D_RET = H_HGRN * HEAD_DIM, H_FOX * HEAD_DIM, H_RET * HEAD_DIM
D_FF = 2816
CONV_W = 3
ROPE_BASE = 10000.0
EPS = 1e-6
MIN_FORGET = 1e-12
MASK_VALUE = -1e30

LANES = 128
SUBLANES = 8
FGATE_PAD = LANES

_A_MAIN = (0, 3 * D_HGRN)
_A_GATE = (_A_MAIN[1], _A_MAIN[1] + D_HGRN)
_B_Q = (_A_GATE[1], _A_GATE[1] + D_FOX)
_B_K = (_B_Q[1], _B_Q[1] + D_FOX)
_B_VG = (_B_K[1], _B_K[1] + 2 * D_FOX)
_B_F = (_B_VG[1], _B_VG[1] + FGATE_PAD)
_C_Q = (_B_F[1], _B_F[1] + D_RET)
_C_K = (_C_Q[1], _C_Q[1] + D_RET)
_C_VG = (_C_K[1], _C_K[1] + 2 * D_RET)
IN_COLS_PADDED = _C_VG[1]

TM_PROJ = 512
TM_FFN = 1024
FC_FFN = 256
TT_HGRN = 256
SUB_HGRN = 32
C_RET = 256
TQ_FOX = 512
VMEM_LIMIT = 48 * 1024 * 1024


def _dot(a, b):
    return jnp.dot(a, b, preferred_element_type=F32)


def _dot_nt(a, b):
    return lax.dot_general(a, b, (((1,), (1,)), ((), ())), preferred_element_type=F32)


def _dot_tn(a, b):
    return lax.dot_general(a, b, (((0,), (0,)), ((), ())), preferred_element_type=F32)


def _dot_01_lhs(m01, x):
    hi = x.astype(BF16)
    r = x - hi.astype(F32)
    mid = r.astype(BF16)
    lo = (r - mid.astype(F32)).astype(BF16)
    return _dot(m01, hi) + _dot(m01, mid) + _dot(m01, lo)


def _head_mean_sq(y, ones_bd):
    sq = y * y
    hi = sq.astype(BF16)
    lo = (sq - hi.astype(F32)).astype(BF16)
    return (_dot(hi, ones_bd) + _dot(lo, ones_bd)) * (1.0 / HEAD_DIM)


def _rmsnorm_rows(x, gain):
    return x * lax.rsqrt(jnp.mean(x * x, axis=-1, keepdims=True) + EPS) * gain


def _sigmoid(x):
    return 1.0 / (1.0 + jnp.exp(-x))


def _first_head_lanes(shape):
    return (lax.broadcasted_iota(jnp.int32, shape, len(shape) - 1) & HEAD_DIM) == 0


def _inproj_kernel(x_ref, gpre_ref, w_ref, bf_ref, qg_ref, kg_ref, cos_ref, sin_ref, tri_ref,
                   ones_ref, pa_ref, za_ref, pb_ref, f_ref, pc_ref, carry_ref, *, tiles_per_seq):
    i = pl.program_id(0)
    h = _rmsnorm_rows(x_ref[...], gpre_ref[...]).astype(BF16)

    def proj(cols):
        return _dot(h, w_ref[:, cols[0]:cols[1]])

    pa_ref[...] = proj(_A_MAIN).astype(BF16)
    za_ref[...] = proj(_A_GATE)

    ones = ones_ref[...]
    bq = proj(_B_Q)
    bq = bq * lax.rsqrt(_head_mean_sq(bq, ones) + EPS) * qg_ref[...] * (HEAD_DIM ** -0.5)
    pb_ref[:, 0:D_FOX] = bq.astype(BF16)
    bk = proj(_B_K)
    bk = bk * lax.rsqrt(_head_mean_sq(bk, ones) + EPS) * kg_ref[...]
    pb_ref[:, D_FOX:2 * D_FOX] = bk.astype(BF16)
    pb_ref[:, 2 * D_FOX:4 * D_FOX] = proj(_B_VG).astype(BF16)

    zf = proj(_B_F) + bf_ref[...]
    log_f = -(jnp.maximum(-zf, 0.0) + jnp.log1p(jnp.exp(-jnp.abs(zf))))

    @pl.when(i % tiles_per_seq == 0)
    def _():
        carry_ref[...] = jnp.zeros_like(carry_ref)

    cum = _dot_01_lhs(tri_ref[...], log_f) + carry_ref[...]
    f_ref[...] = cum
    carry_ref[...] = cum[cum.shape[0] - 1:, :]

    cos = cos_ref[...]
    sin = sin_ref[...]
    low_half = (lax.broadcasted_iota(jnp.int32, cos.shape, 1) & (HEAD_DIM // 2)) == 0

    def rotary(y):
        swapped = jnp.where(low_half, pltpu.roll(y, D_RET - HEAD_DIM // 2, 1),
                            pltpu.roll(y, HEAD_DIM // 2, 1))
        return y * cos + swapped * sin

    pc_ref[:, 0:D_RET] = rotary(proj(_C_Q)).astype(BF16)
    pc_ref[:, D_RET:2 * D_RET] = (rotary(proj(_C_K)) * (HEAD_DIM ** -0.5)).astype(BF16)
    pc_ref[:, 2 * D_RET:4 * D_RET] = proj(_C_VG).astype(BF16)


def _inproj(x2, gpre, w, bf, qg, kg, cos, sin, tri, ones, seq_len):
    n = x2.shape[0]
    tm = TM_PROJ
    tiles_per_seq = seq_len // tm
    const = lambda i: (0, 0)
    row = lambda i: (i, 0)
    return pl.pallas_call(
        functools.partial(_inproj_kernel, tiles_per_seq=tiles_per_seq),
        grid=(n // tm,),
        in_specs=[
            pl.BlockSpec((tm, D_MODEL), row),
            pl.BlockSpec((1, D_MODEL), const),
            pl.BlockSpec((D_MODEL, IN_COLS_PADDED), const),
            pl.BlockSpec((1, FGATE_PAD), const),
            pl.BlockSpec((1, D_FOX), const),
            pl.BlockSpec((1, D_FOX), const),
            pl.BlockSpec((tm, D_RET), lambda i: (i % tiles_per_seq, 0)),
            pl.BlockSpec((tm, D_RET), lambda i: (i % tiles_per_seq, 0)),
            pl.BlockSpec((tm, tm), const),
            pl.BlockSpec((D_FOX, D_FOX), const),
        ],
        out_specs=[
            pl.BlockSpec((tm, 3 * D_HGRN), row),
            pl.BlockSpec((tm, D_HGRN), row),
            pl.BlockSpec((tm, 4 * D_FOX), row),
            pl.BlockSpec((tm, FGATE_PAD), row),
            pl.BlockSpec((tm, 4 * D_RET), row),
        ],
        out_shape=[
            jax.ShapeDtypeStruct((n, 3 * D_HGRN), BF16),
            jax.ShapeDtypeStruct((n, D_HGRN), F32),
            jax.ShapeDtypeStruct((n, 4 * D_FOX), BF16),
            jax.ShapeDtypeStruct((n, FGATE_PAD), F32),
            jax.ShapeDtypeStruct((n, 4 * D_RET), BF16),
        ],
        scratch_shapes=[pltpu.VMEM((1, FGATE_PAD), F32)],
        compiler_params=pltpu.CompilerParams(
            dimension_semantics=("arbitrary",), vmem_limit_bytes=VMEM_LIMIT),
        name="inproj",
    )(x2, gpre, w, bf, qg, kg, cos, sin, tri, ones)


def _hgrn_kernel(q_ref, v_ref, g_ref, z_ref, lb_ref, gain_ref, tri_ref, ones_ref, o_ref, st_ref,
                 *, layer):
    @pl.when(pl.program_id(1) == 0)
    def _():
        st_ref[...] = jnp.zeros_like(st_ref)

    tt = q_ref.shape[0]
    sub = SUB_HGRN

    lb = lb_ref[...]
    e = jnp.exp(lb - jnp.max(lb, axis=0, keepdims=True))
    p = e / jnp.sum(e, axis=0, keepdims=True)
    run = p[0:1]
    for i in range(1, layer + 1):
        run = run + p[i:i + 1]
    lower = run - p[0:1]

    row = lax.broadcasted_iota(jnp.int32, (tt, tt), 0)
    col = lax.broadcasted_iota(jnp.int32, (tt, tt), 1)
    causal = (col <= row) & (col >= (row & ~(sub - 1)))
    first = _first_head_lanes((tt, LANES))
    rr = lax.broadcasted_iota(jnp.int32, (LANES, LANES), 0)
    cc = lax.broadcasted_iota(jnp.int32, (LANES, LANES), 1)
    same_head = ((rr ^ cc) & HEAD_DIM) == 0
    tri = tri_ref[...]
    ones = ones_ref[...]

    for pair in range(H_HGRN // 2):
        sl = slice(pair * LANES, (pair + 1) * LANES)
        lw = lower[:, sl]
        sg = _sigmoid(z_ref[:, sl])
        f = lw + (1.0 - lw) * sg
        log_f = jnp.log(jnp.maximum(f, MIN_FORGET))
        k = (1.0 - lw) * (1.0 - sg)
        cum = _dot_01_lhs(tri, log_f)
        e_q = jnp.exp(cum)
        qt = (q_ref[:, sl].astype(F32) * e_q).astype(BF16)
        kt = (k * jnp.exp(-cum)).astype(BF16)
        v = v_ref[:, sl]

        intra = []
        for keep in (first, ~first):
            s = _dot_nt(jnp.where(keep, qt, jnp.zeros_like(qt)), kt)
            intra.append(_dot(jnp.where(causal, s, 0.0).astype(BF16), v))
        o = jnp.where(first, intra[0], intra[1])

        st = st_ref[pair]
        outs = []
        for j in range(tt // sub):
            r = slice(j * sub, (j + 1) * sub)
            outs.append(o[r] + _dot_nt(qt[r], st.astype(BF16)))
            upd = jnp.where(same_head, _dot_tn(v[r], kt[r]), 0.0)
            st = e_q[(j + 1) * sub - 1:(j + 1) * sub, :] * (st + upd)
        st_ref[pair] = st
        o = jnp.concatenate(outs, axis=0)

        o = o * lax.rsqrt(_head_mean_sq(o, ones) + EPS) * gain_ref[:, sl]
        g = g_ref[:, sl].astype(F32)
        o_ref[:, sl] = (o * (g * _sigmoid(g))).astype(BF16)


def _hgrn(pa, za, lb, gain, tri, ones, batch, seq_len, layer):
    n = pa.shape[0]
    tt = TT_HGRN
    tps = seq_len // tt
    const = lambda b, t: (0, 0)
    return pl.pallas_call(
        functools.partial(_hgrn_kernel, layer=layer),
        grid=(batch, tps),
        in_specs=[
            pl.BlockSpec((tt, D_HGRN), lambda b, t: (b * tps + t, 0)),
            pl.BlockSpec((tt, D_HGRN), lambda b, t: (b * tps + t, 1)),
            pl.BlockSpec((tt, D_HGRN), lambda b, t: (b * tps + t, 2)),
            pl.BlockSpec((tt, D_HGRN), lambda b, t: (b * tps + t, 0)),
            pl.BlockSpec(lb.shape, const),
            pl.BlockSpec((1, D_HGRN), const),
            pl.BlockSpec((tt, tt), const),
            pl.BlockSpec((LANES, LANES), const),
        ],
        out_specs=pl.BlockSpec((tt, D_HGRN), lambda b, t: (b * tps + t, 0)),
        out_shape=jax.ShapeDtypeStruct((n, D_HGRN), BF16),
        scratch_shapes=[pltpu.VMEM((H_HGRN // 2, LANES, LANES), F32)],
        compiler_params=pltpu.CompilerParams(
            dimension_semantics=("arbitrary", "arbitrary"), vmem_limit_bytes=VMEM_LIMIT),
        name="hgrn",
    )(pa, pa, pa, za, lb, gain, tri, ones)


def _ret_kernel(q_ref, k_ref, v_ref, g_ref, dec_ref, qs_ref, ks_ref, cd_ref, gain_ref, ones_ref,
                o_ref, st_ref):
    @pl.when(pl.program_id(1) == 0)
    def _():
        st_ref[...] = jnp.zeros_like(st_ref)

    c = q_ref.shape[0]
    first = _first_head_lanes((c, LANES))
    rr = lax.broadcasted_iota(jnp.int32, (LANES, LANES), 0)
    cc = lax.broadcasted_iota(jnp.int32, (LANES, LANES), 1)
    same_head = ((rr ^ cc) & HEAD_DIM) == 0
    ones = ones_ref[...]

    for pair in range(H_RET // 2):
        sl = slice(pair * LANES, (pair + 1) * LANES)
        q = q_ref[:, sl]
        k = k_ref[:, sl]
        v = v_ref[:, sl]
        intra = []
        for hh, keep in enumerate((first, ~first)):
            s = _dot_nt(jnp.where(keep, q, jnp.zeros_like(q)), k) * dec_ref[2 * pair + hh]
            intra.append(_dot(s.astype(BF16), v))
        st = st_ref[pair]
        q_dec = (q.astype(F32) * qs_ref[:, sl]).astype(BF16)
        o = jnp.where(first, intra[0], intra[1]) + _dot_nt(q_dec, st.astype(BF16))
        k_dec = (k.astype(F32) * ks_ref[:, sl]).astype(BF16)
        st_ref[pair] = cd_ref[:, sl] * st + jnp.where(same_head, _dot_tn(v, k_dec), 0.0)

        o = o * lax.rsqrt(_head_mean_sq(o, ones) + EPS) * gain_ref[:, sl]
        g = g_ref[:, sl].astype(F32)
        o_ref[:, sl] = (o * (g * _sigmoid(g))).astype(BF16)


def _retention(pc, dec, qs, ks, cd, gain, ones, batch, seq_len):
    n = pc.shape[0]
    c = C_RET
    tps = seq_len // c
    const2 = lambda b, t: (0, 0)
    blk = lambda j: pl.BlockSpec((c, D_RET), lambda b, t: (b * tps + t, j))
    return pl.pallas_call(
        _ret_kernel,
        grid=(batch, tps),
        in_specs=[
            blk(0), blk(1), blk(2), blk(3),
            pl.BlockSpec((H_RET, c, c), lambda b, t: (0, 0, 0)),
            pl.BlockSpec((c, D_RET), const2),
            pl.BlockSpec((c, D_RET), const2),
            pl.BlockSpec((1, D_RET), const2),
            pl.BlockSpec((1, D_RET), const2),
            pl.BlockSpec((LANES, LANES), const2),
        ],
        out_specs=pl.BlockSpec((c, D_RET), lambda b, t: (b * tps + t, 0)),
        out_shape=jax.ShapeDtypeStruct((n, D_RET), BF16),
        scratch_shapes=[pltpu.VMEM((H_RET // 2, LANES, LANES), F32)],
        compiler_params=pltpu.CompilerParams(
            dimension_semantics=("arbitrary", "arbitrary"), vmem_limit_bytes=VMEM_LIMIT),
        name="retention",
    )(pc, pc, pc, pc, dec, qs, ks, cd, gain, ones)


def _fox_kernel(q_ref, k_ref, v_ref, g_ref, fcol_ref, frow_ref, o_ref, m_scr, l_scr, acc_scr):
    qi = pl.program_id(2)
    tq = q_ref.shape[0]
    q = q_ref[...]
    first = _first_head_lanes((tq, LANES))
    q_heads = (jnp.where(first, q, jnp.zeros_like(q)), jnp.where(first, jnp.zeros_like(q), q))
    f_q = [fcol_ref[0, 0, :, hh:hh + 1] for hh in range(2)]

    m_scr[...] = jnp.full_like(m_scr, -jnp.inf)
    l_scr[...] = jnp.zeros_like(l_scr)
    acc_scr[...] = jnp.zeros_like(acc_scr)

    def step(kj, masked):
        start = pl.multiple_of(kj * tq, tq)
        ks = k_ref[pl.ds(start, tq), :]
        vs = v_ref[pl.ds(start, tq), :]
        if masked:
            row = lax.broadcasted_iota(jnp.int32, (tq, tq), 0)
            col = lax.broadcasted_iota(jnp.int32, (tq, tq), 1)
            keep = row >= col
        for hh in range(2):
            f_k = frow_ref[0, 0, hh:hh + 1, pl.ds(start, tq)]
            s = _dot_nt(q_heads[hh], ks) + (f_q[hh] - f_k)
            if masked:
                s = jnp.where(keep, s, MASK_VALUE)
            m_old = m_scr[hh]
            m_new = jnp.maximum(m_old, jnp.max(s, axis=-1, keepdims=True))
            p = jnp.exp(s - m_new)
            alpha = jnp.exp(m_old - m_new)
            l_scr[hh] = alpha * l_scr[hh] + jnp.sum(p, axis=-1, keepdims=True)
            acc_scr[hh] = alpha * acc_scr[hh] + _dot(p.astype(BF16), vs)
            m_scr[hh] = m_new

    def body(kj, carry):
        step(kj, masked=False)
        return carry

    lax.fori_loop(0, qi, body, 0)
    step(qi, masked=True)

    o = jnp.where(first, acc_scr[0] / l_scr[0], acc_scr[1] / l_scr[1])
    o_ref[...] = (o * _sigmoid(g_ref[...].astype(F32))).astype(BF16)


def _fox(pb, fcol, frow, batch, seq_len):
    n = pb.shape[0]
    tq = TQ_FOX
    nq = seq_len // tq
    pairs = H_FOX // 2
    return pl.pallas_call(
        _fox_kernel,
        grid=(batch, pairs, nq),
        in_specs=[
            pl.BlockSpec((tq, LANES), lambda b, p, i: (b * nq + i, p)),
            pl.BlockSpec((seq_len, LANES), lambda b, p, i: (b, pairs + p)),
            pl.BlockSpec((seq_len, LANES), lambda b, p, i: (b, 2 * pairs + p)),
            pl.BlockSpec((tq, LANES), lambda b, p, i: (b * nq + i, 3 * pairs + p)),
            pl.BlockSpec((1, 1, tq, 2), lambda b, p, i: (b, p, i, 0)),
            pl.BlockSpec((1, 1, 2, seq_len), lambda b, p, i: (b, p, 0, 0)),
        ],
        out_specs=pl.BlockSpec((tq, LANES), lambda b, p, i: (b * nq + i, p)),
        out_shape=jax.ShapeDtypeStruct((n, D_FOX), BF16),
        scratch_shapes=[
            pltpu.VMEM((2, tq, 1), F32),
            pltpu.VMEM((2, tq, 1), F32),
            pltpu.VMEM((2, tq, LANES), F32),
        ],
        compiler_params=pltpu.CompilerParams(
            dimension_semantics=("arbitrary", "arbitrary", "arbitrary"),
            vmem_limit_bytes=VMEM_LIMIT),
        name="fox",
    )(pb, pb, pb, pb, fcol, frow)


def _outproj_kernel(oa_ref, ob_ref, oc_ref, wa_ref, wb_ref, wc_ref, x_ref, g_ref, out_ref):
    mixed = (_dot(oa_ref[...], wa_ref[...]) + _dot(ob_ref[...], wb_ref[...])
             + _dot(oc_ref[...], wc_ref[...]))
    out_ref[...] = x_ref[...] + _rmsnorm_rows(mixed, g_ref[...])


def _outproj(oa, ob, oc, wa, wb, wc, x2, gpost):
    n = x2.shape[0]
    tm = TM_PROJ
    const = lambda i: (0, 0)
    row = lambda i: (i, 0)
    return pl.pallas_call(
        _outproj_kernel,
        grid=(n // tm,),
        in_specs=[
            pl.BlockSpec((tm, D_HGRN), row),
            pl.BlockSpec((tm, D_FOX), row),
            pl.BlockSpec((tm, D_RET), row),
            pl.BlockSpec((D_HGRN, D_MODEL), const),
            pl.BlockSpec((D_FOX, D_MODEL), const),
            pl.BlockSpec((D_RET, D_MODEL), const),
            pl.BlockSpec((tm, D_MODEL), row),
            pl.BlockSpec((1, D_MODEL), const),
        ],
        out_specs=pl.BlockSpec((tm, D_MODEL), row),
        out_shape=jax.ShapeDtypeStruct((n, D_MODEL), F32),
        compiler_params=pltpu.CompilerParams(
            dimension_semantics=("arbitrary",), vmem_limit_bytes=VMEM_LIMIT),
        name="outproj",
    )(oa, ob, oc, wa, wb, wc, x2, gpost)


def _ffn_kernel(x_ref, gpre_ref, wg_ref, wv_ref, cwg_ref, cwv_ref, cbg_ref, cbv_ref, wd_ref,
                gpost_ref, out_ref, h_scr, acc_scr, ug_scr, uv_scr, tail_g, tail_v,
                *, tiles_per_seq):
    i = pl.program_id(0)
    c = pl.program_id(1)
    tm = x_ref.shape[0]
    pad = SUBLANES

    @pl.when(c == 0)
    def _():
        h_scr[...] = _rmsnorm_rows(x_ref[...], gpre_ref[...]).astype(BF16)
        acc_scr[...] = jnp.zeros_like(acc_scr)

    h = h_scr[...]
    seq_start = i % tiles_per_seq == 0

    def causal_conv(u, w_ref, b_ref, scr, tail):
        scr[0:pad, :] = jnp.where(seq_start, 0.0, tail[c])
        scr[pad:pad + tm, :] = u
        tail[c] = u[tm - pad:, :]
        acc = b_ref[...]
        for j in range(CONV_W):
            shift = CONV_W - 1 - j
            acc = acc + w_ref[j:j + 1, :] * scr[pad - shift:pad - shift + tm, :]
        return acc

    gate = causal_conv(_dot(h, wg_ref[...]), cwg_ref, cbg_ref, ug_scr, tail_g)
    val = causal_conv(_dot(h, wv_ref[...]), cwv_ref, cbv_ref, uv_scr, tail_v)
    act = (gate * _sigmoid(gate) * val).astype(BF16)
    acc_scr[...] += _dot(act, wd_ref[...])

    @pl.when(c == pl.num_programs(1) - 1)
    def _():
        out_ref[...] = x_ref[...] + _rmsnorm_rows(acc_scr[...], gpost_ref[...])


def _ffn(x2, gpre, w_up, conv_w, conv_b, w_down, gpost, seq_len):
    n = x2.shape[0]
    tm = TM_FFN
    fc = FC_FFN
    nf = D_FF // fc
    tiles_per_seq = seq_len // tm
    return pl.pallas_call(
        functools.partial(_ffn_kernel, tiles_per_seq=tiles_per_seq),
        grid=(n // tm, nf),
        in_specs=[
            pl.BlockSpec((tm, D_MODEL), lambda i, c: (i, 0)),
            pl.BlockSpec((1, D_MODEL), lambda i, c: (0, 0)),
            pl.BlockSpec((D_MODEL, fc), lambda i, c: (0, c)),
            pl.BlockSpec((D_MODEL, fc), lambda i, c: (0, nf + c)),
            pl.BlockSpec((CONV_W, fc), lambda i, c: (0, c)),
            pl.BlockSpec((CONV_W, fc), lambda i, c: (0, nf + c)),
            pl.BlockSpec((1, fc), lambda i, c: (0, c)),
            pl.BlockSpec((1, fc), lambda i, c: (0, nf + c)),
            pl.BlockSpec((fc, D_MODEL), lambda i, c: (c, 0)),
            pl.BlockSpec((1, D_MODEL), lambda i, c: (0, 0)),
        ],
        out_specs=pl.BlockSpec((tm, D_MODEL), lambda i, c: (i, 0)),
        out_shape=jax.ShapeDtypeStruct((n, D_MODEL), F32),
        scratch_shapes=[
            pltpu.VMEM((tm, D_MODEL), BF16),
            pltpu.VMEM((tm, D_MODEL), F32),
            pltpu.VMEM((tm + SUBLANES, fc), F32),
            pltpu.VMEM((tm + SUBLANES, fc), F32),
            pltpu.VMEM((nf, SUBLANES, fc), F32),
            pltpu.VMEM((nf, SUBLANES, fc), F32),
        ],
        compiler_params=pltpu.CompilerParams(
            dimension_semantics=("arbitrary", "arbitrary"), vmem_limit_bytes=VMEM_LIMIT),
        name="ffn",
    )(x2, gpre, w_up, w_up, conv_w, conv_w, conv_b, conv_b, w_down, gpost)


def _block_ones(n):
    idx = jnp.arange(n) // HEAD_DIM
    return (idx[:, None] == idx[None, :]).astype(BF16)


def _lower_tri(n, block):
    r = jnp.arange(n)
    return ((r[:, None] >= r[None, :]) & (r[:, None] // block == r[None, :] // block)).astype(BF16)


def _rotary_tables(seq_len):
    inv_freq = 1.0 / (ROPE_BASE ** (jnp.arange(0, HEAD_DIM, 2, dtype=F32) / HEAD_DIM))
    ang = jnp.arange(seq_len, dtype=F32)[:, None] * inv_freq[None, :]
    cos, sin = jnp.cos(ang), jnp.sin(ang)
    cos_full = jnp.tile(jnp.concatenate([cos, cos], axis=-1), (1, H_RET))
    sin_signed = jnp.tile(jnp.concatenate([-sin, sin], axis=-1), (1, H_RET))
    return cos_full, sin_signed


def _retention_tables(c):
    log_gamma = jnp.log1p(-jnp.exp2(-5.0 - jnp.arange(H_RET, dtype=F32)))
    pos = jnp.arange(c, dtype=F32)
    rel = pos[:, None] - pos[None, :]
    dec = jnp.where(rel >= 0, jnp.exp(log_gamma[:, None, None] * jnp.maximum(rel, 0.0)), 0.0)
    per_lane = lambda a: jnp.repeat(a, HEAD_DIM, axis=-1)
    qs = per_lane(jnp.exp(log_gamma[None, :] * (pos[:, None] + 1.0)))
    ks = per_lane(jnp.exp(log_gamma[None, :] * (c - 1.0 - pos[:, None])))
    cd = per_lane(jnp.exp(log_gamma * c)[None, :])
    return dec, qs, ks, cd


def _reorder_in_weight(w):
    widths = [D_HGRN] * 4 + [D_FOX] * 3 + [H_FOX] + [D_FOX] + [D_RET] * 4
    offs = [0]
    for wd in widths:
        offs.append(offs[-1] + wd)
    part = lambda j: w[:, offs[j]:offs[j + 1]]
    a_q, a_f, a_i, a_g, b_q, b_k, b_v, b_f, b_g, c_q, c_k, c_v, c_g = (part(j) for j in range(13))
    b_f = jnp.pad(b_f, ((0, 0), (0, FGATE_PAD - H_FOX)))
    return jnp.concatenate(
        [a_q, a_i, a_g, a_f, b_q, b_k, b_v, b_g, b_f, c_q, c_k, c_v, c_g], axis=-1).astype(BF16)


def kernel(x, w_in, b_fox_f, fox_q_gain, fox_k_gain, hgrn_lb, hgrn_out_gain, ret_out_gain, w_out,
           g_mix_pre, g_mix_post, w_up, conv_w, conv_b, w_down, g_ffn_pre, g_ffn_post):
    batch, seq_len, d_model = x.shape
    assert d_model == D_MODEL
    assert seq_len % TM_FFN == 0 and seq_len % TQ_FOX == 0
    depth = w_in.shape[0]
    n = batch * seq_len
    pairs = H_FOX // 2

    cos_full, sin_signed = _rotary_tables(seq_len)
    dec, qs, ks, cd = _retention_tables(C_RET)
    tri_proj = _lower_tri(TM_PROJ, TM_PROJ)
    tri_hgrn = _lower_tri(TT_HGRN, SUB_HGRN)
    ones_fox = _block_ones(D_FOX)
    ones_pair = _block_ones(LANES)
    lb = hgrn_lb.astype(F32)

    x2 = x.reshape(n, D_MODEL)
    for l in range(depth):
        w = _reorder_in_weight(w_in[l])
        bf = jnp.pad(b_fox_f[l].astype(F32), (0, FGATE_PAD - H_FOX))[None, :]
        qg = jnp.tile(fox_q_gain[l].astype(F32), H_FOX)[None, :]
        kg = jnp.tile(fox_k_gain[l].astype(F32), H_FOX)[None, :]
        pa, za, pb, fcum, pc = _inproj(x2, g_mix_pre[l][None, :], w, bf, qg, kg, cos_full,
                                       sin_signed, tri_proj, ones_fox, seq_len)

        o_a = _hgrn(pa, za, lb, hgrn_out_gain[l].reshape(1, D_HGRN), tri_hgrn, ones_pair,
                    batch, seq_len, l)
        f_heads = fcum[:, :H_FOX].reshape(batch, seq_len, pairs, 2)
        o_b = _fox(pb, f_heads.transpose(0, 2, 1, 3), f_heads.transpose(0, 2, 3, 1),
                   batch, seq_len)
        o_c = _retention(pc, dec, qs, ks, cd, ret_out_gain[l].reshape(1, D_RET), ones_pair,
                         batch, seq_len)

        wo = w_out[l].astype(BF16)
        x2 = _outproj(o_a, o_b, o_c, wo[:D_HGRN], wo[D_HGRN:D_HGRN + D_FOX],
                      wo[D_HGRN + D_FOX:], x2, g_mix_post[l][None, :])
        x2 = _ffn(x2, g_ffn_pre[l][None, :], w_up[l].astype(BF16), conv_w[l], conv_b[l][None, :],
                  w_down[l].astype(BF16), g_ffn_post[l][None, :], seq_len)
    return x2.reshape(batch, seq_len, D_MODEL)
```

```python
import functools

import jax
import jax.numpy as jnp
from jax import lax
from jax.experimental import pallas as pl
from jax.experimental.pallas import tpu as pltpu

F32 = jnp.float32
BF16 = jnp.bfloat16

D_MODEL = 1024
HEAD_DIM = 64
H_HGRN, H_FOX, H_RET = 6, 6, 4
D_HGRN, D_FOX, D_RET = H_HGRN * HEAD_DIM, H_FOX * HEAD_DIM, H_RET * HEAD_DIM
D_FF = 2816
CONV_W = 3
ROPE_BASE = 10000.0
EPS = 1e-6
MIN_FORGET = 1e-12
MASK_VALUE = -1e30

LANES = 128
SUBLANES = 8
FGATE_PAD = LANES

_A_MAIN = (0, 3 * D_HGRN)
_A_GATE = (_A_MAIN[1], _A_MAIN[1] + D_HGRN)
_B_Q = (_A_GATE[1], _A_GATE[1] + D_FOX)
_B_K = (_B_Q[1], _B_Q[1] + D_FOX)
_B_V = (_B_K[1], _B_K[1] + D_FOX)
_B_G = (_B_V[1], _B_V[1] + D_FOX)
_B_F = (_B_G[1], _B_G[1] + FGATE_PAD)
_C_Q = (_B_F[1], _B_F[1] + D_RET)
_C_K = (_C_Q[1], _C_Q[1] + D_RET)
_C_VG = (_C_K[1], _C_K[1] + 2 * D_RET)
IN_COLS_PADDED = _C_VG[1]

TM_PROJ = 512
TM_FFN = 1024
FC_FFN = 256
TT_HGRN = 256
SUB_HGRN = 32
C_RET = 256
TQ_FOX = 512
VMEM_LIMIT = 48 * 1024 * 1024


def _dot(a, b):
    return jnp.dot(a, b, preferred_element_type=F32)


def _dot_nt(a, b):
    return lax.dot_general(a, b, (((1,), (1,)), ((), ())), preferred_element_type=F32)


def _dot_tn(a, b):
    return lax.dot_general(a, b, (((0,), (0,)), ((), ())), preferred_element_type=F32)


def _dot_01_lhs(m01, x):
    hi = x.astype(BF16)
    r = x - hi.astype(F32)
    mid = r.astype(BF16)
    lo = (r - mid.astype(F32)).astype(BF16)
    return _dot(m01, hi) + _dot(m01, mid) + _dot(m01, lo)


def _head_mean_sq(y, ones_bd):
    sq = y * y
    hi = sq.astype(BF16)
    lo = (sq - hi.astype(F32)).astype(BF16)
    return (_dot(hi, ones_bd) + _dot(lo, ones_bd)) * (1.0 / HEAD_DIM)


def _rmsnorm_rows(x, gain):
    return x * lax.rsqrt(jnp.mean(x * x, axis=-1, keepdims=True) + EPS) * gain


def _sigmoid(x):
    return 1.0 / (1.0 + jnp.exp(-x))


def _first_head_lanes(shape):
    return (lax.broadcasted_iota(jnp.int32, shape, len(shape) - 1) & HEAD_DIM) == 0


def _bf16_terms(x):
    hi = x.astype(BF16).astype(F32)
    mid = (x - hi).astype(BF16).astype(F32)
    lo = (x - hi - mid).astype(BF16).astype(F32)
    return hi, mid, lo


def _inproj_kernel(x_ref, gpre_ref, w_ref, bf_ref, qg_ref, kg_ref, cos_ref, sin_ref, tri_ref,
                   ones_ref, pa_ref, za_ref, qt_ref, ka_ref, vt_ref, gb_ref, pc_ref, carry_ref,
                   *, tiles_per_seq):
    i = pl.program_id(0)
    h = _rmsnorm_rows(x_ref[...], gpre_ref[...]).astype(BF16)
    tm = h.shape[0]

    def proj(cols):
        return _dot(h, w_ref[:, cols[0]:cols[1]])

    pa_ref[...] = proj(_A_MAIN).astype(BF16)
    za_ref[...] = proj(_A_GATE)

    zf = proj(_B_F) + bf_ref[...]
    log_f = -(jnp.maximum(-zf, 0.0) + jnp.log1p(jnp.exp(-jnp.abs(zf))))

    @pl.when(i % tiles_per_seq == 0)
    def _():
        carry_ref[...] = jnp.zeros_like(carry_ref)

    cum = _dot_01_lhs(tri_ref[...], log_f) + carry_ref[...]
    carry_ref[...] = cum[tm - 1:, :]

    ones = ones_ref[...]
    bq = proj(_B_Q)
    bk = proj(_B_K)
    lane = lax.broadcasted_iota(jnp.int32, (tm, LANES), 1)
    is_feature = lane < HEAD_DIM
    is_one_q = (lane >= HEAD_DIM + 3) & (lane < HEAD_DIM + 6)
    is_one_k = (lane >= HEAD_DIM) & (lane < HEAD_DIM + 3)
    for pair in range(H_FOX // 2):
        sl = slice(pair * LANES, (pair + 1) * LANES)
        q2 = bq[:, sl]
        q2 = q2 * lax.rsqrt(_head_mean_sq(q2, ones) + EPS) * qg_ref[:, sl] * (HEAD_DIM ** -0.5)
        k2 = bk[:, sl]
        k2 = k2 * lax.rsqrt(_head_mean_sq(k2, ones) + EPS) * kg_ref[:, sl]
        for odd in range(2):
            head = 2 * pair + odd
            slot = slice(head * LANES, (head + 1) * LANES)
            qh = pltpu.roll(q2, HEAD_DIM, 1) if odd else q2
            kh = pltpu.roll(k2, HEAD_DIM, 1) if odd else k2
            hi, mid, lo = _bf16_terms(jnp.broadcast_to(cum[:, head:head + 1], (tm, LANES)))
            q_bias = jnp.where(lane == HEAD_DIM, hi, jnp.where(lane == HEAD_DIM + 1, mid, jnp.where(
                lane == HEAD_DIM + 2, lo, jnp.where(is_one_q, 1.0, 0.0))))
            k_bias = jnp.where(lane == HEAD_DIM + 3, -hi, jnp.where(lane == HEAD_DIM + 4, -mid, jnp.where(
                lane == HEAD_DIM + 5, -lo, jnp.where(is_one_k, 1.0, 0.0))))
            qt_ref[0, slot, :] = jnp.where(is_feature, qh, q_bias).T.astype(BF16)
            ka_ref[:, slot] = jnp.where(is_feature, kh, k_bias).astype(BF16)
    vt_ref[0] = proj(_B_V).T.astype(BF16)
    gb_ref[...] = proj(_B_G).astype(BF16)

    cos = cos_ref[...]
    sin = sin_ref[...]
    low_half = (lax.broadcasted_iota(jnp.int32, cos.shape, 1) & (HEAD_DIM // 2)) == 0

    def rotary(y):
        swapped = jnp.where(low_half, pltpu.roll(y, D_RET - HEAD_DIM // 2, 1),
                            pltpu.roll(y, HEAD_DIM // 2, 1))
        return y * cos + swapped * sin

    pc_ref[:, 0:D_RET] = rotary(proj(_C_Q)).astype(BF16)
    pc_ref[:, D_RET:2 * D_RET] = (rotary(proj(_C_K)) * (HEAD_DIM ** -0.5)).astype(BF16)
    pc_ref[:, 2 * D_RET:4 * D_RET] = proj(_C_VG).astype(BF16)


def _inproj(x2, gpre, w, bf, qg, kg, cos, sin, tri, ones, seq_len):
    n = x2.shape[0]
    batch = n // seq_len
    tm = TM_PROJ
    tiles_per_seq = seq_len // tm
    const = lambda i: (0, 0)
    row = lambda i: (i, 0)
    seq_t = lambda i: (i // tiles_per_seq, 0, i % tiles_per_seq)
    return pl.pallas_call(
        functools.partial(_inproj_kernel, tiles_per_seq=tiles_per_seq),
        grid=(n // tm,),
        in_specs=[
            pl.BlockSpec((tm, D_MODEL), row),
            pl.BlockSpec((1, D_MODEL), const),
            pl.BlockSpec((D_MODEL, IN_COLS_PADDED), const),
            pl.BlockSpec((1, FGATE_PAD), const),
            pl.BlockSpec((1, D_FOX), const),
            pl.BlockSpec((1, D_FOX), const),
            pl.BlockSpec((tm, D_RET), lambda i: (i % tiles_per_seq, 0)),
            pl.BlockSpec((tm, D_RET), lambda i: (i % tiles_per_seq, 0)),
            pl.BlockSpec((tm, tm), const),
            pl.BlockSpec((LANES, LANES), const),
        ],
        out_specs=[
            pl.BlockSpec((tm, 3 * D_HGRN), row),
            pl.BlockSpec((tm, D_HGRN), row),
            pl.BlockSpec((1, H_FOX * LANES, tm), seq_t),
            pl.BlockSpec((tm, H_FOX * LANES), row),
            pl.BlockSpec((1, D_FOX, tm), seq_t),
            pl.BlockSpec((tm, D_FOX), row),
            pl.BlockSpec((tm, 4 * D_RET), row),
        ],
        out_shape=[
            jax.ShapeDtypeStruct((n, 3 * D_HGRN), BF16),
            jax.ShapeDtypeStruct((n, D_HGRN), F32),
            jax.ShapeDtypeStruct((batch, H_FOX * LANES, seq_len), BF16),
            jax.ShapeDtypeStruct((n, H_FOX * LANES), BF16),
            jax.ShapeDtypeStruct((batch, D_FOX, seq_len), BF16),
            jax.ShapeDtypeStruct((n, D_FOX), BF16),
            jax.ShapeDtypeStruct((n, 4 * D_RET), BF16),
        ],
        scratch_shapes=[pltpu.VMEM((1, FGATE_PAD), F32)],
        compiler_params=pltpu.CompilerParams(
            dimension_semantics=("arbitrary",), vmem_limit_bytes=VMEM_LIMIT),
        name="inproj",
    )(x2, gpre, w, bf, qg, kg, cos, sin, tri, ones)


def _hgrn_kernel(q_ref, v_ref, g_ref, z_ref, lb_ref, gain_ref, tri_ref, ones_ref, o_ref, st_ref,
                 *, layer):
    @pl.when(pl.program_id(1) == 0)
    def _():
        st_ref[...] = jnp.zeros_like(st_ref)

    tt = q_ref.shape[0]
    sub = SUB_HGRN

    lb = lb_ref[...]
    e = jnp.exp(lb - jnp.max(lb, axis=0, keepdims=True))
    p = e / jnp.sum(e, axis=0, keepdims=True)
    run = p[0:1]
    for i in range(1, layer + 1):
        run = run + p[i:i + 1]
    lower = run - p[0:1]

    row = lax.broadcasted_iota(jnp.int32, (tt, tt), 0)
    col = lax.broadcasted_iota(jnp.int32, (tt, tt), 1)
    causal = (col <= row) & (col >= (row & ~(sub - 1)))
    first = _first_head_lanes((tt, LANES))
    rr = lax.broadcasted_iota(jnp.int32, (LANES, LANES), 0)
    cc = lax.broadcasted_iota(jnp.int32, (LANES, LANES), 1)
    same_head = ((rr ^ cc) & HEAD_DIM) == 0
    tri = tri_ref[...]
    ones = ones_ref[...]

    for pair in range(H_HGRN // 2):
        sl = slice(pair * LANES, (pair + 1) * LANES)
        lw = lower[:, sl]
        sg = _sigmoid(z_ref[:, sl])
        f = lw + (1.0 - lw) * sg
        log_f = jnp.log(jnp.maximum(f, MIN_FORGET))
        k = (1.0 - lw) * (1.0 - sg)
        cum = _dot_01_lhs(tri, log_f)
        e_q = jnp.exp(cum)
        qt = (q_ref[:, sl].astype(F32) * e_q).astype(BF16)
        kt = (k * jnp.exp(-cum)).astype(BF16)
        v = v_ref[:, sl]

        intra = []
        for keep in (first, ~first):
            s = _dot_nt(jnp.where(keep, qt, jnp.zeros_like(qt)), kt)
            intra.append(_dot(jnp.where(causal, s, 0.0).astype(BF16), v))
        o = jnp.where(first, intra[0], intra[1])

        st = st_ref[pair]
        outs = []
        for j in range(tt // sub):
            r = slice(j * sub, (j + 1) * sub)
            outs.append(o[r] + _dot_nt(qt[r], st.astype(BF16)))
            upd = jnp.where(same_head, _dot_tn(v[r], kt[r]), 0.0)
            st = e_q[(j + 1) * sub - 1:(j + 1) * sub, :] * (st + upd)
        st_ref[pair] = st
        o = jnp.concatenate(outs, axis=0)

        o = o * lax.rsqrt(_head_mean_sq(o, ones) + EPS) * gain_ref[:, sl]
        g = g_ref[:, sl].astype(F32)
        o_ref[:, sl] = (o * (g * _sigmoid(g))).astype(BF16)


def _hgrn(pa, za, lb, gain, tri, ones, batch, seq_len, layer):
    n = pa.shape[0]
    tt = TT_HGRN
    tps = seq_len // tt
    const = lambda b, t: (0, 0)
    return pl.pallas_call(
        functools.partial(_hgrn_kernel, layer=layer),
        grid=(batch, tps),
        in_specs=[
            pl.BlockSpec((tt, D_HGRN), lambda b, t: (b * tps + t, 0)),
            pl.BlockSpec((tt, D_HGRN), lambda b, t: (b * tps + t, 1)),
            pl.BlockSpec((tt, D_HGRN), lambda b, t: (b * tps + t, 2)),
            pl.BlockSpec((tt, D_HGRN), lambda b, t: (b * tps + t, 0)),
            pl.BlockSpec(lb.shape, const),
            pl.BlockSpec((1, D_HGRN), const),
            pl.BlockSpec((tt, tt), const),
            pl.BlockSpec((LANES, LANES), const),
        ],
        out_specs=pl.BlockSpec((tt, D_HGRN), lambda b, t: (b * tps + t, 0)),
        out_shape=jax.ShapeDtypeStruct((n, D_HGRN), BF16),
        scratch_shapes=[pltpu.VMEM((H_HGRN // 2, LANES, LANES), F32)],
        compiler_params=pltpu.CompilerParams(
            dimension_semantics=("arbitrary", "arbitrary"), vmem_limit_bytes=VMEM_LIMIT),
        name="hgrn",
    )(pa, pa, pa, za, lb, gain, tri, ones)


def _ret_kernel(q_ref, k_ref, v_ref, g_ref, dec_ref, qs_ref, ks_ref, cd_ref, gain_ref, ones_ref,
                o_ref, st_ref):
    @pl.when(pl.program_id(1) == 0)
    def _():
        st_ref[...] = jnp.zeros_like(st_ref)

    c = q_ref.shape[0]
    first = _first_head_lanes((c, LANES))
    rr = lax.broadcasted_iota(jnp.int32, (LANES, LANES), 0)
    cc = lax.broadcasted_iota(jnp.int32, (LANES, LANES), 1)
    same_head = ((rr ^ cc) & HEAD_DIM) == 0
    ones = ones_ref[...]

    for pair in range(H_RET // 2):
        sl = slice(pair * LANES, (pair + 1) * LANES)
        q = q_ref[:, sl]
        k = k_ref[:, sl]
        v = v_ref[:, sl]
        intra = []
        for hh, keep in enumerate((first, ~first)):
            s = _dot_nt(jnp.where(keep, q, jnp.zeros_like(q)), k) * dec_ref[2 * pair + hh]
            intra.append(_dot(s.astype(BF16), v))
        st = st_ref[pair]
        q_dec = (q.astype(F32) * qs_ref[:, sl]).astype(BF16)
        o = jnp.where(first, intra[0], intra[1]) + _dot_nt(q_dec, st.astype(BF16))
        k_dec = (k.astype(F32) * ks_ref[:, sl]).astype(BF16)
        st_ref[pair] = cd_ref[:, sl] * st + jnp.where(same_head, _dot_tn(v, k_dec), 0.0)

        o = o * lax.rsqrt(_head_mean_sq(o, ones) + EPS) * gain_ref[:, sl]
        g = g_ref[:, sl].astype(F32)
        o_ref[:, sl] = (o * (g * _sigmoid(g))).astype(BF16)


def _retention(pc, dec, qs, ks, cd, gain, ones, batch, seq_len):
    n = pc.shape[0]
    c = C_RET
    tps = seq_len // c
    const2 = lambda b, t: (0, 0)
    blk = lambda j: pl.BlockSpec((c, D_RET), lambda b, t: (b * tps + t, j))
    return pl.pallas_call(
        _ret_kernel,
        grid=(batch, tps),
        in_specs=[
            blk(0), blk(1), blk(2), blk(3),
            pl.BlockSpec((H_RET, c, c), lambda b, t: (0, 0, 0)),
            pl.BlockSpec((c, D_RET), const2),
            pl.BlockSpec((c, D_RET), const2),
            pl.BlockSpec((1, D_RET), const2),
            pl.BlockSpec((1, D_RET), const2),
            pl.BlockSpec((LANES, LANES), const2),
        ],
        out_specs=pl.BlockSpec((c, D_RET), lambda b, t: (b * tps + t, 0)),
        out_shape=jax.ShapeDtypeStruct((n, D_RET), BF16),
        scratch_shapes=[pltpu.VMEM((H_RET // 2, LANES, LANES), F32)],
        compiler_params=pltpu.CompilerParams(
            dimension_semantics=("arbitrary", "arbitrary"), vmem_limit_bytes=VMEM_LIMIT),
        name="retention",
    )(pc, pc, pc, pc, dec, qs, ks, cd, gain, ones)


def _fox_kernel(qt_ref, k_ref, vt_ref, g_ref, o_ref, m_scr, l_scr, acc_scr):
    qi = pl.program_id(2)
    tq = o_ref.shape[0]

    m_scr[...] = jnp.full_like(m_scr, -jnp.inf)
    l_scr[...] = jnp.zeros_like(l_scr)
    acc_scr[...] = jnp.zeros_like(acc_scr)

    def step(kj, masked):
        start = pl.multiple_of(kj * tq, tq)
        if masked:
            key = lax.broadcasted_iota(jnp.int32, (tq, tq), 0)
            qry = lax.broadcasted_iota(jnp.int32, (tq, tq), 1)
            keep = qry >= key
        for hh in range(2):
            slot = slice(hh * LANES, (hh + 1) * LANES)
            s = _dot(k_ref[pl.ds(start, tq), slot], qt_ref[0, slot, :])
            if masked:
                s = jnp.where(keep, s, MASK_VALUE)
            m_old = m_scr[hh]
            m_new = jnp.maximum(m_old, jnp.max(s, axis=0, keepdims=True))
            p = jnp.exp(s - m_new)
            alpha = jnp.exp(m_old - m_new)
            l_scr[hh] = alpha * l_scr[hh] + jnp.sum(p, axis=0, keepdims=True)
            v_t = vt_ref[0, hh * HEAD_DIM:(hh + 1) * HEAD_DIM, pl.ds(start, tq)]
            acc_scr[hh] = alpha * acc_scr[hh] + _dot(v_t, p.astype(BF16))
            m_scr[hh] = m_new

    def body(kj, carry):
        step(kj, masked=False)
        return carry

    lax.fori_loop(0, qi, body, 0)
    step(qi, masked=True)

    o_t = jnp.concatenate([acc_scr[0] / l_scr[0], acc_scr[1] / l_scr[1]], axis=0)
    o_ref[...] = (o_t.T * _sigmoid(g_ref[...].astype(F32))).astype(BF16)


def _fox(qt, ka, vt, gb, batch, seq_len):
    n = ka.shape[0]
    tq = TQ_FOX
    nq = seq_len // tq
    pairs = H_FOX // 2
    return pl.pallas_call(
        _fox_kernel,
        grid=(batch, pairs, nq),
        in_specs=[
            pl.BlockSpec((1, 2 * LANES, tq), lambda b, p, i: (b, p, i)),
            pl.BlockSpec((seq_len, 2 * LANES), lambda b, p, i: (b, p)),
            pl.BlockSpec((1, LANES, seq_len), lambda b, p, i: (b, p, 0)),
            pl.BlockSpec((tq, LANES), lambda b, p, i: (b * nq + i, p)),
        ],
        out_specs=pl.BlockSpec((tq, LANES), lambda b, p, i: (b * nq + i, p)),
        out_shape=jax.ShapeDtypeStruct((n, D_FOX), BF16),
        scratch_shapes=[
            pltpu.VMEM((2, 1, tq), F32),
            pltpu.VMEM((2, 1, tq), F32),
            pltpu.VMEM((2, HEAD_DIM, tq), F32),
        ],
        compiler_params=pltpu.CompilerParams(
            dimension_semantics=("arbitrary", "arbitrary", "arbitrary"),
            vmem_limit_bytes=VMEM_LIMIT),
        name="fox",
    )(qt, ka, vt, gb)


def _outproj_kernel(oa_ref, ob_ref, oc_ref, wa_ref, wb_ref, wc_ref, x_ref, g_ref, out_ref):
    mixed = (_dot(oa_ref[...], wa_ref[...]) + _dot(ob_ref[...], wb_ref[...])
             + _dot(oc_ref[...], wc_ref[...]))
    out_ref[...] = x_ref[...] + _rmsnorm_rows(mixed, g_ref[...])


def _outproj(oa, ob, oc, wa, wb, wc, x2, gpost):
    n = x2.shape[0]
    tm = TM_PROJ
    const = lambda i: (0, 0)
    row = lambda i: (i, 0)
    return pl.pallas_call(
        _outproj_kernel,
        grid=(n // tm,),
        in_specs=[
            pl.BlockSpec((tm, D_HGRN), row),
            pl.BlockSpec((tm, D_FOX), row),
            pl.BlockSpec((tm, D_RET), row),
            pl.BlockSpec((D_HGRN, D_MODEL), const),
            pl.BlockSpec((D_FOX, D_MODEL), const),
            pl.BlockSpec((D_RET, D_MODEL), const),
            pl.BlockSpec((tm, D_MODEL), row),
            pl.BlockSpec((1, D_MODEL), const),
        ],
        out_specs=pl.BlockSpec((tm, D_MODEL), row),
        out_shape=jax.ShapeDtypeStruct((n, D_MODEL), F32),
        compiler_params=pltpu.CompilerParams(
            dimension_semantics=("arbitrary",), vmem_limit_bytes=VMEM_LIMIT),
        name="outproj",
    )(oa, ob, oc, wa, wb, wc, x2, gpost)


def _ffn_kernel(x_ref, gpre_ref, wg_ref, wv_ref, cwg_ref, cwv_ref, cbg_ref, cbv_ref, wd_ref,
                gpost_ref, out_ref, h_scr, acc_scr, ug_scr, uv_scr, tail_g, tail_v,
                *, tiles_per_seq):
    i = pl.program_id(0)
    c = pl.program_id(1)
    tm = x_ref.shape[0]
    pad = SUBLANES

    @pl.when(c == 0)
    def _():
        h_scr[...] = _rmsnorm_rows(x_ref[...], gpre_ref[...]).astype(BF16)
        acc_scr[...] = jnp.zeros_like(acc_scr)

    h = h_scr[...]
    seq_start = i % tiles_per_seq == 0

    def causal_conv(u, w_ref, b_ref, scr, tail):
        scr[0:pad, :] = jnp.where(seq_start, 0.0, tail[c])
        scr[pad:pad + tm, :] = u
        tail[c] = u[tm - pad:, :]
        acc = b_ref[...]
        for j in range(CONV_W):
            shift = CONV_W - 1 - j
            acc = acc + w_ref[j:j + 1, :] * scr[pad - shift:pad - shift + tm, :]
        return acc

    gate = causal_conv(_dot(h, wg_ref[...]), cwg_ref, cbg_ref, ug_scr, tail_g)
    val = causal_conv(_dot(h, wv_ref[...]), cwv_ref, cbv_ref, uv_scr, tail_v)
    act = (gate * _sigmoid(gate) * val).astype(BF16)
    acc_scr[...] += _dot(act, wd_ref[...])

    @pl.when(c == pl.num_programs(1) - 1)
    def _():
        out_ref[...] = x_ref[...] + _rmsnorm_rows(acc_scr[...], gpost_ref[...])


def _ffn(x2, gpre, w_up, conv_w, conv_b, w_down, gpost, seq_len):
    n = x2.shape[0]
    tm = TM_FFN
    fc = FC_FFN
    nf = D_FF // fc
    tiles_per_seq = seq_len // tm
    return pl.pallas_call(
        functools.partial(_ffn_kernel, tiles_per_seq=tiles_per_seq),
        grid=(n // tm, nf),
        in_specs=[
            pl.BlockSpec((tm, D_MODEL), lambda i, c: (i, 0)),
            pl.BlockSpec((1, D_MODEL), lambda i, c: (0, 0)),
            pl.BlockSpec((D_MODEL, fc), lambda i, c: (0, c)),
            pl.BlockSpec((D_MODEL, fc), lambda i, c: (0, nf + c)),
            pl.BlockSpec((CONV_W, fc), lambda i, c: (0, c)),
            pl.BlockSpec((CONV_W, fc), lambda i, c: (0, nf + c)),
            pl.BlockSpec((1, fc), lambda i, c: (0, c)),
            pl.BlockSpec((1, fc), lambda i, c: (0, nf + c)),
            pl.BlockSpec((fc, D_MODEL), lambda i, c: (c, 0)),
            pl.BlockSpec((1, D_MODEL), lambda i, c: (0, 0)),
        ],
        out_specs=pl.BlockSpec((tm, D_MODEL), lambda i, c: (i, 0)),
        out_shape=jax.ShapeDtypeStruct((n, D_MODEL), F32),
        scratch_shapes=[
            pltpu.VMEM((tm, D_MODEL), BF16),
            pltpu.VMEM((tm, D_MODEL), F32),
            pltpu.VMEM((tm + SUBLANES, fc), F32),
            pltpu.VMEM((tm + SUBLANES, fc), F32),
            pltpu.VMEM((nf, SUBLANES, fc), F32),
            pltpu.VMEM((nf, SUBLANES, fc), F32),
        ],
        compiler_params=pltpu.CompilerParams(
            dimension_semantics=("arbitrary", "arbitrary"), vmem_limit_bytes=VMEM_LIMIT),
        name="ffn",
    )(x2, gpre, w_up, w_up, conv_w, conv_w, conv_b, conv_b, w_down, gpost)


def _block_ones(n):
    idx = jnp.arange(n) // HEAD_DIM
    return (idx[:, None] == idx[None, :]).astype(BF16)


def _lower_tri(n, block):
    r = jnp.arange(n)
    return ((r[:, None] >= r[None, :]) & (r[:, None] // block == r[None, :] // block)).astype(BF16)


def _rotary_tables(seq_len):
    inv_freq = 1.0 / (ROPE_BASE ** (jnp.arange(0, HEAD_DIM, 2, dtype=F32) / HEAD_DIM))
    ang = jnp.arange(seq_len, dtype=F32)[:, None] * inv_freq[None, :]
    cos, sin = jnp.cos(ang), jnp.sin(ang)
    cos_full = jnp.tile(jnp.concatenate([cos, cos], axis=-1), (1, H_RET))
    sin_signed = jnp.tile(jnp.concatenate([-sin, sin], axis=-1), (1, H_RET))
    return cos_full, sin_signed


def _retention_tables(c):
    log_gamma = jnp.log1p(-jnp.exp2(-5.0 - jnp.arange(H_RET, dtype=F32)))
    pos = jnp.arange(c, dtype=F32)
    rel = pos[:, None] - pos[None, :]
    dec = jnp.where(rel >= 0, jnp.exp(log_gamma[:, None, None] * jnp.maximum(rel, 0.0)), 0.0)
    per_lane = lambda a: jnp.repeat(a, HEAD_DIM, axis=-1)
    qs = per_lane(jnp.exp(log_gamma[None, :] * (pos[:, None] + 1.0)))
    ks = per_lane(jnp.exp(log_gamma[None, :] * (c - 1.0 - pos[:, None])))
    cd = per_lane(jnp.exp(log_gamma * c)[None, :])
    return dec, qs, ks, cd


def _reorder_in_weight(w):
    widths = [D_HGRN] * 4 + [D_FOX] * 3 + [H_FOX] + [D_FOX] + [D_RET] * 4
    offs = [0]
    for wd in widths:
        offs.append(offs[-1] + wd)
    part = lambda j: w[:, offs[j]:offs[j + 1]]
    a_q, a_f, a_i, a_g, b_q, b_k, b_v, b_f, b_g, c_q, c_k, c_v, c_g = (part(j) for j in range(13))
    b_f = jnp.pad(b_f, ((0, 0), (0, FGATE_PAD - H_FOX)))
    return jnp.concatenate(
        [a_q, a_i, a_g, a_f, b_q, b_k, b_v, b_g, b_f, c_q, c_k, c_v, c_g], axis=-1).astype(BF16)


def kernel(x, w_in, b_fox_f, fox_q_gain, fox_k_gain, hgrn_lb, hgrn_out_gain, ret_out_gain, w_out,
           g_mix_pre, g_mix_post, w_up, conv_w, conv_b, w_down, g_ffn_pre, g_ffn_post):
    batch, seq_len, d_model = x.shape
    assert d_model == D_MODEL
    assert seq_len % TM_FFN == 0 and seq_len % TQ_FOX == 0
    depth = w_in.shape[0]
    n = batch * seq_len

    cos_full, sin_signed = _rotary_tables(seq_len)
    dec, qs, ks, cd = _retention_tables(C_RET)
    tri_proj = _lower_tri(TM_PROJ, TM_PROJ)
    tri_hgrn = _lower_tri(TT_HGRN, SUB_HGRN)
    ones_pair = _block_ones(LANES)
    lb = hgrn_lb.astype(F32)

    x2 = x.reshape(n, D_MODEL)
    for l in range(depth):
        w = _reorder_in_weight(w_in[l])
        bf = jnp.pad(b_fox_f[l].astype(F32), (0, FGATE_PAD - H_FOX))[None, :]
        qg = jnp.tile(fox_q_gain[l].astype(F32), H_FOX)[None, :]
        kg = jnp.tile(fox_k_gain[l].astype(F32), H_FOX)[None, :]
        pa, za, qt, ka, vt, gb, pc = _inproj(x2, g_mix_pre[l][None, :], w, bf, qg, kg, cos_full,
                                             sin_signed, tri_proj, ones_pair, seq_len)

        o_a = _hgrn(pa, za, lb, hgrn_out_gain[l].reshape(1, D_HGRN), tri_hgrn, ones_pair,
                    batch, seq_len, l)
        o_b = _fox(qt, ka, vt, gb, batch, seq_len)
        o_c = _retention(pc, dec, qs, ks, cd, ret_out_gain[l].reshape(1, D_RET), ones_pair,
                         batch, seq_len)

        wo = w_out[l].astype(BF16)
        x2 = _outproj(o_a, o_b, o_c, wo[:D_HGRN], wo[D_HGRN:D_HGRN + D_FOX],
                      wo[D_HGRN + D_FOX:], x2, g_mix_post[l][None, :])
        x2 = _ffn(x2, g_ffn_pre[l][None, :], w_up[l].astype(BF16), conv_w[l], conv_b[l][None, :],
                  w_down[l].astype(BF16), g_ffn_post[l][None, :], seq_len)
    return x2.reshape(batch, seq_len, D_MODEL)
```

```python
import functools

import jax
import jax.numpy as jnp
from jax import lax
from jax.experimental import pallas as pl
from jax.experimental.pallas import tpu as pltpu

F32 = jnp.float32
BF16 = jnp.bfloat16

D_MODEL = 1024
HEAD_DIM = 64
H_HGRN, H_FOX, H_RET = 6, 6, 4
D_HGRN, D_FOX, D_RET = H_HGRN * HEAD_DIM, H_FOX * HEAD_DIM, H_RET * HEAD_DIM
D_FF = 2816
CONV_W = 3
ROPE_BASE = 10000.0
EPS = 1e-6
MIN_FORGET = 1e-12
MASK_VALUE = -1e30

LANES = 128
SUBLANES = 8
FGATE_PAD = LANES
BF16_SUBLANES = 16
VT_ROWS = HEAD_DIM + BF16_SUBLANES
LOG2E = 1.4426950408889634

_A_MAIN = (0, 3 * D_HGRN)
_A_GATE = (_A_MAIN[1], _A_MAIN[1] + D_HGRN)
_B_Q = (_A_GATE[1], _A_GATE[1] + D_FOX)
_B_K = (_B_Q[1], _B_Q[1] + D_FOX)
_B_V = (_B_K[1], _B_K[1] + D_FOX)
_B_G = (_B_V[1], _B_V[1] + D_FOX)
_B_F = (_B_G[1], _B_G[1] + FGATE_PAD)
_C_Q = (_B_F[1], _B_F[1] + D_RET)
_C_K = (_C_Q[1], _C_Q[1] + D_RET)
_C_VG = (_C_K[1], _C_K[1] + 2 * D_RET)
IN_COLS_PADDED = _C_VG[1]

TM_PROJ = 512
TM_FFN = 512
FC_FFN = 256
TT_HGRN = 256
SUB_HGRN = 32
C_RET = 256
TQ_FOX = 512
VMEM_LIMIT = 48 * 1024 * 1024


def _dot(a, b):
    return jnp.dot(a, b, preferred_element_type=F32)


def _dot_nt(a, b):
    return lax.dot_general(a, b, (((1,), (1,)), ((), ())), preferred_element_type=F32)


def _dot_tn(a, b):
    return lax.dot_general(a, b, (((0,), (0,)), ((), ())), preferred_element_type=F32)


def _dot_01_lhs(m01, x):
    hi = x.astype(BF16)
    r = x - hi.astype(F32)
    mid = r.astype(BF16)
    lo = (r - mid.astype(F32)).astype(BF16)
    return _dot(m01, hi) + _dot(m01, mid) + _dot(m01, lo)


def _head_mean_sq(y, ones_bd):
    sq = y * y
    hi = sq.astype(BF16)
    lo = (sq - hi.astype(F32)).astype(BF16)
    return (_dot(hi, ones_bd) + _dot(lo, ones_bd)) * (1.0 / HEAD_DIM)


def _rmsnorm_rows(x, gain):
    return x * lax.rsqrt(jnp.mean(x * x, axis=-1, keepdims=True) + EPS) * gain


def _sigmoid(x):
    return 1.0 / (1.0 + jnp.exp(-x))


def _first_head_lanes(shape):
    return (lax.broadcasted_iota(jnp.int32, shape, len(shape) - 1) & HEAD_DIM) == 0


def _bf16_terms(x):
    hi = x.astype(BF16).astype(F32)
    mid = (x - hi).astype(BF16).astype(F32)
    lo = (x - hi - mid).astype(BF16).astype(F32)
    return hi, mid, lo


def _inproj_kernel(x_ref, gpre_ref, w_ref, bf_ref, qg_ref, kg_ref, cos_ref, sin_ref, tri_ref,
                   ones_ref, pa_ref, za_ref, qt_ref, ka_ref, vt_ref, gb_ref, pc_ref, carry_ref,
                   *, tiles_per_seq):
    i = pl.program_id(0)
    h = _rmsnorm_rows(x_ref[...], gpre_ref[...]).astype(BF16)
    tm = h.shape[0]

    def proj(cols):
        return _dot(h, w_ref[:, cols[0]:cols[1]])

    pa_ref[...] = proj(_A_MAIN).astype(BF16)
    za_ref[...] = proj(_A_GATE)

    zf = proj(_B_F) + bf_ref[...]
    log_f = -(jnp.maximum(-zf, 0.0) + jnp.log1p(jnp.exp(-jnp.abs(zf))))

    @pl.when(i % tiles_per_seq == 0)
    def _():
        carry_ref[...] = jnp.zeros_like(carry_ref)

    cum = _dot_01_lhs(tri_ref[...], log_f) + carry_ref[...]
    carry_ref[...] = cum[tm - 1:, :]

    ones = ones_ref[...]
    bq = proj(_B_Q)
    bk = proj(_B_K)
    lane = lax.broadcasted_iota(jnp.int32, (tm, LANES), 1)
    is_feature = lane < HEAD_DIM
    is_one_q = (lane >= HEAD_DIM + 3) & (lane < HEAD_DIM + 6)
    is_one_k = (lane >= HEAD_DIM) & (lane < HEAD_DIM + 3)
    for pair in range(H_FOX // 2):
        sl = slice(pair * LANES, (pair + 1) * LANES)
        q2 = bq[:, sl]
        q2 = (q2 * lax.rsqrt(_head_mean_sq(q2, ones) + EPS) * qg_ref[:, sl]
              * (HEAD_DIM ** -0.5 * LOG2E))
        k2 = bk[:, sl]
        k2 = k2 * lax.rsqrt(_head_mean_sq(k2, ones) + EPS) * kg_ref[:, sl]
        for odd in range(2):
            head = 2 * pair + odd
            slot = slice(head * LANES, (head + 1) * LANES)
            qh = pltpu.roll(q2, HEAD_DIM, 1) if odd else q2
            kh = pltpu.roll(k2, HEAD_DIM, 1) if odd else k2
            hi, mid, lo = _bf16_terms(
                jnp.broadcast_to(cum[:, head:head + 1], (tm, LANES)) * LOG2E)
            q_bias = jnp.where(lane == HEAD_DIM, hi, jnp.where(lane == HEAD_DIM + 1, mid, jnp.where(
                lane == HEAD_DIM + 2, lo, jnp.where(is_one_q, 1.0, 0.0))))
            k_bias = jnp.where(lane == HEAD_DIM + 3, -hi, jnp.where(lane == HEAD_DIM + 4, -mid, jnp.where(
                lane == HEAD_DIM + 5, -lo, jnp.where(is_one_k, 1.0, 0.0))))
            qt_ref[0, slot, :] = jnp.where(is_feature, qh, q_bias).T.astype(BF16)
            ka_ref[:, slot] = jnp.where(is_feature, kh, k_bias).astype(BF16)
    v_t = proj(_B_V).T.astype(BF16)
    ones_row = jnp.where(lax.broadcasted_iota(jnp.int32, (BF16_SUBLANES, tm), 0) == 0,
                         1.0, 0.0).astype(BF16)
    for head in range(H_FOX):
        vt_ref[0, head * VT_ROWS:head * VT_ROWS + HEAD_DIM, :] = (
            v_t[head * HEAD_DIM:(head + 1) * HEAD_DIM, :])
        vt_ref[0, head * VT_ROWS + HEAD_DIM:(head + 1) * VT_ROWS, :] = ones_row
    gb_ref[...] = proj(_B_G).astype(BF16)

    cos = cos_ref[...]
    sin = sin_ref[...]
    low_half = (lax.broadcasted_iota(jnp.int32, cos.shape, 1) & (HEAD_DIM // 2)) == 0

    def rotary(y):
        swapped = jnp.where(low_half, pltpu.roll(y, D_RET - HEAD_DIM // 2, 1),
                            pltpu.roll(y, HEAD_DIM // 2, 1))
        return y * cos + swapped * sin

    pc_ref[:, 0:D_RET] = rotary(proj(_C_Q)).astype(BF16)
    pc_ref[:, D_RET:2 * D_RET] = (rotary(proj(_C_K)) * (HEAD_DIM ** -0.5)).astype(BF16)
    pc_ref[:, 2 * D_RET:4 * D_RET] = proj(_C_VG).astype(BF16)


def _inproj(x2, gpre, w, bf, qg, kg, cos, sin, tri, ones, seq_len):
    n = x2.shape[0]
    batch = n // seq_len
    tm = TM_PROJ
    tiles_per_seq = seq_len // tm
    const = lambda i: (0, 0)
    row = lambda i: (i, 0)
    seq_t = lambda i: (i // tiles_per_seq, 0, i % tiles_per_seq)
    return pl.pallas_call(
        functools.partial(_inproj_kernel, tiles_per_seq=tiles_per_seq),
        grid=(n // tm,),
        in_specs=[
            pl.BlockSpec((tm, D_MODEL), row),
            pl.BlockSpec((1, D_MODEL), const),
            pl.BlockSpec((D_MODEL, IN_COLS_PADDED), const),
            pl.BlockSpec((1, FGATE_PAD), const),
            pl.BlockSpec((1, D_FOX), const),
            pl.BlockSpec((1, D_FOX), const),
            pl.BlockSpec((tm, D_RET), lambda i: (i % tiles_per_seq, 0)),
            pl.BlockSpec((tm, D_RET), lambda i: (i % tiles_per_seq, 0)),
            pl.BlockSpec((tm, tm), const),
            pl.BlockSpec((LANES, LANES), const),
        ],
        out_specs=[
            pl.BlockSpec((tm, 3 * D_HGRN), row),
            pl.BlockSpec((tm, D_HGRN), row),
            pl.BlockSpec((1, H_FOX * LANES, tm), seq_t),
            pl.BlockSpec((tm, H_FOX * LANES), row),
            pl.BlockSpec((1, H_FOX * VT_ROWS, tm), seq_t),
            pl.BlockSpec((tm, D_FOX), row),
            pl.BlockSpec((tm, 4 * D_RET), row),
        ],
        out_shape=[
            jax.ShapeDtypeStruct((n, 3 * D_HGRN), BF16),
            jax.ShapeDtypeStruct((n, D_HGRN), F32),
            jax.ShapeDtypeStruct((batch, H_FOX * LANES, seq_len), BF16),
            jax.ShapeDtypeStruct((n, H_FOX * LANES), BF16),
            jax.ShapeDtypeStruct((batch, H_FOX * VT_ROWS, seq_len), BF16),
            jax.ShapeDtypeStruct((n, D_FOX), BF16),
            jax.ShapeDtypeStruct((n, 4 * D_RET), BF16),
        ],
        scratch_shapes=[pltpu.VMEM((1, FGATE_PAD), F32)],
        compiler_params=pltpu.CompilerParams(
            dimension_semantics=("arbitrary",), vmem_limit_bytes=VMEM_LIMIT),
        name="inproj",
    )(x2, gpre, w, bf, qg, kg, cos, sin, tri, ones)


def _hgrn_kernel(q_ref, v_ref, g_ref, z_ref, lb_ref, gain_ref, tri_ref, ones_ref, o_ref, st_ref,
                 *, layer):
    @pl.when(pl.program_id(1) == 0)
    def _():
        st_ref[...] = jnp.zeros_like(st_ref)

    tt = q_ref.shape[0]
    sub = SUB_HGRN

    lb = lb_ref[...]
    e = jnp.exp(lb - jnp.max(lb, axis=0, keepdims=True))
    p = e / jnp.sum(e, axis=0, keepdims=True)
    run = p[0:1]
    for i in range(1, layer + 1):
        run = run + p[i:i + 1]
    lower = run - p[0:1]

    row = lax.broadcasted_iota(jnp.int32, (tt, tt), 0)
    col = lax.broadcasted_iota(jnp.int32, (tt, tt), 1)
    causal = (col <= row) & (col >= (row & ~(sub - 1)))
    first = _first_head_lanes((tt, LANES))
    rr = lax.broadcasted_iota(jnp.int32, (LANES, LANES), 0)
    cc = lax.broadcasted_iota(jnp.int32, (LANES, LANES), 1)
    same_head = ((rr ^ cc) & HEAD_DIM) == 0
    tri = tri_ref[...]
    ones = ones_ref[...]

    for pair in range(H_HGRN // 2):
        sl = slice(pair * LANES, (pair + 1) * LANES)
        lw = lower[:, sl]
        sg = _sigmoid(z_ref[:, sl])
        f = lw + (1.0 - lw) * sg
        log_f = jnp.log(jnp.maximum(f, MIN_FORGET))
        k = (1.0 - lw) * (1.0 - sg)
        cum = _dot_01_lhs(tri, log_f)
        e_q = jnp.exp(cum)
        qt = (q_ref[:, sl].astype(F32) * e_q).astype(BF16)
        kt = (k * jnp.exp(-cum)).astype(BF16)
        v = v_ref[:, sl]

        intra = []
        for keep in (first, ~first):
            s = _dot_nt(jnp.where(keep, qt, jnp.zeros_like(qt)), kt)
            intra.append(_dot(jnp.where(causal, s, 0.0).astype(BF16), v))
        o = jnp.where(first, intra[0], intra[1])

        st = st_ref[pair]
        outs = []
        for j in range(tt // sub):
            r = slice(j * sub, (j + 1) * sub)
            outs.append(o[r] + _dot_nt(qt[r], st.astype(BF16)))
            upd = jnp.where(same_head, _dot_tn(v[r], kt[r]), 0.0)
            st = e_q[(j + 1) * sub - 1:(j + 1) * sub, :] * (st + upd)
        st_ref[pair] = st
        o = jnp.concatenate(outs, axis=0)

        o = o * lax.rsqrt(_head_mean_sq(o, ones) + EPS) * gain_ref[:, sl]
        g = g_ref[:, sl].astype(F32)
        o_ref[:, sl] = (o * (g * _sigmoid(g))).astype(BF16)


def _hgrn(pa, za, lb, gain, tri, ones, batch, seq_len, layer):
    n = pa.shape[0]
    tt = TT_HGRN
    tps = seq_len // tt
    const = lambda b, t: (0, 0)
    return pl.pallas_call(
        functools.partial(_hgrn_kernel, layer=layer),
        grid=(batch, tps),
        in_specs=[
            pl.BlockSpec((tt, D_HGRN), lambda b, t: (b * tps + t, 0)),
            pl.BlockSpec((tt, D_HGRN), lambda b, t: (b * tps + t, 1)),
            pl.BlockSpec((tt, D_HGRN), lambda b, t: (b * tps + t, 2)),
            pl.BlockSpec((tt, D_HGRN), lambda b, t: (b * tps + t, 0)),
            pl.BlockSpec(lb.shape, const),
            pl.BlockSpec((1, D_HGRN), const),
            pl.BlockSpec((tt, tt), const),
            pl.BlockSpec((LANES, LANES), const),
        ],
        out_specs=pl.BlockSpec((tt, D_HGRN), lambda b, t: (b * tps + t, 0)),
        out_shape=jax.ShapeDtypeStruct((n, D_HGRN), BF16),
        scratch_shapes=[pltpu.VMEM((H_HGRN // 2, LANES, LANES), F32)],
        compiler_params=pltpu.CompilerParams(
            dimension_semantics=("arbitrary", "arbitrary"), vmem_limit_bytes=VMEM_LIMIT),
        name="hgrn",
    )(pa, pa, pa, za, lb, gain, tri, ones)


def _ret_kernel(q_ref, k_ref, v_ref, g_ref, dec_ref, qs_ref, ks_ref, cd_ref, gain_ref, ones_ref,
                o_ref, st_ref):
    @pl.when(pl.program_id(1) == 0)
    def _():
        st_ref[...] = jnp.zeros_like(st_ref)

    c = q_ref.shape[0]
    first = _first_head_lanes((c, LANES))
    rr = lax.broadcasted_iota(jnp.int32, (LANES, LANES), 0)
    cc = lax.broadcasted_iota(jnp.int32, (LANES, LANES), 1)
    same_head = ((rr ^ cc) & HEAD_DIM) == 0
    ones = ones_ref[...]

    for pair in range(H_RET // 2):
        sl = slice(pair * LANES, (pair + 1) * LANES)
        q = q_ref[:, sl]
        k = k_ref[:, sl]
        v = v_ref[:, sl]
        intra = []
        for hh, keep in enumerate((first, ~first)):
            s = _dot_nt(jnp.where(keep, q, jnp.zeros_like(q)), k) * dec_ref[2 * pair + hh]
            intra.append(_dot(s.astype(BF16), v))
        st = st_ref[pair]
        q_dec = (q.astype(F32) * qs_ref[:, sl]).astype(BF16)
        o = jnp.where(first, intra[0], intra[1]) + _dot_nt(q_dec, st.astype(BF16))
        k_dec = (k.astype(F32) * ks_ref[:, sl]).astype(BF16)
        st_ref[pair] = cd_ref[:, sl] * st + jnp.where(same_head, _dot_tn(v, k_dec), 0.0)

        o = o * lax.rsqrt(_head_mean_sq(o, ones) + EPS) * gain_ref[:, sl]
        g = g_ref[:, sl].astype(F32)
        o_ref[:, sl] = (o * (g * _sigmoid(g))).astype(BF16)


def _retention(pc, dec, qs, ks, cd, gain, ones, batch, seq_len):
    n = pc.shape[0]
    c = C_RET
    tps = seq_len // c
    const2 = lambda b, t: (0, 0)
    blk = lambda j: pl.BlockSpec((c, D_RET), lambda b, t: (b * tps + t, j))
    return pl.pallas_call(
        _ret_kernel,
        grid=(batch, tps),
        in_specs=[
            blk(0), blk(1), blk(2), blk(3),
            pl.BlockSpec((H_RET, c, c), lambda b, t: (0, 0, 0)),
            pl.BlockSpec((c, D_RET), const2),
            pl.BlockSpec((c, D_RET), const2),
            pl.BlockSpec((1, D_RET), const2),
            pl.BlockSpec((1, D_RET), const2),
            pl.BlockSpec((LANES, LANES), const2),
        ],
        out_specs=pl.BlockSpec((c, D_RET), lambda b, t: (b * tps + t, 0)),
        out_shape=jax.ShapeDtypeStruct((n, D_RET), BF16),
        scratch_shapes=[pltpu.VMEM((H_RET // 2, LANES, LANES), F32)],
        compiler_params=pltpu.CompilerParams(
            dimension_semantics=("arbitrary", "arbitrary"), vmem_limit_bytes=VMEM_LIMIT),
        name="retention",
    )(pc, pc, pc, pc, dec, qs, ks, cd, gain, ones)


def _fox_kernel(qt_ref, k_ref, vt_ref, g_ref, o_ref, s_scr, p_scr, m_scr, acc_scr):
    qi = pl.program_id(2)
    tq = o_ref.shape[0]

    m_scr[...] = jnp.full_like(m_scr, -jnp.inf)
    acc_scr[...] = jnp.zeros_like(acc_scr)
    p_scr[1] = jnp.zeros_like(p_scr[1])

    def scores(kj, par):
        start = pl.multiple_of(kj * tq, tq)
        for hh in range(2):
            slot = slice(hh * LANES, (hh + 1) * LANES)
            s_scr[par, hh] = _dot(k_ref[pl.ds(start, tq), slot], qt_ref[0, slot, :])

    def softmax(par, masked):
        if masked:
            key = lax.broadcasted_iota(jnp.int32, (tq, tq), 0)
            qry = lax.broadcasted_iota(jnp.int32, (tq, tq), 1)
            keep = qry >= key
        alphas = []
        for hh in range(2):
            s = s_scr[par, hh]
            if masked:
                s = jnp.where(keep, s, MASK_VALUE)
            m_old = m_scr[hh]
            m_new = jnp.maximum(m_old, jnp.max(s, axis=0, keepdims=True))
            p_scr[par, hh] = jnp.exp2(s - m_new).astype(BF16)
            alphas.append(jnp.exp2(m_old - m_new))
            m_scr[hh] = m_new
        return alphas

    def weighted_values(kj, par):
        start = pl.multiple_of(kj * tq, tq)
        return [_dot(vt_ref[0, hh * VT_ROWS:(hh + 1) * VT_ROWS, pl.ds(start, tq)],
                     p_scr[par, hh]) for hh in range(2)]

    def step(j, par, masked, has_next):
        pv = weighted_values(jnp.maximum(j - 1, 0), 1 - par)
        alphas = softmax(par, masked)
        if has_next:
            scores(j + 1, 1 - par)
        for hh in range(2):
            acc_scr[hh] = alphas[hh] * (acc_scr[hh] + pv[hh])

    scores(0, 0)

    def two_steps(i, carry):
        step(2 * i, 0, masked=False, has_next=True)
        step(2 * i + 1, 1, masked=False, has_next=True)
        return carry

    lax.fori_loop(0, qi // 2, two_steps, 0)

    @pl.when(qi % 2 == 1)
    def _():
        step(qi - 1, 0, masked=False, has_next=True)

    def finish(par):
        step(qi, par, masked=True, has_next=False)
        pv = weighted_values(qi, par)
        heads = []
        for hh in range(2):
            total = acc_scr[hh] + pv[hh]
            heads.append(total[:HEAD_DIM] / total[HEAD_DIM:HEAD_DIM + 1])
        o_t = jnp.concatenate(heads, axis=0)
        o_ref[...] = (o_t.T * _sigmoid(g_ref[...].astype(F32))).astype(BF16)

    for par in range(2):
        pl.when(qi % 2 == par)(functools.partial(finish, par))


def _fox(qt, ka, vt, gb, batch, seq_len):
    n = ka.shape[0]
    tq = TQ_FOX
    nq = seq_len // tq
    pairs = H_FOX // 2
    return pl.pallas_call(
        _fox_kernel,
        grid=(batch, pairs, nq),
        in_specs=[
            pl.BlockSpec((1, 2 * LANES, tq), lambda b, p, i: (b, p, i)),
            pl.BlockSpec((seq_len, 2 * LANES), lambda b, p, i: (b, p)),
            pl.BlockSpec((1, 2 * VT_ROWS, seq_len), lambda b, p, i: (b, p, 0)),
            pl.BlockSpec((tq, LANES), lambda b, p, i: (b * nq + i, p)),
        ],
        out_specs=pl.BlockSpec((tq, LANES), lambda b, p, i: (b * nq + i, p)),
        out_shape=jax.ShapeDtypeStruct((n, D_FOX), BF16),
        scratch_shapes=[
            pltpu.VMEM((2, 2, tq, tq), F32),
            pltpu.VMEM((2, 2, tq, tq), BF16),
            pltpu.VMEM((2, 1, tq), F32),
            pltpu.VMEM((2, VT_ROWS, tq), F32),
        ],
        compiler_params=pltpu.CompilerParams(
            dimension_semantics=("arbitrary", "arbitrary", "arbitrary"),
            vmem_limit_bytes=VMEM_LIMIT),
        name="fox",
    )(qt, ka, vt, gb)


def _outproj_kernel(oa_ref, ob_ref, oc_ref, wa_ref, wb_ref, wc_ref, x_ref, g_ref, out_ref):
    mixed = (_dot(oa_ref[...], wa_ref[...]) + _dot(ob_ref[...], wb_ref[...])
             + _dot(oc_ref[...], wc_ref[...]))
    out_ref[...] = x_ref[...] + _rmsnorm_rows(mixed, g_ref[...])


def _outproj(oa, ob, oc, wa, wb, wc, x2, gpost):
    n = x2.shape[0]
    tm = TM_PROJ
    const = lambda i: (0, 0)
    row = lambda i: (i, 0)
    return pl.pallas_call(
        _outproj_kernel,
        grid=(n // tm,),
        in_specs=[
            pl.BlockSpec((tm, D_HGRN), row),
            pl.BlockSpec((tm, D_FOX), row),
            pl.BlockSpec((tm, D_RET), row),
            pl.BlockSpec((D_HGRN, D_MODEL), const),
            pl.BlockSpec((D_FOX, D_MODEL), const),
            pl.BlockSpec((D_RET, D_MODEL), const),
            pl.BlockSpec((tm, D_MODEL), row),
            pl.BlockSpec((1, D_MODEL), const),
        ],
        out_specs=pl.BlockSpec((tm, D_MODEL), row),
        out_shape=jax.ShapeDtypeStruct((n, D_MODEL), F32),
        compiler_params=pltpu.CompilerParams(
            dimension_semantics=("arbitrary",), vmem_limit_bytes=VMEM_LIMIT),
        name="outproj",
    )(oa, ob, oc, wa, wb, wc, x2, gpost)


def _ffn_kernel(x_ref, gpre_ref, wup_ref, cw_ref, cb_ref, wd_ref, gpost_ref, out_ref,
                xp_scr, act_scr, tail_scr, *, tiles_per_seq):
    i = pl.program_id(0)
    tm = x_ref.shape[0]
    groups = tm // SUBLANES
    fc = FC_FFN

    xp_scr[...] = pltpu.einshape("(ab)f->(ba)f", x_ref[...], a=SUBLANES)
    h = _rmsnorm_rows(xp_scr[...], gpre_ref[...]).astype(BF16)

    @pl.when(i == 0)
    def _():
        tail_scr[...] = jnp.zeros_like(tail_scr)

    seq_start = i % tiles_per_seq == 0
    first_sublane = lax.broadcasted_iota(jnp.int32, (SUBLANES, fc), 0) == 0

    def causal_conv(u, col, slot):
        prev = jnp.where(seq_start, 0.0, tail_scr[slot])
        last = u[tm - 2 * SUBLANES:, :]
        tail_scr[slot] = last
        head = []
        for g in range(2):
            rows = slice(g * SUBLANES, (g + 1) * SUBLANES)
            entering = prev[(g + 1) * SUBLANES - 1:(g + 1) * SUBLANES, :]
            head.append(jnp.where(first_sublane, entering, pltpu.roll(last[rows], 1, 0)))
        u1 = jnp.concatenate([head[1], u[:tm - SUBLANES]], axis=0)
        u2 = jnp.concatenate([head[0], head[1], u[:tm - 2 * SUBLANES]], axis=0)
        cols = slice(col, col + fc)
        return (cb_ref[:, cols] + cw_ref[0:1, cols] * u2 + cw_ref[1:2, cols] * u1
                + cw_ref[2:3, cols] * u)

    for c in range(D_FF // fc):
        gate = causal_conv(_dot(h, wup_ref[:, c * fc:(c + 1) * fc]), c * fc, 2 * c)
        val = causal_conv(_dot(h, wup_ref[:, D_FF + c * fc:D_FF + (c + 1) * fc]),
                          D_FF + c * fc, 2 * c + 1)
        act_scr[:, c * fc:(c + 1) * fc] = (gate * _sigmoid(gate) * val).astype(BF16)

    y = _dot(act_scr[...], wd_ref[...])
    res = xp_scr[...] + _rmsnorm_rows(y, gpost_ref[...])
    out_ref[...] = pltpu.einshape("(ba)f->(ab)f", res, a=SUBLANES)


def _ffn(x2, gpre, w_up, conv_w, conv_b, w_down, gpost, seq_len):
    n = x2.shape[0]
    tm = TM_FFN
    nf = D_FF // FC_FFN
    tiles_per_seq = seq_len // tm
    row = lambda i: (i, 0)
    resident = lambda shape: pl.BlockSpec(shape, lambda i: (0, 0), pipeline_mode=pl.Buffered(1))
    return pl.pallas_call(
        functools.partial(_ffn_kernel, tiles_per_seq=tiles_per_seq),
        grid=(n // tm,),
        in_specs=[
            pl.BlockSpec((tm, D_MODEL), row),
            resident((1, D_MODEL)),
            resident((D_MODEL, 2 * D_FF)),
            resident((CONV_W, 2 * D_FF)),
            resident((1, 2 * D_FF)),
            resident((D_FF, D_MODEL)),
            resident((1, D_MODEL)),
        ],
        out_specs=pl.BlockSpec((tm, D_MODEL), row),
        out_shape=jax.ShapeDtypeStruct((n, D_MODEL), F32),
        scratch_shapes=[
            pltpu.VMEM((tm, D_MODEL), F32),
            pltpu.VMEM((tm, D_FF), BF16),
            pltpu.VMEM((2 * nf, 2 * SUBLANES, FC_FFN), F32),
        ],
        compiler_params=pltpu.CompilerParams(
            dimension_semantics=("arbitrary",), vmem_limit_bytes=VMEM_LIMIT),
        name="ffn",
    )(x2, gpre, w_up, conv_w, conv_b, w_down, gpost)


def _block_ones(n):
    idx = jnp.arange(n) // HEAD_DIM
    return (idx[:, None] == idx[None, :]).astype(BF16)


def _lower_tri(n, block):
    r = jnp.arange(n)
    return ((r[:, None] >= r[None, :]) & (r[:, None] // block == r[None, :] // block)).astype(BF16)


def _rotary_tables(seq_len):
    inv_freq = 1.0 / (ROPE_BASE ** (jnp.arange(0, HEAD_DIM, 2, dtype=F32) / HEAD_DIM))
    ang = jnp.arange(seq_len, dtype=F32)[:, None] * inv_freq[None, :]
    cos, sin = jnp.cos(ang), jnp.sin(ang)
    cos_full = jnp.tile(jnp.concatenate([cos, cos], axis=-1), (1, H_RET))
    sin_signed = jnp.tile(jnp.concatenate([-sin, sin], axis=-1), (1, H_RET))
    return cos_full, sin_signed


def _retention_tables(c):
    log_gamma = jnp.log1p(-jnp.exp2(-5.0 - jnp.arange(H_RET, dtype=F32)))
    pos = jnp.arange(c, dtype=F32)
    rel = pos[:, None] - pos[None, :]
    dec = jnp.where(rel >= 0, jnp.exp(log_gamma[:, None, None] * jnp.maximum(rel, 0.0)), 0.0)
    per_lane = lambda a: jnp.repeat(a, HEAD_DIM, axis=-1)
    qs = per_lane(jnp.exp(log_gamma[None, :] * (pos[:, None] + 1.0)))
    ks = per_lane(jnp.exp(log_gamma[None, :] * (c - 1.0 - pos[:, None])))
    cd = per_lane(jnp.exp(log_gamma * c)[None, :])
    return dec, qs, ks, cd


def _reorder_in_weight(w):
    widths = [D_HGRN] * 4 + [D_FOX] * 3 + [H_FOX] + [D_FOX] + [D_RET] * 4
    offs = [0]
    for wd in widths:
        offs.append(offs[-1] + wd)
    part = lambda j: w[:, offs[j]:offs[j + 1]]
    a_q, a_f, a_i, a_g, b_q, b_k, b_v, b_f, b_g, c_q, c_k, c_v, c_g = (part(j) for j in range(13))
    b_f = jnp.pad(b_f, ((0, 0), (0, FGATE_PAD - H_FOX)))
    return jnp.concatenate(
        [a_q, a_i, a_g, a_f, b_q, b_k, b_v, b_g, b_f, c_q, c_k, c_v, c_g], axis=-1).astype(BF16)


def kernel(x, w_in, b_fox_f, fox_q_gain, fox_k_gain, hgrn_lb, hgrn_out_gain, ret_out_gain, w_out,
           g_mix_pre, g_mix_post, w_up, conv_w, conv_b, w_down, g_ffn_pre, g_ffn_post):
    batch, seq_len, d_model = x.shape
    assert d_model == D_MODEL
    assert seq_len % TM_FFN == 0 and seq_len % TQ_FOX == 0
    depth = w_in.shape[0]
    n = batch * seq_len

    cos_full, sin_signed = _rotary_tables(seq_len)
    dec, qs, ks, cd = _retention_tables(C_RET)
    tri_proj = _lower_tri(TM_PROJ, TM_PROJ)
    tri_hgrn = _lower_tri(TT_HGRN, SUB_HGRN)
    ones_pair = _block_ones(LANES)
    lb = hgrn_lb.astype(F32)

    x2 = x.reshape(n, D_MODEL)
    for l in range(depth):
        w = _reorder_in_weight(w_in[l])
        bf = jnp.pad(b_fox_f[l].astype(F32), (0, FGATE_PAD - H_FOX))[None, :]
        qg = jnp.tile(fox_q_gain[l].astype(F32), H_FOX)[None, :]
        kg = jnp.tile(fox_k_gain[l].astype(F32), H_FOX)[None, :]
        pa, za, qt, ka, vt, gb, pc = _inproj(x2, g_mix_pre[l][None, :], w, bf, qg, kg, cos_full,
                                             sin_signed, tri_proj, ones_pair, seq_len)

        o_a = _hgrn(pa, za, lb, hgrn_out_gain[l].reshape(1, D_HGRN), tri_hgrn, ones_pair,
                    batch, seq_len, l)
        o_b = _fox(qt, ka, vt, gb, batch, seq_len)
        o_c = _retention(pc, dec, qs, ks, cd, ret_out_gain[l].reshape(1, D_RET), ones_pair,
                         batch, seq_len)

        wo = w_out[l].astype(BF16)
        x2 = _outproj(o_a, o_b, o_c, wo[:D_HGRN], wo[D_HGRN:D_HGRN + D_FOX],
                      wo[D_HGRN + D_FOX:], x2, g_mix_post[l][None, :])
        x2 = _ffn(x2, g_ffn_pre[l][None, :], w_up[l].astype(BF16), conv_w[l], conv_b[l][None, :],
                  w_down[l].astype(BF16), g_ffn_post[l][None, :], seq_len)
    return x2.reshape(batch, seq_len, D_MODEL)
```

```python
import functools

import jax
import jax.numpy as jnp
from jax import lax
from jax.experimental import pallas as pl
from jax.experimental.pallas import tpu as pltpu

F32 = jnp.float32
BF16 = jnp.bfloat16

D_MODEL = 1024
HEAD_DIM = 64
H_HGRN, H_FOX, H_RET = 6, 6, 4
D_HGRN, D_FOX, D_RET = H_HGRN * HEAD_DIM, H_FOX * HEAD_DIM, H_RET * HEAD_DIM
D_FF = 2816
CONV_W = 3
ROPE_BASE = 10000.0
EPS = 1e-6
MIN_FORGET = 1e-12
MASK_VALUE = -1e30

LANES = 128
SUBLANES = 8
FGATE_PAD = LANES
BF16_SUBLANES = 16
VT_ROWS = HEAD_DIM + BF16_SUBLANES
LOG2E = 1.4426950408889634

_A_MAIN = (0, 3 * D_HGRN)
_A_GATE = (_A_MAIN[1], _A_MAIN[1] + D_HGRN)
_B_Q = (_A_GATE[1], _A_GATE[1] + D_FOX)
_B_K = (_B_Q[1], _B_Q[1] + D_FOX)
_B_V = (_B_K[1], _B_K[1] + D_FOX)
_B_G = (_B_V[1], _B_V[1] + D_FOX)
_B_F = (_B_G[1], _B_G[1] + FGATE_PAD)
_C_Q = (_B_F[1], _B_F[1] + D_RET)
_C_K = (_C_Q[1], _C_Q[1] + D_RET)
_C_VG = (_C_K[1], _C_K[1] + 2 * D_RET)
IN_COLS_PADDED = _C_VG[1]

TM_PROJ = 512
TM_FFN = 512
FC_FFN = 256
TT_HGRN = 256
NB_HGRN = 4
SUB_HGRN = 32
SAFE_DIAG = 4
FAST_PATH_MAX_DECAY = 60.0
C_RET = 256
TQ_FOX = 512
VMEM_LIMIT = 48 * 1024 * 1024


def _dot(a, b):
    return jnp.dot(a, b, preferred_element_type=F32)


def _dot_nt(a, b):
    return lax.dot_general(a, b, (((1,), (1,)), ((), ())), preferred_element_type=F32)


def _dot_tn(a, b):
    return lax.dot_general(a, b, (((0,), (0,)), ((), ())), preferred_element_type=F32)


def _dot_01_lhs(m01, x):
    hi = x.astype(BF16)
    r = x - hi.astype(F32)
    mid = r.astype(BF16)
    lo = (r - mid.astype(F32)).astype(BF16)
    return _dot(m01, hi) + _dot(m01, mid) + _dot(m01, lo)


def _head_mean_sq(y, ones_bd):
    sq = y * y
    hi = sq.astype(BF16)
    lo = (sq - hi.astype(F32)).astype(BF16)
    return (_dot(hi, ones_bd) + _dot(lo, ones_bd)) * (1.0 / HEAD_DIM)


def _rmsnorm_rows(x, gain):
    return x * lax.rsqrt(jnp.mean(x * x, axis=-1, keepdims=True) + EPS) * gain


def _sigmoid(x):
    return 1.0 / (1.0 + jnp.exp(-x))


def _first_head_lanes(shape):
    return (lax.broadcasted_iota(jnp.int32, shape, len(shape) - 1) & HEAD_DIM) == 0


def _bf16_terms(x):
    hi = x.astype(BF16).astype(F32)
    mid = (x - hi).astype(BF16).astype(F32)
    lo = (x - hi - mid).astype(BF16).astype(F32)
    return hi, mid, lo


def _inproj_kernel(x_ref, gpre_ref, w_ref, bf_ref, qg_ref, kg_ref, cos_ref, sin_ref, tri_ref,
                   ones_ref, pa_ref, za_ref, qt_ref, ka_ref, vt_ref, gb_ref, pc_ref, carry_ref,
                   *, tiles_per_seq):
    i = pl.program_id(0)
    h = _rmsnorm_rows(x_ref[...], gpre_ref[...]).astype(BF16)
    tm = h.shape[0]

    def proj(cols):
        return _dot(h, w_ref[:, cols[0]:cols[1]])

    pa_ref[...] = proj(_A_MAIN).astype(BF16)
    za_ref[...] = proj(_A_GATE)

    zf = proj(_B_F) + bf_ref[...]
    log_f = -(jnp.maximum(-zf, 0.0) + jnp.log1p(jnp.exp(-jnp.abs(zf))))

    @pl.when(i % tiles_per_seq == 0)
    def _():
        carry_ref[...] = jnp.zeros_like(carry_ref)

    cum = _dot_01_lhs(tri_ref[...], log_f) + carry_ref[...]
    carry_ref[...] = cum[tm - 1:, :]

    ones = ones_ref[...]
    bq = proj(_B_Q)
    bk = proj(_B_K)
    lane = lax.broadcasted_iota(jnp.int32, (tm, LANES), 1)
    is_feature = lane < HEAD_DIM
    is_one_q = (lane >= HEAD_DIM + 3) & (lane < HEAD_DIM + 6)
    is_one_k = (lane >= HEAD_DIM) & (lane < HEAD_DIM + 3)
    for pair in range(H_FOX // 2):
        sl = slice(pair * LANES, (pair + 1) * LANES)
        q2 = bq[:, sl]
        q2 = (q2 * lax.rsqrt(_head_mean_sq(q2, ones) + EPS) * qg_ref[:, sl]
              * (HEAD_DIM ** -0.5 * LOG2E))
        k2 = bk[:, sl]
        k2 = k2 * lax.rsqrt(_head_mean_sq(k2, ones) + EPS) * kg_ref[:, sl]
        for odd in range(2):
            head = 2 * pair + odd
            slot = slice(head * LANES, (head + 1) * LANES)
            qh = pltpu.roll(q2, HEAD_DIM, 1) if odd else q2
            kh = pltpu.roll(k2, HEAD_DIM, 1) if odd else k2
            hi, mid, lo = _bf16_terms(
                jnp.broadcast_to(cum[:, head:head + 1], (tm, LANES)) * LOG2E)
            q_bias = jnp.where(lane == HEAD_DIM, hi, jnp.where(lane == HEAD_DIM + 1, mid, jnp.where(
                lane == HEAD_DIM + 2, lo, jnp.where(is_one_q, 1.0, 0.0))))
            k_bias = jnp.where(lane == HEAD_DIM + 3, -hi, jnp.where(lane == HEAD_DIM + 4, -mid, jnp.where(
                lane == HEAD_DIM + 5, -lo, jnp.where(is_one_k, 1.0, 0.0))))
            qt_ref[0, slot, :] = jnp.where(is_feature, qh, q_bias).T.astype(BF16)
            ka_ref[:, slot] = jnp.where(is_feature, kh, k_bias).astype(BF16)
    v_t = proj(_B_V).T.astype(BF16)
    ones_row = jnp.where(lax.broadcasted_iota(jnp.int32, (BF16_SUBLANES, tm), 0) == 0,
                         1.0, 0.0).astype(BF16)
    for head in range(H_FOX):
        vt_ref[0, head * VT_ROWS:head * VT_ROWS + HEAD_DIM, :] = (
            v_t[head * HEAD_DIM:(head + 1) * HEAD_DIM, :])
        vt_ref[0, head * VT_ROWS + HEAD_DIM:(head + 1) * VT_ROWS, :] = ones_row
    gb_ref[...] = proj(_B_G).astype(BF16)

    cos = cos_ref[...]
    sin = sin_ref[...]
    low_half = (lax.broadcasted_iota(jnp.int32, cos.shape, 1) & (HEAD_DIM // 2)) == 0

    def rotary(y):
        swapped = jnp.where(low_half, pltpu.roll(y, D_RET - HEAD_DIM // 2, 1),
                            pltpu.roll(y, HEAD_DIM // 2, 1))
        return y * cos + swapped * sin

    pc_ref[:, 0:D_RET] = rotary(proj(_C_Q)).astype(BF16)
    pc_ref[:, D_RET:2 * D_RET] = (rotary(proj(_C_K)) * (HEAD_DIM ** -0.5)).astype(BF16)
    pc_ref[:, 2 * D_RET:4 * D_RET] = proj(_C_VG).astype(BF16)


def _inproj(x2, gpre, w, bf, qg, kg, cos, sin, tri, ones, seq_len):
    n = x2.shape[0]
    batch = n // seq_len
    tm = TM_PROJ
    tiles_per_seq = seq_len // tm
    const = lambda i: (0, 0)
    row = lambda i: (i, 0)
    seq_t = lambda i: (i // tiles_per_seq, 0, i % tiles_per_seq)
    return pl.pallas_call(
        functools.partial(_inproj_kernel, tiles_per_seq=tiles_per_seq),
        grid=(n // tm,),
        in_specs=[
            pl.BlockSpec((tm, D_MODEL), row),
            pl.BlockSpec((1, D_MODEL), const),
            pl.BlockSpec((D_MODEL, IN_COLS_PADDED), const),
            pl.BlockSpec((1, FGATE_PAD), const),
            pl.BlockSpec((1, D_FOX), const),
            pl.BlockSpec((1, D_FOX), const),
            pl.BlockSpec((tm, D_RET), lambda i: (i % tiles_per_seq, 0)),
            pl.BlockSpec((tm, D_RET), lambda i: (i % tiles_per_seq, 0)),
            pl.BlockSpec((tm, tm), const),
            pl.BlockSpec((LANES, LANES), const),
        ],
        out_specs=[
            pl.BlockSpec((tm, 3 * D_HGRN), row),
            pl.BlockSpec((tm, D_HGRN), row),
            pl.BlockSpec((1, H_FOX * LANES, tm), seq_t),
            pl.BlockSpec((tm, H_FOX * LANES), row),
            pl.BlockSpec((1, H_FOX * VT_ROWS, tm), seq_t),
            pl.BlockSpec((tm, D_FOX), row),
            pl.BlockSpec((tm, 4 * D_RET), row),
        ],
        out_shape=[
            jax.ShapeDtypeStruct((n, 3 * D_HGRN), BF16),
            jax.ShapeDtypeStruct((n, D_HGRN), F32),
            jax.ShapeDtypeStruct((batch, H_FOX * LANES, seq_len), BF16),
            jax.ShapeDtypeStruct((n, H_FOX * LANES), BF16),
            jax.ShapeDtypeStruct((batch, H_FOX * VT_ROWS, seq_len), BF16),
            jax.ShapeDtypeStruct((n, D_FOX), BF16),
            jax.ShapeDtypeStruct((n, 4 * D_RET), BF16),
        ],
        scratch_shapes=[pltpu.VMEM((1, FGATE_PAD), F32)],
        compiler_params=pltpu.CompilerParams(
            dimension_semantics=("arbitrary",), vmem_limit_bytes=VMEM_LIMIT),
        name="inproj",
    )(x2, gpre, w, bf, qg, kg, cos, sin, tri, ones)


_M_CUM, _M_AFTER, _M_DIAG, _M_MERGE0 = 0, 1, 2, 3
_MERGE_BLOCKS = (2 * SAFE_DIAG, 4 * SAFE_DIAG, 8 * SAFE_DIAG)
assert _MERGE_BLOCKS[-1] == SUB_HGRN


def _hgrn_matrices(tt):
    t = jnp.arange(tt)[:, None]
    u = jnp.arange(tt)[None, :]
    same = lambda m: (t // m) == (u // m)
    mats = [same(SUB_HGRN) & (u <= t), same(SUB_HGRN) & (u > t),
            same(SAFE_DIAG) & (u <= t) & (u > (t // SAFE_DIAG) * SAFE_DIAG)]
    for m in _MERGE_BLOCKS:
        mid = (t // m) * m + m // 2 - 1
        mats.append(same(m) & (((t > mid) & (u > mid) & (u <= t)) | ((t <= mid) & (u > t) & (u <= mid))))
    return jnp.stack(mats).astype(BF16)


def _hgrn_kernel(q_ref, v_ref, g_ref, z_ref, lb_ref, gain_ref, mats_ref, ones_ref, o_ref, st_ref,
                 logf_scr, k_scr, cum_scr, *, layer):
    @pl.when(pl.program_id(1) == 0)
    def _():
        st_ref[...] = jnp.zeros_like(st_ref)

    nb, tt = q_ref.shape[0], q_ref.shape[1]
    sub = SUB_HGRN

    lb = lb_ref[...]
    e = jnp.exp(lb - jnp.max(lb, axis=0, keepdims=True))
    p = e / jnp.sum(e, axis=0, keepdims=True)
    run = p[0:1]
    for i in range(1, layer + 1):
        run = run + p[i:i + 1]
    lower = run - p[0:1]

    lowest = None
    for b in range(nb):
        sg = _sigmoid(z_ref[b])
        f = lower + (1.0 - lower) * sg
        log_f = jnp.log(jnp.maximum(f, MIN_FORGET))
        cum = _dot_01_lhs(mats_ref[_M_CUM], log_f)
        logf_scr[b] = log_f
        k_scr[b] = (1.0 - lower) * (1.0 - sg)
        cum_scr[b] = cum
        low_b = jnp.min(cum)
        lowest = low_b if lowest is None else jnp.minimum(lowest, low_b)
    exact_fallback = lowest < -FAST_PATH_MAX_DECAY

    row = lax.broadcasted_iota(jnp.int32, (tt, tt), 0)
    col = lax.broadcasted_iota(jnp.int32, (tt, tt), 1)

    def diag_mask(m):
        in_block = row & (m - 1)
        dist = row - col
        return ((in_block - dist) | dist) >= 0

    def merge_mask(m):
        half = m // 2
        return (((row ^ col) & ~(m - 1)) | ((row & half) ^ half) | (col & half)) == 0

    first = _first_head_lanes((tt, LANES))
    rr = lax.broadcasted_iota(jnp.int32, (LANES, LANES), 0)
    cc = lax.broadcasted_iota(jnp.int32, (LANES, LANES), 1)
    same_head = ((rr ^ cc) & HEAD_DIM) == 0
    ones = ones_ref[...]

    def pair_tile(b, pair, exact):
        sl = slice(pair * LANES, (pair + 1) * LANES)
        q = q_ref[b, :, sl].astype(F32)
        k = k_scr[b, :, sl]
        cum = cum_scr[b, :, sl]
        v = v_ref[b, :, sl]
        e_q = jnp.exp(cum)
        qt = (q * e_q).astype(BF16)
        if exact:
            log_f = logf_scr[b, :, sl]
            since = _dot_01_lhs(mats_ref[_M_DIAG], log_f)
            pieces = [((q * jnp.exp(since)).astype(BF16), (k * jnp.exp(-since)).astype(BF16),
                       diag_mask(SAFE_DIAG))]
            for idx, m in enumerate(_MERGE_BLOCKS):
                w = jnp.exp(_dot_01_lhs(mats_ref[_M_MERGE0 + idx], log_f))
                pieces.append(((q * w).astype(BF16), (k * w).astype(BF16), merge_mask(m)))
            k_state = (k * jnp.exp(_dot_01_lhs(mats_ref[_M_AFTER], log_f))).astype(BF16)
        else:
            k_state = (k * jnp.exp(-cum)).astype(BF16)
            pieces = [(qt, k_state, diag_mask(sub))]

        intra = []
        for keep in (first, ~first):
            s = jnp.zeros((tt, tt), F32)
            for q_f, k_f, mask in pieces:
                s = jnp.where(mask, _dot_nt(jnp.where(keep, q_f, jnp.zeros_like(q_f)), k_f), s)
            intra.append(_dot(s.astype(BF16), v))
        o = jnp.where(first, intra[0], intra[1])

        st = st_ref[b, pair]
        outs = []
        for j in range(tt // sub):
            r = slice(j * sub, (j + 1) * sub)
            outs.append(o[r] + _dot_nt(qt[r], st.astype(BF16)))
            upd = jnp.where(same_head, _dot_tn(v[r], k_state[r]), 0.0)
            decay = e_q[(j + 1) * sub - 1:(j + 1) * sub, :]
            st = decay * st + upd if exact else decay * (st + upd)
        st_ref[b, pair] = st
        o = jnp.concatenate(outs, axis=0)

        o = o * lax.rsqrt(_head_mean_sq(o, ones) + EPS) * gain_ref[:, sl]
        g = g_ref[b, :, sl].astype(F32)
        o_ref[b, :, sl] = (o * (g * _sigmoid(g))).astype(BF16)

    def all_tiles(exact):
        for b in range(nb):
            for pair in range(H_HGRN // 2):
                pair_tile(b, pair, exact)

    pl.when(exact_fallback)(functools.partial(all_tiles, True))
    pl.when(jnp.logical_not(exact_fallback))(functools.partial(all_tiles, False))


def _hgrn(pa, za, lb, gain, mats, ones, batch, seq_len, layer):
    n = pa.shape[0]
    tt = TT_HGRN
    nb = NB_HGRN
    const = lambda b, t: (0, 0)
    tile_f32 = pltpu.VMEM((nb, tt, D_HGRN), F32)
    pa3 = pa.reshape(batch, seq_len, 3 * D_HGRN)
    blk = lambda j: pl.BlockSpec((nb, tt, D_HGRN), lambda b, t: (b, t, j))
    out = pl.pallas_call(
        functools.partial(_hgrn_kernel, layer=layer),
        grid=(batch // nb, seq_len // tt),
        in_specs=[
            blk(0), blk(1), blk(2), blk(0),
            pl.BlockSpec(lb.shape, const),
            pl.BlockSpec((1, D_HGRN), const),
            pl.BlockSpec(mats.shape, lambda b, t: (0, 0, 0)),
            pl.BlockSpec((LANES, LANES), const),
        ],
        out_specs=blk(0),
        out_shape=jax.ShapeDtypeStruct((batch, seq_len, D_HGRN), BF16),
        scratch_shapes=[pltpu.VMEM((nb, H_HGRN // 2, LANES, LANES), F32),
                        tile_f32, tile_f32, tile_f32],
        compiler_params=pltpu.CompilerParams(
            dimension_semantics=("arbitrary", "arbitrary"), vmem_limit_bytes=VMEM_LIMIT),
        name="hgrn",
    )(pa3, pa3, pa3, za.reshape(batch, seq_len, D_HGRN), lb, gain, mats, ones)
    return out.reshape(n, D_HGRN)


def _ret_kernel(q_ref, k_ref, v_ref, g_ref, dec_ref, qs_ref, ks_ref, cd_ref, gain_ref, ones_ref,
                o_ref, st_ref):
    @pl.when(pl.program_id(1) == 0)
    def _():
        st_ref[...] = jnp.zeros_like(st_ref)

    c = q_ref.shape[0]
    first = _first_head_lanes((c, LANES))
    rr = lax.broadcasted_iota(jnp.int32, (LANES, LANES), 0)
    cc = lax.broadcasted_iota(jnp.int32, (LANES, LANES), 1)
    same_head = ((rr ^ cc) & HEAD_DIM) == 0
    ones = ones_ref[...]

    for pair in range(H_RET // 2):
        sl = slice(pair * LANES, (pair + 1) * LANES)
        q = q_ref[:, sl]
        k = k_ref[:, sl]
        v = v_ref[:, sl]
        intra = []
        for hh, keep in enumerate((first, ~first)):
            s = _dot_nt(jnp.where(keep, q, jnp.zeros_like(q)), k) * dec_ref[2 * pair + hh]
            intra.append(_dot(s.astype(BF16), v))
        st = st_ref[pair]
        q_dec = (q.astype(F32) * qs_ref[:, sl]).astype(BF16)
        o = jnp.where(first, intra[0], intra[1]) + _dot_nt(q_dec, st.astype(BF16))
        k_dec = (k.astype(F32) * ks_ref[:, sl]).astype(BF16)
        st_ref[pair] = cd_ref[:, sl] * st + jnp.where(same_head, _dot_tn(v, k_dec), 0.0)

        o = o * lax.rsqrt(_head_mean_sq(o, ones) + EPS) * gain_ref[:, sl]
        g = g_ref[:, sl].astype(F32)
        o_ref[:, sl] = (o * (g * _sigmoid(g))).astype(BF16)


def _retention(pc, dec, qs, ks, cd, gain, ones, batch, seq_len):
    n = pc.shape[0]
    c = C_RET
    tps = seq_len // c
    const2 = lambda b, t: (0, 0)
    blk = lambda j: pl.BlockSpec((c, D_RET), lambda b, t: (b * tps + t, j))
    return pl.pallas_call(
        _ret_kernel,
        grid=(batch, tps),
        in_specs=[
            blk(0), blk(1), blk(2), blk(3),
            pl.BlockSpec((H_RET, c, c), lambda b, t: (0, 0, 0)),
            pl.BlockSpec((c, D_RET), const2),
            pl.BlockSpec((c, D_RET), const2),
            pl.BlockSpec((1, D_RET), const2),
            pl.BlockSpec((1, D_RET), const2),
            pl.BlockSpec((LANES, LANES), const2),
        ],
        out_specs=pl.BlockSpec((c, D_RET), lambda b, t: (b * tps + t, 0)),
        out_shape=jax.ShapeDtypeStruct((n, D_RET), BF16),
        scratch_shapes=[pltpu.VMEM((H_RET // 2, LANES, LANES), F32)],
        compiler_params=pltpu.CompilerParams(
            dimension_semantics=("arbitrary", "arbitrary"), vmem_limit_bytes=VMEM_LIMIT),
        name="retention",
    )(pc, pc, pc, pc, dec, qs, ks, cd, gain, ones)


def _fox_kernel(qt_ref, k_ref, vt_ref, g_ref, o_ref, s_scr, p_scr, m_scr, acc_scr):
    qi = pl.program_id(2)
    tq = o_ref.shape[0]

    m_scr[...] = jnp.full_like(m_scr, -jnp.inf)
    acc_scr[...] = jnp.zeros_like(acc_scr)
    p_scr[1] = jnp.zeros_like(p_scr[1])

    def scores(kj, par):
        start = pl.multiple_of(kj * tq, tq)
        for hh in range(2):
            slot = slice(hh * LANES, (hh + 1) * LANES)
            s_scr[par, hh] = _dot(k_ref[pl.ds(start, tq), slot], qt_ref[0, slot, :])

    def softmax(par, masked):
        if masked:
            key = lax.broadcasted_iota(jnp.int32, (tq, tq), 0)
            qry = lax.broadcasted_iota(jnp.int32, (tq, tq), 1)
            keep = qry >= key
        alphas = []
        for hh in range(2):
            s = s_scr[par, hh]
            if masked:
                s = jnp.where(keep, s, MASK_VALUE)
            m_old = m_scr[hh]
            m_new = jnp.maximum(m_old, jnp.max(s, axis=0, keepdims=True))
            p_scr[par, hh] = jnp.exp2(s - m_new).astype(BF16)
            alphas.append(jnp.exp2(m_old - m_new))
            m_scr[hh] = m_new
        return alphas

    def weighted_values(kj, par):
        start = pl.multiple_of(kj * tq, tq)
        return [_dot(vt_ref[0, hh * VT_ROWS:(hh + 1) * VT_ROWS, pl.ds(start, tq)],
                     p_scr[par, hh]) for hh in range(2)]

    def step(j, par, masked, has_next):
        pv = weighted_values(jnp.maximum(j - 1, 0), 1 - par)
        alphas = softmax(par, masked)
        if has_next:
            scores(j + 1, 1 - par)
        for hh in range(2):
            acc_scr[hh] = alphas[hh] * (acc_scr[hh] + pv[hh])

    scores(0, 0)

    def two_steps(i, carry):
        step(2 * i, 0, masked=False, has_next=True)
        step(2 * i + 1, 1, masked=False, has_next=True)
        return carry

    lax.fori_loop(0, qi // 2, two_steps, 0)

    @pl.when(qi % 2 == 1)
    def _():
        step(qi - 1, 0, masked=False, has_next=True)

    def finish(par):
        step(qi, par, masked=True, has_next=False)
        pv = weighted_values(qi, par)
        heads = []
        for hh in range(2):
            total = acc_scr[hh] + pv[hh]
            heads.append(total[:HEAD_DIM] / total[HEAD_DIM:HEAD_DIM + 1])
        o_t = jnp.concatenate(heads, axis=0)
        o_ref[...] = (o_t.T * _sigmoid(g_ref[...].astype(F32))).astype(BF16)

    for par in range(2):
        pl.when(qi % 2 == par)(functools.partial(finish, par))


def _fox(qt, ka, vt, gb, batch, seq_len):
    n = ka.shape[0]
    tq = TQ_FOX
    nq = seq_len // tq
    pairs = H_FOX // 2
    return pl.pallas_call(
        _fox_kernel,
        grid=(batch, pairs, nq),
        in_specs=[
            pl.BlockSpec((1, 2 * LANES, tq), lambda b, p, i: (b, p, i)),
            pl.BlockSpec((seq_len, 2 * LANES), lambda b, p, i: (b, p)),
            pl.BlockSpec((1, 2 * VT_ROWS, seq_len), lambda b, p, i: (b, p, 0)),
            pl.BlockSpec((tq, LANES), lambda b, p, i: (b * nq + i, p)),
        ],
        out_specs=pl.BlockSpec((tq, LANES), lambda b, p, i: (b * nq + i, p)),
        out_shape=jax.ShapeDtypeStruct((n, D_FOX), BF16),
        scratch_shapes=[
            pltpu.VMEM((2, 2, tq, tq), F32),
            pltpu.VMEM((2, 2, tq, tq), BF16),
            pltpu.VMEM((2, 1, tq), F32),
            pltpu.VMEM((2, VT_ROWS, tq), F32),
        ],
        compiler_params=pltpu.CompilerParams(
            dimension_semantics=("arbitrary", "arbitrary", "arbitrary"),
            vmem_limit_bytes=VMEM_LIMIT),
        name="fox",
    )(qt, ka, vt, gb)


def _outproj_kernel(oa_ref, ob_ref, oc_ref, wa_ref, wb_ref, wc_ref, x_ref, g_ref, out_ref):
    mixed = (_dot(oa_ref[...], wa_ref[...]) + _dot(ob_ref[...], wb_ref[...])
             + _dot(oc_ref[...], wc_ref[...]))
    out_ref[...] = x_ref[...] + _rmsnorm_rows(mixed, g_ref[...])


def _outproj(oa, ob, oc, wa, wb, wc, x2, gpost):
    n = x2.shape[0]
    tm = TM_PROJ
    const = lambda i: (0, 0)
    row = lambda i: (i, 0)
    return pl.pallas_call(
        _outproj_kernel,
        grid=(n // tm,),
        in_specs=[
            pl.BlockSpec((tm, D_HGRN), row),
            pl.BlockSpec((tm, D_FOX), row),
            pl.BlockSpec((tm, D_RET), row),
            pl.BlockSpec((D_HGRN, D_MODEL), const),
            pl.BlockSpec((D_FOX, D_MODEL), const),
            pl.BlockSpec((D_RET, D_MODEL), const),
            pl.BlockSpec((tm, D_MODEL), row),
            pl.BlockSpec((1, D_MODEL), const),
        ],
        out_specs=pl.BlockSpec((tm, D_MODEL), row),
        out_shape=jax.ShapeDtypeStruct((n, D_MODEL), F32),
        compiler_params=pltpu.CompilerParams(
            dimension_semantics=("arbitrary",), vmem_limit_bytes=VMEM_LIMIT),
        name="outproj",
    )(oa, ob, oc, wa, wb, wc, x2, gpost)


def _ffn_kernel(x_ref, gpre_ref, wup_ref, cw_ref, cb_ref, wd_ref, gpost_ref, out_ref,
                xp_scr, act_scr, tail_scr, *, tiles_per_seq):
    i = pl.program_id(0)
    tm = x_ref.shape[0]
    groups = tm // SUBLANES
    fc = FC_FFN

    xp_scr[...] = pltpu.einshape("(ab)f->(ba)f", x_ref[...], a=SUBLANES)
    h = _rmsnorm_rows(xp_scr[...], gpre_ref[...]).astype(BF16)

    @pl.when(i == 0)
    def _():
        tail_scr[...] = jnp.zeros_like(tail_scr)

    seq_start = i % tiles_per_seq == 0
    first_sublane = lax.broadcasted_iota(jnp.int32, (SUBLANES, fc), 0) == 0

    def causal_conv(u, col, slot):
        prev = jnp.where(seq_start, 0.0, tail_scr[slot])
        last = u[tm - 2 * SUBLANES:, :]
        tail_scr[slot] = last
        head = []
        for g in range(2):
            rows = slice(g * SUBLANES, (g + 1) * SUBLANES)
            entering = prev[(g + 1) * SUBLANES - 1:(g + 1) * SUBLANES, :]
            head.append(jnp.where(first_sublane, entering, pltpu.roll(last[rows], 1, 0)))
        u1 = jnp.concatenate([head[1], u[:tm - SUBLANES]], axis=0)
        u2 = jnp.concatenate([head[0], head[1], u[:tm - 2 * SUBLANES]], axis=0)
        cols = slice(col, col + fc)
        return (cb_ref[:, cols] + cw_ref[0:1, cols] * u2 + cw_ref[1:2, cols] * u1
                + cw_ref[2:3, cols] * u)

    for c in range(D_FF // fc):
        gate = causal_conv(_dot(h, wup_ref[:, c * fc:(c + 1) * fc]), c * fc, 2 * c)
        val = causal_conv(_dot(h, wup_ref[:, D_FF + c * fc:D_FF + (c + 1) * fc]),
                          D_FF + c * fc, 2 * c + 1)
        act_scr[:, c * fc:(c + 1) * fc] = (gate * _sigmoid(gate) * val).astype(BF16)

    y = _dot(act_scr[...], wd_ref[...])
    res = xp_scr[...] + _rmsnorm_rows(y, gpost_ref[...])
    out_ref[...] = pltpu.einshape("(ba)f->(ab)f", res, a=SUBLANES)


def _ffn(x2, gpre, w_up, conv_w, conv_b, w_down, gpost, seq_len):
    n = x2.shape[0]
    tm = TM_FFN
    nf = D_FF // FC_FFN
    tiles_per_seq = seq_len // tm
    row = lambda i: (i, 0)
    resident = lambda shape: pl.BlockSpec(shape, lambda i: (0, 0), pipeline_mode=pl.Buffered(1))
    return pl.pallas_call(
        functools.partial(_ffn_kernel, tiles_per_seq=tiles_per_seq),
        grid=(n // tm,),
        in_specs=[
            pl.BlockSpec((tm, D_MODEL), row),
            resident((1, D_MODEL)),
            resident((D_MODEL, 2 * D_FF)),
            resident((CONV_W, 2 * D_FF)),
            resident((1, 2 * D_FF)),
            resident((D_FF, D_MODEL)),
            resident((1, D_MODEL)),
        ],
        out_specs=pl.BlockSpec((tm, D_MODEL), row),
        out_shape=jax.ShapeDtypeStruct((n, D_MODEL), F32),
        scratch_shapes=[
            pltpu.VMEM((tm, D_MODEL), F32),
            pltpu.VMEM((tm, D_FF), BF16),
            pltpu.VMEM((2 * nf, 2 * SUBLANES, FC_FFN), F32),
        ],
        compiler_params=pltpu.CompilerParams(
            dimension_semantics=("arbitrary",), vmem_limit_bytes=VMEM_LIMIT),
        name="ffn",
    )(x2, gpre, w_up, conv_w, conv_b, w_down, gpost)


def _block_ones(n):
    idx = jnp.arange(n) // HEAD_DIM
    return (idx[:, None] == idx[None, :]).astype(BF16)


def _lower_tri(n, block):
    r = jnp.arange(n)
    return ((r[:, None] >= r[None, :]) & (r[:, None] // block == r[None, :] // block)).astype(BF16)


def _rotary_tables(seq_len):
    inv_freq = 1.0 / (ROPE_BASE ** (jnp.arange(0, HEAD_DIM, 2, dtype=F32) / HEAD_DIM))
    ang = jnp.arange(seq_len, dtype=F32)[:, None] * inv_freq[None, :]
    cos, sin = jnp.cos(ang), jnp.sin(ang)
    cos_full = jnp.tile(jnp.concatenate([cos, cos], axis=-1), (1, H_RET))
    sin_signed = jnp.tile(jnp.concatenate([-sin, sin], axis=-1), (1, H_RET))
    return cos_full, sin_signed


def _retention_tables(c):
    log_gamma = jnp.log1p(-jnp.exp2(-5.0 - jnp.arange(H_RET, dtype=F32)))
    pos = jnp.arange(c, dtype=F32)
    rel = pos[:, None] - pos[None, :]
    dec = jnp.where(rel >= 0, jnp.exp(log_gamma[:, None, None] * jnp.maximum(rel, 0.0)), 0.0)
    per_lane = lambda a: jnp.repeat(a, HEAD_DIM, axis=-1)
    qs = per_lane(jnp.exp(log_gamma[None, :] * (pos[:, None] + 1.0)))
    ks = per_lane(jnp.exp(log_gamma[None, :] * (c - 1.0 - pos[:, None])))
    cd = per_lane(jnp.exp(log_gamma * c)[None, :])
    return dec, qs, ks, cd


def _reorder_in_weight(w):
    widths = [D_HGRN] * 4 + [D_FOX] * 3 + [H_FOX] + [D_FOX] + [D_RET] * 4
    offs = [0]
    for wd in widths:
        offs.append(offs[-1] + wd)
    part = lambda j: w[:, offs[j]:offs[j + 1]]
    a_q, a_f, a_i, a_g, b_q, b_k, b_v, b_f, b_g, c_q, c_k, c_v, c_g = (part(j) for j in range(13))
    b_f = jnp.pad(b_f, ((0, 0), (0, FGATE_PAD - H_FOX)))
    return jnp.concatenate(
        [a_q, a_i, a_g, a_f, b_q, b_k, b_v, b_g, b_f, c_q, c_k, c_v, c_g], axis=-1).astype(BF16)


def kernel(x, w_in, b_fox_f, fox_q_gain, fox_k_gain, hgrn_lb, hgrn_out_gain, ret_out_gain, w_out,
           g_mix_pre, g_mix_post, w_up, conv_w, conv_b, w_down, g_ffn_pre, g_ffn_post):
    batch, seq_len, d_model = x.shape
    assert d_model == D_MODEL
    assert seq_len % TM_FFN == 0 and seq_len % TQ_FOX == 0 and batch % NB_HGRN == 0
    depth = w_in.shape[0]
    n = batch * seq_len

    cos_full, sin_signed = _rotary_tables(seq_len)
    dec, qs, ks, cd = _retention_tables(C_RET)
    tri_proj = _lower_tri(TM_PROJ, TM_PROJ)
    mats_hgrn = _hgrn_matrices(TT_HGRN)
    ones_pair = _block_ones(LANES)
    lb = hgrn_lb.astype(F32)

    x2 = x.reshape(n, D_MODEL)
    for l in range(depth):
        w = _reorder_in_weight(w_in[l])
        bf = jnp.pad(b_fox_f[l].astype(F32), (0, FGATE_PAD - H_FOX))[None, :]
        qg = jnp.tile(fox_q_gain[l].astype(F32), H_FOX)[None, :]
        kg = jnp.tile(fox_k_gain[l].astype(F32), H_FOX)[None, :]
        pa, za, qt, ka, vt, gb, pc = _inproj(x2, g_mix_pre[l][None, :], w, bf, qg, kg, cos_full,
                                             sin_signed, tri_proj, ones_pair, seq_len)

        o_a = _hgrn(pa, za, lb, hgrn_out_gain[l].reshape(1, D_HGRN), mats_hgrn, ones_pair,
                    batch, seq_len, l)
        o_b = _fox(qt, ka, vt, gb, batch, seq_len)
        o_c = _retention(pc, dec, qs, ks, cd, ret_out_gain[l].reshape(1, D_RET), ones_pair,
                         batch, seq_len)

        wo = w_out[l].astype(BF16)
        x2 = _outproj(o_a, o_b, o_c, wo[:D_HGRN], wo[D_HGRN:D_HGRN + D_FOX],
                      wo[D_HGRN + D_FOX:], x2, g_mix_post[l][None, :])
        x2 = _ffn(x2, g_ffn_pre[l][None, :], w_up[l].astype(BF16), conv_w[l], conv_b[l][None, :],
                  w_down[l].astype(BF16), g_ffn_post[l][None, :], seq_len)
    return x2.reshape(batch, seq_len, D_MODEL)
```

```python
import functools

import jax
import jax.numpy as jnp
from jax import lax
from jax.experimental import pallas as pl
from jax.experimental.pallas import tpu as pltpu

F32 = jnp.float32
BF16 = jnp.bfloat16

D_MODEL = 1024
HEAD_DIM = 64
H_HGRN, H_FOX, H_RET = 6, 6, 4
D_HGRN, D_FOX, D_RET = H_HGRN * HEAD_DIM, H_FOX * HEAD_DIM, H_RET * HEAD_DIM
D_FF = 2816
CONV_W = 3
ROPE_BASE = 10000.0
EPS = 1e-6
MIN_FORGET = 1e-12
MASK_VALUE = -1e30

LANES = 128
SUBLANES = 8
FGATE_PAD = LANES
BF16_SUBLANES = 16
VT_ROWS = HEAD_DIM + BF16_SUBLANES
LOG2E = 1.4426950408889634

_A_MAIN = (0, 3 * D_HGRN)
_A_GATE = (_A_MAIN[1], _A_MAIN[1] + D_HGRN)
_B_Q = (_A_GATE[1], _A_GATE[1] + D_FOX)
_B_K = (_B_Q[1], _B_Q[1] + D_FOX)
_B_V = (_B_K[1], _B_K[1] + D_FOX)
_B_G = (_B_V[1], _B_V[1] + D_FOX)
_B_F = (_B_G[1], _B_G[1] + FGATE_PAD)
_C_Q = (_B_F[1], _B_F[1] + D_RET)
_C_K = (_C_Q[1], _C_Q[1] + D_RET)
_C_VG = (_C_K[1], _C_K[1] + 2 * D_RET)
IN_COLS_PADDED = _C_VG[1]

TM_PROJ = 512
TM_FFN = 1024
FC_FFN = 256
TT_HGRN = 256
NB_HGRN = 4
SUB_HGRN = 32
SAFE_DIAG = 4
FAST_PATH_MAX_DECAY = 60.0
C_RET = 256
NB_RET = 4
TQ_FOX = 512
VMEM_LIMIT = 56 * 1024 * 1024


def _dot(a, b):
    return jnp.dot(a, b, preferred_element_type=F32)


def _dot_nt(a, b):
    return lax.dot_general(a, b, (((1,), (1,)), ((), ())), preferred_element_type=F32)


def _dot_tn(a, b):
    return lax.dot_general(a, b, (((0,), (0,)), ((), ())), preferred_element_type=F32)


def _dot_01_lhs(m01, x):
    hi = x.astype(BF16)
    r = x - hi.astype(F32)
    mid = r.astype(BF16)
    lo = (r - mid.astype(F32)).astype(BF16)
    return _dot(m01, hi) + _dot(m01, mid) + _dot(m01, lo)


def _head_mean_sq(y, ones_bd):
    sq = y * y
    hi = sq.astype(BF16)
    lo = (sq - hi.astype(F32)).astype(BF16)
    return (_dot(hi, ones_bd) + _dot(lo, ones_bd)) * (1.0 / HEAD_DIM)


def _rmsnorm_rows(x, gain):
    return x * lax.rsqrt(jnp.mean(x * x, axis=-1, keepdims=True) + EPS) * gain


def _sigmoid(x):
    return 1.0 / (1.0 + jnp.exp(-x))


def _first_head_lanes(shape):
    return (lax.broadcasted_iota(jnp.int32, shape, len(shape) - 1) & HEAD_DIM) == 0


def _bf16_terms(x):
    hi = x.astype(BF16).astype(F32)
    mid = (x - hi).astype(BF16).astype(F32)
    lo = (x - hi - mid).astype(BF16).astype(F32)
    return hi, mid, lo


def _inproj_kernel(x_ref, gpre_ref, w_ref, bf_ref, qg_ref, kg_ref, cos_ref, sin_ref, tri_ref,
                   ones_ref, pa_ref, za_ref, qt_ref, ka_ref, vt_ref, gb_ref, pc_ref, carry_ref,
                   *, tiles_per_seq):
    i = pl.program_id(0)
    h = _rmsnorm_rows(x_ref[...], gpre_ref[...]).astype(BF16)
    tm = h.shape[0]

    def proj(cols):
        return _dot(h, w_ref[:, cols[0]:cols[1]])

    pa_ref[...] = proj(_A_MAIN).astype(BF16)
    za_ref[...] = proj(_A_GATE)

    zf = proj(_B_F) + bf_ref[...]
    log_f = -(jnp.maximum(-zf, 0.0) + jnp.log1p(jnp.exp(-jnp.abs(zf))))

    @pl.when(i % tiles_per_seq == 0)
    def _():
        carry_ref[...] = jnp.zeros_like(carry_ref)

    cum = _dot_01_lhs(tri_ref[...], log_f) + carry_ref[...]
    carry_ref[...] = cum[tm - 1:, :]

    ones = ones_ref[...]
    bq = proj(_B_Q)
    bk = proj(_B_K)
    lane = lax.broadcasted_iota(jnp.int32, (tm, LANES), 1)
    is_feature = lane < HEAD_DIM
    is_one_q = (lane >= HEAD_DIM + 3) & (lane < HEAD_DIM + 6)
    is_one_k = (lane >= HEAD_DIM) & (lane < HEAD_DIM + 3)
    for pair in range(H_FOX // 2):
        sl = slice(pair * LANES, (pair + 1) * LANES)
        q2 = bq[:, sl]
        q2 = (q2 * lax.rsqrt(_head_mean_sq(q2, ones) + EPS) * qg_ref[:, sl]
              * (HEAD_DIM ** -0.5 * LOG2E))
        k2 = bk[:, sl]
        k2 = k2 * lax.rsqrt(_head_mean_sq(k2, ones) + EPS) * kg_ref[:, sl]
        for odd in range(2):
            head = 2 * pair + odd
            slot = slice(head * LANES, (head + 1) * LANES)
            qh = pltpu.roll(q2, HEAD_DIM, 1) if odd else q2
            kh = pltpu.roll(k2, HEAD_DIM, 1) if odd else k2
            hi, mid, lo = _bf16_terms(
                jnp.broadcast_to(cum[:, head:head + 1], (tm, LANES)) * LOG2E)
            q_bias = jnp.where(lane == HEAD_DIM, hi, jnp.where(lane == HEAD_DIM + 1, mid, jnp.where(
                lane == HEAD_DIM + 2, lo, jnp.where(is_one_q, 1.0, 0.0))))
            k_bias = jnp.where(lane == HEAD_DIM + 3, -hi, jnp.where(lane == HEAD_DIM + 4, -mid, jnp.where(
                lane == HEAD_DIM + 5, -lo, jnp.where(is_one_k, 1.0, 0.0))))
            qt_ref[0, slot, :] = jnp.where(is_feature, qh, q_bias).T.astype(BF16)
            ka_ref[:, slot] = jnp.where(is_feature, kh, k_bias).astype(BF16)
    v_t = proj(_B_V).T.astype(BF16)
    ones_row = jnp.where(lax.broadcasted_iota(jnp.int32, (BF16_SUBLANES, tm), 0) == 0,
                         1.0, 0.0).astype(BF16)
    for head in range(H_FOX):
        vt_ref[0, head * VT_ROWS:head * VT_ROWS + HEAD_DIM, :] = (
            v_t[head * HEAD_DIM:(head + 1) * HEAD_DIM, :])
        vt_ref[0, head * VT_ROWS + HEAD_DIM:(head + 1) * VT_ROWS, :] = ones_row
    gb_ref[...] = proj(_B_G).astype(BF16)

    cos = cos_ref[...]
    sin = sin_ref[...]
    low_half = (lax.broadcasted_iota(jnp.int32, cos.shape, 1) & (HEAD_DIM // 2)) == 0

    def rotary(y):
        swapped = jnp.where(low_half, pltpu.roll(y, D_RET - HEAD_DIM // 2, 1),
                            pltpu.roll(y, HEAD_DIM // 2, 1))
        return y * cos + swapped * sin

    pc_ref[:, 0:D_RET] = rotary(proj(_C_Q)).astype(BF16)
    pc_ref[:, D_RET:2 * D_RET] = (rotary(proj(_C_K)) * (HEAD_DIM ** -0.5)).astype(BF16)
    pc_ref[:, 2 * D_RET:4 * D_RET] = proj(_C_VG).astype(BF16)


def _inproj(x2, gpre, w, bf, qg, kg, cos, sin, tri, ones, seq_len):
    n = x2.shape[0]
    batch = n // seq_len
    tm = TM_PROJ
    tiles_per_seq = seq_len // tm
    const = lambda i: (0, 0)
    row = lambda i: (i, 0)
    seq_t = lambda i: (i // tiles_per_seq, 0, i % tiles_per_seq)
    return pl.pallas_call(
        functools.partial(_inproj_kernel, tiles_per_seq=tiles_per_seq),
        grid=(n // tm,),
        in_specs=[
            pl.BlockSpec((tm, D_MODEL), row),
            pl.BlockSpec((1, D_MODEL), const),
            pl.BlockSpec((D_MODEL, IN_COLS_PADDED), const),
            pl.BlockSpec((1, FGATE_PAD), const),
            pl.BlockSpec((1, D_FOX), const),
            pl.BlockSpec((1, D_FOX), const),
            pl.BlockSpec((tm, D_RET), lambda i: (i % tiles_per_seq, 0)),
            pl.BlockSpec((tm, D_RET), lambda i: (i % tiles_per_seq, 0)),
            pl.BlockSpec((tm, tm), const),
            pl.BlockSpec((LANES, LANES), const),
        ],
        out_specs=[
            pl.BlockSpec((tm, 3 * D_HGRN), row),
            pl.BlockSpec((tm, D_HGRN), row),
            pl.BlockSpec((1, H_FOX * LANES, tm), seq_t),
            pl.BlockSpec((tm, H_FOX * LANES), row),
            pl.BlockSpec((1, H_FOX * VT_ROWS, tm), seq_t),
            pl.BlockSpec((tm, D_FOX), row),
            pl.BlockSpec((tm, 4 * D_RET), row),
        ],
        out_shape=[
            jax.ShapeDtypeStruct((n, 3 * D_HGRN), BF16),
            jax.ShapeDtypeStruct((n, D_HGRN), F32),
            jax.ShapeDtypeStruct((batch, H_FOX * LANES, seq_len), BF16),
            jax.ShapeDtypeStruct((n, H_FOX * LANES), BF16),
            jax.ShapeDtypeStruct((batch, H_FOX * VT_ROWS, seq_len), BF16),
            jax.ShapeDtypeStruct((n, D_FOX), BF16),
            jax.ShapeDtypeStruct((n, 4 * D_RET), BF16),
        ],
        scratch_shapes=[pltpu.VMEM((1, FGATE_PAD), F32)],
        compiler_params=pltpu.CompilerParams(
            dimension_semantics=("arbitrary",), vmem_limit_bytes=VMEM_LIMIT),
        name="inproj",
    )(x2, gpre, w, bf, qg, kg, cos, sin, tri, ones)


_M_CUM, _M_AFTER, _M_DIAG, _M_MERGE0 = 0, 1, 2, 3
_MERGE_BLOCKS = (2 * SAFE_DIAG, 4 * SAFE_DIAG, 8 * SAFE_DIAG)
assert _MERGE_BLOCKS[-1] == SUB_HGRN


def _hgrn_matrices(tt):
    t = jnp.arange(tt)[:, None]
    u = jnp.arange(tt)[None, :]
    same = lambda m: (t // m) == (u // m)
    mats = [same(SUB_HGRN) & (u <= t), same(SUB_HGRN) & (u > t),
            same(SAFE_DIAG) & (u <= t) & (u > (t // SAFE_DIAG) * SAFE_DIAG)]
    for m in _MERGE_BLOCKS:
        mid = (t // m) * m + m // 2 - 1
        mats.append(same(m) & (((t > mid) & (u > mid) & (u <= t)) | ((t <= mid) & (u > t) & (u <= mid))))
    return jnp.stack(mats).astype(BF16)


def _hgrn_kernel(q_ref, v_ref, g_ref, z_ref, lb_ref, gain_ref, mats_ref, ones_ref, o_ref, st_ref,
                 logf_scr, k_scr, cum_scr, *, layer):
    @pl.when(pl.program_id(1) == 0)
    def _():
        st_ref[...] = jnp.zeros_like(st_ref)

    nb, tt = q_ref.shape[0], q_ref.shape[1]
    sub = SUB_HGRN

    lb = lb_ref[...]
    e = jnp.exp(lb - jnp.max(lb, axis=0, keepdims=True))
    p = e / jnp.sum(e, axis=0, keepdims=True)
    run = p[0:1]
    for i in range(1, layer + 1):
        run = run + p[i:i + 1]
    lower = run - p[0:1]

    lowest = None
    for b in range(nb):
        sg = _sigmoid(z_ref[b])
        f = lower + (1.0 - lower) * sg
        log_f = jnp.log(jnp.maximum(f, MIN_FORGET))
        cum = _dot_01_lhs(mats_ref[_M_CUM], log_f)
        logf_scr[b] = log_f
        k_scr[b] = (1.0 - lower) * (1.0 - sg)
        cum_scr[b] = cum
        low_b = jnp.min(cum)
        lowest = low_b if lowest is None else jnp.minimum(lowest, low_b)
    exact_fallback = lowest < -FAST_PATH_MAX_DECAY

    row = lax.broadcasted_iota(jnp.int32, (tt, tt), 0)
    col = lax.broadcasted_iota(jnp.int32, (tt, tt), 1)

    def diag_mask(m):
        in_block = row & (m - 1)
        dist = row - col
        return ((in_block - dist) | dist) >= 0

    def merge_mask(m):
        half = m // 2
        return (((row ^ col) & ~(m - 1)) | ((row & half) ^ half) | (col & half)) == 0

    first = _first_head_lanes((tt, LANES))
    rr = lax.broadcasted_iota(jnp.int32, (LANES, LANES), 0)
    cc = lax.broadcasted_iota(jnp.int32, (LANES, LANES), 1)
    same_head = ((rr ^ cc) & HEAD_DIM) == 0
    ones = ones_ref[...]

    def pair_tile(b, pair, exact):
        sl = slice(pair * LANES, (pair + 1) * LANES)
        q = q_ref[b, :, sl].astype(F32)
        k = k_scr[b, :, sl]
        cum = cum_scr[b, :, sl]
        v = v_ref[b, :, sl]
        e_q = jnp.exp(cum)
        qt = (q * e_q).astype(BF16)
        if exact:
            log_f = logf_scr[b, :, sl]
            since = _dot_01_lhs(mats_ref[_M_DIAG], log_f)
            pieces = [((q * jnp.exp(since)).astype(BF16), (k * jnp.exp(-since)).astype(BF16),
                       diag_mask(SAFE_DIAG))]
            for idx, m in enumerate(_MERGE_BLOCKS):
                w = jnp.exp(_dot_01_lhs(mats_ref[_M_MERGE0 + idx], log_f))
                pieces.append(((q * w).astype(BF16), (k * w).astype(BF16), merge_mask(m)))
            k_state = (k * jnp.exp(_dot_01_lhs(mats_ref[_M_AFTER], log_f))).astype(BF16)
        else:
            k_state = (k * jnp.exp(-cum)).astype(BF16)
            pieces = [(qt, k_state, diag_mask(sub))]

        intra = []
        for keep in (first, ~first):
            s = jnp.zeros((tt, tt), F32)
            for q_f, k_f, mask in pieces:
                s = jnp.where(mask, _dot_nt(jnp.where(keep, q_f, jnp.zeros_like(q_f)), k_f), s)
            intra.append(_dot(s.astype(BF16), v))
        o = jnp.where(first, intra[0], intra[1])

        st = st_ref[b, pair]
        outs = []
        for j in range(tt // sub):
            r = slice(j * sub, (j + 1) * sub)
            outs.append(o[r] + _dot_nt(qt[r], st.astype(BF16)))
            upd = jnp.where(same_head, _dot_tn(v[r], k_state[r]), 0.0)
            decay = e_q[(j + 1) * sub - 1:(j + 1) * sub, :]
            st = decay * st + upd if exact else decay * (st + upd)
        st_ref[b, pair] = st
        o = jnp.concatenate(outs, axis=0)

        o = o * lax.rsqrt(_head_mean_sq(o, ones) + EPS) * gain_ref[:, sl]
        g = g_ref[b, :, sl].astype(F32)
        o_ref[b, :, sl] = (o * (g * _sigmoid(g))).astype(BF16)

    def all_tiles(exact):
        for b in range(nb):
            for pair in range(H_HGRN // 2):
                pair_tile(b, pair, exact)

    pl.when(exact_fallback)(functools.partial(all_tiles, True))
    pl.when(jnp.logical_not(exact_fallback))(functools.partial(all_tiles, False))


def _hgrn(pa, za, lb, gain, mats, ones, batch, seq_len, layer):
    n = pa.shape[0]
    tt = TT_HGRN
    nb = NB_HGRN
    const = lambda b, t: (0, 0)
    tile_f32 = pltpu.VMEM((nb, tt, D_HGRN), F32)
    pa3 = pa.reshape(batch, seq_len, 3 * D_HGRN)
    blk = lambda j: pl.BlockSpec((nb, tt, D_HGRN), lambda b, t: (b, t, j))
    out = pl.pallas_call(
        functools.partial(_hgrn_kernel, layer=layer),
        grid=(batch // nb, seq_len // tt),
        in_specs=[
            blk(0), blk(1), blk(2), blk(0),
            pl.BlockSpec(lb.shape, const),
            pl.BlockSpec((1, D_HGRN), const),
            pl.BlockSpec(mats.shape, lambda b, t: (0, 0, 0)),
            pl.BlockSpec((LANES, LANES), const),
        ],
        out_specs=blk(0),
        out_shape=jax.ShapeDtypeStruct((batch, seq_len, D_HGRN), BF16),
        scratch_shapes=[pltpu.VMEM((nb, H_HGRN // 2, LANES, LANES), F32),
                        tile_f32, tile_f32, tile_f32],
        compiler_params=pltpu.CompilerParams(
            dimension_semantics=("arbitrary", "arbitrary"), vmem_limit_bytes=VMEM_LIMIT),
        name="hgrn",
    )(pa3, pa3, pa3, za.reshape(batch, seq_len, D_HGRN), lb, gain, mats, ones)
    return out.reshape(n, D_HGRN)


def _ret_kernel(q_ref, k_ref, v_ref, g_ref, dec_ref, qs_ref, ks_ref, cd_ref, gain_ref, ones_ref,
                o_ref, st_ref):
    @pl.when(pl.program_id(1) == 0)
    def _():
        st_ref[...] = jnp.zeros_like(st_ref)

    nb, c = q_ref.shape[0], q_ref.shape[1]
    first = _first_head_lanes((c, LANES))
    rr = lax.broadcasted_iota(jnp.int32, (LANES, LANES), 0)
    cc = lax.broadcasted_iota(jnp.int32, (LANES, LANES), 1)
    same_head = ((rr ^ cc) & HEAD_DIM) == 0
    ones = ones_ref[...]

    for b, pair in [(b, pair) for b in range(nb) for pair in range(H_RET // 2)]:
        sl = slice(pair * LANES, (pair + 1) * LANES)
        q = q_ref[b, :, sl]
        k = k_ref[b, :, sl]
        v = v_ref[b, :, sl]
        intra = []
        for hh, keep in enumerate((first, ~first)):
            s = _dot_nt(jnp.where(keep, q, jnp.zeros_like(q)), k) * dec_ref[2 * pair + hh]
            intra.append(_dot(s.astype(BF16), v))
        st = st_ref[b, pair]
        q_dec = (q.astype(F32) * qs_ref[:, sl]).astype(BF16)
        o = jnp.where(first, intra[0], intra[1]) + _dot_nt(q_dec, st.astype(BF16))
        k_dec = (k.astype(F32) * ks_ref[:, sl]).astype(BF16)
        st_ref[b, pair] = cd_ref[:, sl] * st + jnp.where(same_head, _dot_tn(v, k_dec), 0.0)

        o = o * lax.rsqrt(_head_mean_sq(o, ones) + EPS) * gain_ref[:, sl]
        g = g_ref[b, :, sl].astype(F32)
        o_ref[b, :, sl] = (o * (g * _sigmoid(g))).astype(BF16)


def _retention(pc, dec, qs, ks, cd, gain, ones, batch, seq_len):
    n = pc.shape[0]
    c = C_RET
    nb = NB_RET
    const2 = lambda b, t: (0, 0)
    pc3 = pc.reshape(batch, seq_len, 4 * D_RET)
    blk = lambda j: pl.BlockSpec((nb, c, D_RET), lambda b, t: (b, t, j))
    out = pl.pallas_call(
        _ret_kernel,
        grid=(batch // nb, seq_len // c),
        in_specs=[
            blk(0), blk(1), blk(2), blk(3),
            pl.BlockSpec((H_RET, c, c), lambda b, t: (0, 0, 0)),
            pl.BlockSpec((c, D_RET), const2),
            pl.BlockSpec((c, D_RET), const2),
            pl.BlockSpec((1, D_RET), const2),
            pl.BlockSpec((1, D_RET), const2),
            pl.BlockSpec((LANES, LANES), const2),
        ],
        out_specs=blk(0),
        out_shape=jax.ShapeDtypeStruct((batch, seq_len, D_RET), BF16),
        scratch_shapes=[pltpu.VMEM((nb, H_RET // 2, LANES, LANES), F32)],
        compiler_params=pltpu.CompilerParams(
            dimension_semantics=("arbitrary", "arbitrary"), vmem_limit_bytes=VMEM_LIMIT),
        name="retention",
    )(pc3, pc3, pc3, pc3, dec, qs, ks, cd, gain, ones)
    return out.reshape(n, D_RET)


def _fox_kernel(qt_ref, k_ref, vt_ref, g_ref, o_ref, s_scr, p_scr, m_scr, acc_scr):
    qi = pl.program_id(2)
    tq = o_ref.shape[0]

    m_scr[...] = jnp.full_like(m_scr, -jnp.inf)
    acc_scr[...] = jnp.zeros_like(acc_scr)
    p_scr[1] = jnp.zeros_like(p_scr[1])

    def scores(kj, par):
        start = pl.multiple_of(kj * tq, tq)
        for hh in range(2):
            slot = slice(hh * LANES, (hh + 1) * LANES)
            s_scr[par, hh] = _dot(k_ref[pl.ds(start, tq), slot], qt_ref[0, slot, :])

    def softmax(par, masked):
        if masked:
            key = lax.broadcasted_iota(jnp.int32, (tq, tq), 0)
            qry = lax.broadcasted_iota(jnp.int32, (tq, tq), 1)
            keep = qry >= key
        alphas = []
        for hh in range(2):
            s = s_scr[par, hh]
            if masked:
                s = jnp.where(keep, s, MASK_VALUE)
            m_old = m_scr[hh]
            m_new = jnp.maximum(m_old, jnp.max(s, axis=0, keepdims=True))
            p_scr[par, hh] = jnp.exp2(s - m_new).astype(BF16)
            alphas.append(jnp.exp2(m_old - m_new))
            m_scr[hh] = m_new
        return alphas

    def weighted_values(kj, par):
        start = pl.multiple_of(kj * tq, tq)
        return [_dot(vt_ref[0, hh * VT_ROWS:(hh + 1) * VT_ROWS, pl.ds(start, tq)],
                     p_scr[par, hh]) for hh in range(2)]

    def step(j, par, masked, has_next):
        pv = weighted_values(jnp.maximum(j - 1, 0), 1 - par)
        alphas = softmax(par, masked)
        if has_next:
            scores(j + 1, 1 - par)
        for hh in range(2):
            acc_scr[hh] = alphas[hh] * (acc_scr[hh] + pv[hh])

    scores(0, 0)

    def two_steps(i, carry):
        step(2 * i, 0, masked=False, has_next=True)
        step(2 * i + 1, 1, masked=False, has_next=True)
        return carry

    lax.fori_loop(0, qi // 2, two_steps, 0)

    @pl.when(qi % 2 == 1)
    def _():
        step(qi - 1, 0, masked=False, has_next=True)

    def finish(par):
        step(qi, par, masked=True, has_next=False)
        pv = weighted_values(qi, par)
        heads = []
        for hh in range(2):
            total = acc_scr[hh] + pv[hh]
            heads.append(total[:HEAD_DIM] / total[HEAD_DIM:HEAD_DIM + 1])
        o_t = jnp.concatenate(heads, axis=0)
        o_ref[...] = (o_t.T * _sigmoid(g_ref[...].astype(F32))).astype(BF16)

    for par in range(2):
        pl.when(qi % 2 == par)(functools.partial(finish, par))


def _fox(qt, ka, vt, gb, batch, seq_len):
    n = ka.shape[0]
    tq = TQ_FOX
    nq = seq_len // tq
    pairs = H_FOX // 2
    return pl.pallas_call(
        _fox_kernel,
        grid=(batch, pairs, nq),
        in_specs=[
            pl.BlockSpec((1, 2 * LANES, tq), lambda b, p, i: (b, p, i)),
            pl.BlockSpec((seq_len, 2 * LANES), lambda b, p, i: (b, p)),
            pl.BlockSpec((1, 2 * VT_ROWS, seq_len), lambda b, p, i: (b, p, 0)),
            pl.BlockSpec((tq, LANES), lambda b, p, i: (b * nq + i, p)),
        ],
        out_specs=pl.BlockSpec((tq, LANES), lambda b, p, i: (b * nq + i, p)),
        out_shape=jax.ShapeDtypeStruct((n, D_FOX), BF16),
        scratch_shapes=[
            pltpu.VMEM((2, 2, tq, tq), F32),
            pltpu.VMEM((2, 2, tq, tq), BF16),
            pltpu.VMEM((2, 1, tq), F32),
            pltpu.VMEM((2, VT_ROWS, tq), F32),
        ],
        compiler_params=pltpu.CompilerParams(
            dimension_semantics=("arbitrary", "arbitrary", "arbitrary"),
            vmem_limit_bytes=VMEM_LIMIT),
        name="fox",
    )(qt, ka, vt, gb)


def _mix_ffn_kernel(oa_ref, ob_ref, oc_ref, wa_ref, wb_ref, wc_ref, gmix_ref, x_ref, gpre_ref,
                    wup_ref, cw_ref, cb_ref, wd_ref, gpost_ref, out_ref, xp_scr, act_scr, tail_scr,
                    *, tiles_per_seq):
    i = pl.program_id(0)
    tm = x_ref.shape[0]
    fc = FC_FFN

    mixed = (_dot(oa_ref[...], wa_ref[...]) + _dot(ob_ref[...], wb_ref[...])
             + _dot(oc_ref[...], wc_ref[...]))
    x_mid = x_ref[...] + _rmsnorm_rows(mixed, gmix_ref[...])
    xp_scr[...] = pltpu.einshape("(ab)f->(ba)f", x_mid, a=SUBLANES)
    h = _rmsnorm_rows(xp_scr[...], gpre_ref[...]).astype(BF16)

    @pl.when(i == 0)
    def _():
        tail_scr[...] = jnp.zeros_like(tail_scr)

    seq_start = i % tiles_per_seq == 0
    first_sublane = lax.broadcasted_iota(jnp.int32, (SUBLANES, fc), 0) == 0

    def causal_conv(u, col, slot):
        prev = jnp.where(seq_start, 0.0, tail_scr[slot])
        last = u[tm - 2 * SUBLANES:, :]
        tail_scr[slot] = last
        head = []
        for g in range(2):
            rows = slice(g * SUBLANES, (g + 1) * SUBLANES)
            entering = prev[(g + 1) * SUBLANES - 1:(g + 1) * SUBLANES, :]
            head.append(jnp.where(first_sublane, entering, pltpu.roll(last[rows], 1, 0)))
        u1 = jnp.concatenate([head[1], u[:tm - SUBLANES]], axis=0)
        u2 = jnp.concatenate([head[0], head[1], u[:tm - 2 * SUBLANES]], axis=0)
        cols = slice(col, col + fc)
        return (cb_ref[:, cols] + cw_ref[0:1, cols] * u2 + cw_ref[1:2, cols] * u1
                + cw_ref[2:3, cols] * u)

    for c in range(D_FF // fc):
        gate = causal_conv(_dot(h, wup_ref[:, c * fc:(c + 1) * fc]), c * fc, 2 * c)
        val = causal_conv(_dot(h, wup_ref[:, D_FF + c * fc:D_FF + (c + 1) * fc]),
                          D_FF + c * fc, 2 * c + 1)
        act_scr[:, c * fc:(c + 1) * fc] = (gate * _sigmoid(gate) * val).astype(BF16)

    y = _dot(act_scr[...], wd_ref[...])
    res = xp_scr[...] + _rmsnorm_rows(y, gpost_ref[...])
    out_ref[...] = pltpu.einshape("(ba)f->(ab)f", res, a=SUBLANES)


def _mix_ffn(oa, ob, oc, wa, wb, wc, gmix, x2, gpre, w_up, conv_w, conv_b, w_down, gpost, seq_len):
    n = x2.shape[0]
    tm = TM_FFN
    nf = D_FF // FC_FFN
    tiles_per_seq = seq_len // tm
    row = lambda i: (i, 0)
    resident = lambda shape: pl.BlockSpec(shape, lambda i: (0, 0), pipeline_mode=pl.Buffered(1))
    return pl.pallas_call(
        functools.partial(_mix_ffn_kernel, tiles_per_seq=tiles_per_seq),
        grid=(n // tm,),
        in_specs=[
            pl.BlockSpec((tm, D_HGRN), row),
            pl.BlockSpec((tm, D_FOX), row),
            pl.BlockSpec((tm, D_RET), row),
            resident((D_HGRN, D_MODEL)),
            resident((D_FOX, D_MODEL)),
            resident((D_RET, D_MODEL)),
            resident((1, D_MODEL)),
            pl.BlockSpec((tm, D_MODEL), row),
            resident((1, D_MODEL)),
            resident((D_MODEL, 2 * D_FF)),
            resident((CONV_W, 2 * D_FF)),
            resident((1, 2 * D_FF)),
            resident((D_FF, D_MODEL)),
            resident((1, D_MODEL)),
        ],
        out_specs=pl.BlockSpec((tm, D_MODEL), row),
        out_shape=jax.ShapeDtypeStruct((n, D_MODEL), F32),
        scratch_shapes=[
            pltpu.VMEM((tm, D_MODEL), F32),
            pltpu.VMEM((tm, D_FF), BF16),
            pltpu.VMEM((2 * nf, 2 * SUBLANES, FC_FFN), F32),
        ],
        compiler_params=pltpu.CompilerParams(
            dimension_semantics=("arbitrary",), vmem_limit_bytes=VMEM_LIMIT),
        name="mix_ffn",
    )(oa, ob, oc, wa, wb, wc, gmix, x2, gpre, w_up, conv_w, conv_b, w_down, gpost)


def _block_ones(n):
    idx = jnp.arange(n) // HEAD_DIM
    return (idx[:, None] == idx[None, :]).astype(BF16)


def _lower_tri(n, block):
    r = jnp.arange(n)
    return ((r[:, None] >= r[None, :]) & (r[:, None] // block == r[None, :] // block)).astype(BF16)


def _rotary_tables(seq_len):
    inv_freq = 1.0 / (ROPE_BASE ** (jnp.arange(0, HEAD_DIM, 2, dtype=F32) / HEAD_DIM))
    ang = jnp.arange(seq_len, dtype=F32)[:, None] * inv_freq[None, :]
    cos, sin = jnp.cos(ang), jnp.sin(ang)
    cos_full = jnp.tile(jnp.concatenate([cos, cos], axis=-1), (1, H_RET))
    sin_signed = jnp.tile(jnp.concatenate([-sin, sin], axis=-1), (1, H_RET))
    return cos_full, sin_signed


def _retention_tables(c):
    log_gamma = jnp.log1p(-jnp.exp2(-5.0 - jnp.arange(H_RET, dtype=F32)))
    pos = jnp.arange(c, dtype=F32)
    rel = pos[:, None] - pos[None, :]
    dec = jnp.where(rel >= 0, jnp.exp(log_gamma[:, None, None] * jnp.maximum(rel, 0.0)), 0.0)
    per_lane = lambda a: jnp.repeat(a, HEAD_DIM, axis=-1)
    qs = per_lane(jnp.exp(log_gamma[None, :] * (pos[:, None] + 1.0)))
    ks = per_lane(jnp.exp(log_gamma[None, :] * (c - 1.0 - pos[:, None])))
    cd = per_lane(jnp.exp(log_gamma * c)[None, :])
    return dec, qs, ks, cd


def _reorder_in_weight_rows(w_t):
    widths = [D_HGRN] * 4 + [D_FOX] * 3 + [H_FOX] + [D_FOX] + [D_RET] * 4
    offs = [0]
    for wd in widths:
        offs.append(offs[-1] + wd)
    part = lambda j: w_t[offs[j]:offs[j + 1]]
    a_q, a_f, a_i, a_g, b_q, b_k, b_v, b_f, b_g, c_q, c_k, c_v, c_g = (part(j) for j in range(13))
    b_f = jnp.pad(b_f, ((0, FGATE_PAD - H_FOX), (0, 0)))
    return jnp.concatenate(
        [a_q, a_i, a_g, a_f, b_q, b_k, b_v, b_g, b_f, c_q, c_k, c_v, c_g], axis=0)


def _transpose_cast_kernel(w_ref, o_ref):
    o_ref[...] = w_ref[...].T.astype(BF16)


def _transpose_cast(w_t):
    rows = w_t.shape[0]
    blk = D_HGRN
    return pl.pallas_call(
        _transpose_cast_kernel,
        grid=(rows // blk,),
        in_specs=[pl.BlockSpec((blk, D_MODEL), lambda i: (i, 0))],
        out_specs=pl.BlockSpec((D_MODEL, blk), lambda i: (0, i)),
        out_shape=jax.ShapeDtypeStruct((D_MODEL, rows), BF16),
        compiler_params=pltpu.CompilerParams(
            dimension_semantics=("arbitrary",), vmem_limit_bytes=VMEM_LIMIT),
        name="wprep",
    )(w_t)


def kernel(x, w_in, b_fox_f, fox_q_gain, fox_k_gain, hgrn_lb, hgrn_out_gain, ret_out_gain, w_out,
           g_mix_pre, g_mix_post, w_up, conv_w, conv_b, w_down, g_ffn_pre, g_ffn_post):
    batch, seq_len, d_model = x.shape
    assert d_model == D_MODEL
    assert seq_len % TM_FFN == 0 and seq_len % TQ_FOX == 0
    assert batch % NB_HGRN == 0 and batch % NB_RET == 0
    depth = w_in.shape[0]
    n = batch * seq_len

    cos_full, sin_signed = _rotary_tables(seq_len)
    dec, qs, ks, cd = _retention_tables(C_RET)
    tri_proj = _lower_tri(TM_PROJ, TM_PROJ)
    mats_hgrn = _hgrn_matrices(TT_HGRN)
    ones_pair = _block_ones(LANES)
    lb = hgrn_lb.astype(F32)

    w_in_t = jnp.transpose(w_in, (2, 0, 1))

    x2 = x.reshape(n, D_MODEL)
    for l in range(depth):
        w = _transpose_cast(_reorder_in_weight_rows(w_in_t[:, l, :]))
        bf = jnp.pad(b_fox_f[l].astype(F32), (0, FGATE_PAD - H_FOX))[None, :]
        qg = jnp.tile(fox_q_gain[l].astype(F32), H_FOX)[None, :]
        kg = jnp.tile(fox_k_gain[l].astype(F32), H_FOX)[None, :]
        pa, za, qt, ka, vt, gb, pc = _inproj(x2, g_mix_pre[l][None, :], w, bf, qg, kg, cos_full,
                                             sin_signed, tri_proj, ones_pair, seq_len)

        o_a = _hgrn(pa, za, lb, hgrn_out_gain[l].reshape(1, D_HGRN), mats_hgrn, ones_pair,
                    batch, seq_len, l)
        o_b = _fox(qt, ka, vt, gb, batch, seq_len)
        o_c = _retention(pc, dec, qs, ks, cd, ret_out_gain[l].reshape(1, D_RET), ones_pair,
                         batch, seq_len)

        wo = w_out[l].astype(BF16)
        x2 = _mix_ffn(o_a, o_b, o_c, wo[:D_HGRN], wo[D_HGRN:D_HGRN + D_FOX], wo[D_HGRN + D_FOX:],
                      g_mix_post[l][None, :], x2, g_ffn_pre[l][None, :], w_up[l].astype(BF16),
                      conv_w[l], conv_b[l][None, :], w_down[l].astype(BF16),
                      g_ffn_post[l][None, :], seq_len)
    return x2.reshape(batch, seq_len, D_MODEL)
```

```python
import functools

import jax
import jax.numpy as jnp
from jax import lax
from jax.experimental import pallas as pl
from jax.experimental.pallas import tpu as pltpu

F32 = jnp.float32
BF16 = jnp.bfloat16

D_MODEL = 1024
HEAD_DIM = 64
H_HGRN, H_FOX, H_RET = 6, 6, 4
D_HGRN, D_FOX, D_RET = H_HGRN * HEAD_DIM, H_FOX * HEAD_DIM, H_RET * HEAD_DIM
D_FF = 2816
CONV_W = 3
ROPE_BASE = 10000.0
EPS = 1e-6
MIN_FORGET = 1e-12
MASK_VALUE = -1e30

LANES = 128
SUBLANES = 8
FGATE_PAD = LANES
BF16_SUBLANES = 16
VT_ROWS = HEAD_DIM + BF16_SUBLANES
LOG2E = 1.4426950408889634

_A_MAIN = (0, 3 * D_HGRN)
_A_GATE = (_A_MAIN[1], _A_MAIN[1] + D_HGRN)
_B_Q = (_A_GATE[1], _A_GATE[1] + D_FOX)
_B_K = (_B_Q[1], _B_Q[1] + D_FOX)
_B_V = (_B_K[1], _B_K[1] + D_FOX)
_B_G = (_B_V[1], _B_V[1] + D_FOX)
_B_F = (_B_G[1], _B_G[1] + FGATE_PAD)
_C_Q = (_B_F[1], _B_F[1] + D_RET)
_C_K = (_C_Q[1], _C_Q[1] + D_RET)
_C_VG = (_C_K[1], _C_K[1] + 2 * D_RET)
IN_COLS_PADDED = _C_VG[1]

TM_PROJ = 512
TM_FFN = 1024
FC_FFN = 256
TT_HGRN = 256
NB_HGRN = 4
SUB_HGRN = 32
SAFE_DIAG = 4
FAST_PATH_MAX_DECAY = 60.0
C_RET = 256
NB_RET = 4
TQ_FOX = 512
VMEM_LIMIT = 56 * 1024 * 1024


def _dot(a, b):
    return jnp.dot(a, b, preferred_element_type=F32)


def _dot_nt(a, b):
    return lax.dot_general(a, b, (((1,), (1,)), ((), ())), preferred_element_type=F32)


def _dot_tn(a, b):
    return lax.dot_general(a, b, (((0,), (0,)), ((), ())), preferred_element_type=F32)


def _dot_01_lhs(m01, x):
    hi = x.astype(BF16)
    r = x - hi.astype(F32)
    mid = r.astype(BF16)
    lo = (r - mid.astype(F32)).astype(BF16)
    return _dot(m01, hi) + _dot(m01, mid) + _dot(m01, lo)


def _head_mean_sq(y, ones_bd):
    sq = y * y
    hi = sq.astype(BF16)
    lo = (sq - hi.astype(F32)).astype(BF16)
    return (_dot(hi, ones_bd) + _dot(lo, ones_bd)) * (1.0 / HEAD_DIM)


def _rmsnorm_rows(x, gain):
    return x * lax.rsqrt(jnp.mean(x * x, axis=-1, keepdims=True) + EPS) * gain


def _sigmoid(x):
    return 1.0 / (1.0 + jnp.exp(-x))


def _first_head_lanes(shape):
    return (lax.broadcasted_iota(jnp.int32, shape, len(shape) - 1) & HEAD_DIM) == 0


def _bf16_terms(x):
    hi = x.astype(BF16).astype(F32)
    mid = (x - hi).astype(BF16).astype(F32)
    lo = (x - hi - mid).astype(BF16).astype(F32)
    return hi, mid, lo


def _inproj_kernel(x_ref, gpre_ref, w_ref, bf_ref, qg_ref, kg_ref, cos_ref, sin_ref, tri_ref,
                   ones_ref, pa_ref, za_ref, qt_ref, ka_ref, vt_ref, gb_ref, pc_ref, carry_ref,
                   *, tiles_per_seq):
    i = pl.program_id(0)
    h = _rmsnorm_rows(x_ref[...], gpre_ref[...]).astype(BF16)
    tm = h.shape[0]

    def proj(cols):
        return _dot(h, w_ref[:, cols[0]:cols[1]])

    pa_ref[...] = proj(_A_MAIN).astype(BF16)
    za_ref[...] = proj(_A_GATE)

    zf = proj(_B_F) + bf_ref[...]
    log_f = -(jnp.maximum(-zf, 0.0) + jnp.log1p(jnp.exp(-jnp.abs(zf))))

    @pl.when(i % tiles_per_seq == 0)
    def _():
        carry_ref[...] = jnp.zeros_like(carry_ref)

    cum = _dot_01_lhs(tri_ref[...], log_f) + carry_ref[...]
    carry_ref[...] = cum[tm - 1:, :]

    ones = ones_ref[...]
    bq = proj(_B_Q)
    bk = proj(_B_K)
    lane = lax.broadcasted_iota(jnp.int32, (tm, LANES), 1)
    is_feature = lane < HEAD_DIM
    is_one_q = (lane >= HEAD_DIM + 3) & (lane < HEAD_DIM + 6)
    is_one_k = (lane >= HEAD_DIM) & (lane < HEAD_DIM + 3)
    for pair in range(H_FOX // 2):
        sl = slice(pair * LANES, (pair + 1) * LANES)
        q2 = bq[:, sl]
        q2 = (q2 * lax.rsqrt(_head_mean_sq(q2, ones) + EPS) * qg_ref[:, sl]
              * (HEAD_DIM ** -0.5 * LOG2E))
        k2 = bk[:, sl]
        k2 = k2 * lax.rsqrt(_head_mean_sq(k2, ones) + EPS) * kg_ref[:, sl]
        for odd in range(2):
            head = 2 * pair + odd
            slot = slice(head * LANES, (head + 1) * LANES)
            qh = pltpu.roll(q2, HEAD_DIM, 1) if odd else q2
            kh = pltpu.roll(k2, HEAD_DIM, 1) if odd else k2
            hi, mid, lo = _bf16_terms(
                jnp.broadcast_to(cum[:, head:head + 1], (tm, LANES)) * LOG2E)
            q_bias = jnp.where(lane == HEAD_DIM, hi, jnp.where(lane == HEAD_DIM + 1, mid, jnp.where(
                lane == HEAD_DIM + 2, lo, jnp.where(is_one_q, 1.0, 0.0))))
            k_bias = jnp.where(lane == HEAD_DIM + 3, -hi, jnp.where(lane == HEAD_DIM + 4, -mid, jnp.where(
                lane == HEAD_DIM + 5, -lo, jnp.where(is_one_k, 1.0, 0.0))))
            qt_ref[0, slot, :] = jnp.where(is_feature, qh, q_bias).T.astype(BF16)
            ka_ref[:, slot] = jnp.where(is_feature, kh, k_bias).astype(BF16)
    v_t = proj(_B_V).T.astype(BF16)
    ones_row = jnp.where(lax.broadcasted_iota(jnp.int32, (BF16_SUBLANES, tm), 0) == 0,
                         1.0, 0.0).astype(BF16)
    for head in range(H_FOX):
        vt_ref[0, head * VT_ROWS:head * VT_ROWS + HEAD_DIM, :] = (
            v_t[head * HEAD_DIM:(head + 1) * HEAD_DIM, :])
        vt_ref[0, head * VT_ROWS + HEAD_DIM:(head + 1) * VT_ROWS, :] = ones_row
    gb_ref[...] = proj(_B_G).astype(BF16)

    cos = cos_ref[...]
    sin = sin_ref[...]
    low_half = (lax.broadcasted_iota(jnp.int32, cos.shape, 1) & (HEAD_DIM // 2)) == 0

    def rotary(y):
        swapped = jnp.where(low_half, pltpu.roll(y, D_RET - HEAD_DIM // 2, 1),
                            pltpu.roll(y, HEAD_DIM // 2, 1))
        return y * cos + swapped * sin

    pc_ref[:, 0:D_RET] = rotary(proj(_C_Q)).astype(BF16)
    pc_ref[:, D_RET:2 * D_RET] = (rotary(proj(_C_K)) * (HEAD_DIM ** -0.5)).astype(BF16)
    pc_ref[:, 2 * D_RET:4 * D_RET] = proj(_C_VG).astype(BF16)


def _inproj(x2, gpre, w, bf, qg, kg, cos, sin, tri, ones, seq_len):
    n = x2.shape[0]
    batch = n // seq_len
    tm = TM_PROJ
    tiles_per_seq = seq_len // tm
    const = lambda i: (0, 0)
    row = lambda i: (i, 0)
    seq_t = lambda i: (i // tiles_per_seq, 0, i % tiles_per_seq)
    return pl.pallas_call(
        functools.partial(_inproj_kernel, tiles_per_seq=tiles_per_seq),
        grid=(n // tm,),
        in_specs=[
            pl.BlockSpec((tm, D_MODEL), row),
            pl.BlockSpec((1, D_MODEL), const),
            pl.BlockSpec((D_MODEL, IN_COLS_PADDED), const),
            pl.BlockSpec((1, FGATE_PAD), const),
            pl.BlockSpec((1, D_FOX), const),
            pl.BlockSpec((1, D_FOX), const),
            pl.BlockSpec((tm, D_RET), lambda i: (i % tiles_per_seq, 0)),
            pl.BlockSpec((tm, D_RET), lambda i: (i % tiles_per_seq, 0)),
            pl.BlockSpec((tm, tm), const),
            pl.BlockSpec((LANES, LANES), const),
        ],
        out_specs=[
            pl.BlockSpec((tm, 3 * D_HGRN), row),
            pl.BlockSpec((tm, D_HGRN), row),
            pl.BlockSpec((1, H_FOX * LANES, tm), seq_t),
            pl.BlockSpec((tm, H_FOX * LANES), row),
            pl.BlockSpec((1, H_FOX * VT_ROWS, tm), seq_t),
            pl.BlockSpec((tm, D_FOX), row),
            pl.BlockSpec((tm, 4 * D_RET), row),
        ],
        out_shape=[
            jax.ShapeDtypeStruct((n, 3 * D_HGRN), BF16),
            jax.ShapeDtypeStruct((n, D_HGRN), F32),
            jax.ShapeDtypeStruct((batch, H_FOX * LANES, seq_len), BF16),
            jax.ShapeDtypeStruct((n, H_FOX * LANES), BF16),
            jax.ShapeDtypeStruct((batch, H_FOX * VT_ROWS, seq_len), BF16),
            jax.ShapeDtypeStruct((n, D_FOX), BF16),
            jax.ShapeDtypeStruct((n, 4 * D_RET), BF16),
        ],
        scratch_shapes=[pltpu.VMEM((1, FGATE_PAD), F32)],
        compiler_params=pltpu.CompilerParams(
            dimension_semantics=("arbitrary",), vmem_limit_bytes=VMEM_LIMIT),
        name="inproj",
    )(x2, gpre, w, bf, qg, kg, cos, sin, tri, ones)


_M_CUM, _M_AFTER, _M_DIAG, _M_MERGE0 = 0, 1, 2, 3
_MERGE_BLOCKS = (2 * SAFE_DIAG, 4 * SAFE_DIAG, 8 * SAFE_DIAG)
assert _MERGE_BLOCKS[-1] == SUB_HGRN


def _hgrn_matrices(tt):
    t = jnp.arange(tt)[:, None]
    u = jnp.arange(tt)[None, :]
    same = lambda m: (t // m) == (u // m)
    mats = [same(SUB_HGRN) & (u <= t), same(SUB_HGRN) & (u > t),
            same(SAFE_DIAG) & (u <= t) & (u > (t // SAFE_DIAG) * SAFE_DIAG)]
    for m in _MERGE_BLOCKS:
        mid = (t // m) * m + m // 2 - 1
        mats.append(same(m) & (((t > mid) & (u > mid) & (u <= t)) | ((t <= mid) & (u > t) & (u <= mid))))
    return jnp.stack(mats).astype(BF16)


def _hgrn_kernel(q_ref, v_ref, g_ref, z_ref, lb_ref, gain_ref, mats_ref, ones_ref, o_ref, st_ref,
                 logf_scr, k_scr, cum_scr, *, layer):
    @pl.when(pl.program_id(1) == 0)
    def _():
        st_ref[...] = jnp.zeros_like(st_ref)

    nb, tt = q_ref.shape[0], q_ref.shape[1]
    sub = SUB_HGRN

    lb = lb_ref[...]
    e = jnp.exp(lb - jnp.max(lb, axis=0, keepdims=True))
    p = e / jnp.sum(e, axis=0, keepdims=True)
    run = p[0:1]
    for i in range(1, layer + 1):
        run = run + p[i:i + 1]
    lower = run - p[0:1]

    lowest = None
    for b in range(nb):
        sg = _sigmoid(z_ref[b])
        f = lower + (1.0 - lower) * sg
        log_f = jnp.log(jnp.maximum(f, MIN_FORGET))
        cum = _dot_01_lhs(mats_ref[_M_CUM], log_f)
        logf_scr[b] = log_f
        k_scr[b] = (1.0 - lower) * (1.0 - sg)
        cum_scr[b] = cum
        low_b = jnp.min(cum)
        lowest = low_b if lowest is None else jnp.minimum(lowest, low_b)
    exact_fallback = lowest < -FAST_PATH_MAX_DECAY

    row = lax.broadcasted_iota(jnp.int32, (tt, tt), 0)
    col = lax.broadcasted_iota(jnp.int32, (tt, tt), 1)

    def diag_mask(m):
        in_block = row & (m - 1)
        dist = row - col
        return ((in_block - dist) | dist) >= 0

    def merge_mask(m):
        half = m // 2
        return (((row ^ col) & ~(m - 1)) | ((row & half) ^ half) | (col & half)) == 0

    first = _first_head_lanes((tt, LANES))
    rr = lax.broadcasted_iota(jnp.int32, (LANES, LANES), 0)
    cc = lax.broadcasted_iota(jnp.int32, (LANES, LANES), 1)
    same_head = ((rr ^ cc) & HEAD_DIM) == 0
    ones = ones_ref[...]

    def pair_tile(b, pair, exact):
        sl = slice(pair * LANES, (pair + 1) * LANES)
        q = q_ref[b, :, sl].astype(F32)
        k = k_scr[b, :, sl]
        cum = cum_scr[b, :, sl]
        v = v_ref[b, :, sl]
        e_q = jnp.exp(cum)
        qt = (q * e_q).astype(BF16)
        if exact:
            log_f = logf_scr[b, :, sl]
            since = _dot_01_lhs(mats_ref[_M_DIAG], log_f)
            pieces = [((q * jnp.exp(since)).astype(BF16), (k * jnp.exp(-since)).astype(BF16),
                       diag_mask(SAFE_DIAG))]
            for idx, m in enumerate(_MERGE_BLOCKS):
                w = jnp.exp(_dot_01_lhs(mats_ref[_M_MERGE0 + idx], log_f))
                pieces.append(((q * w).astype(BF16), (k * w).astype(BF16), merge_mask(m)))
            k_state = (k * jnp.exp(_dot_01_lhs(mats_ref[_M_AFTER], log_f))).astype(BF16)
        else:
            k_state = (k * jnp.exp(-cum)).astype(BF16)
            pieces = [(qt, k_state, diag_mask(sub))]

        intra = []
        for keep in (first, ~first):
            s = jnp.zeros((tt, tt), F32)
            for q_f, k_f, mask in pieces:
                s = jnp.where(mask, _dot_nt(jnp.where(keep, q_f, jnp.zeros_like(q_f)), k_f), s)
            intra.append(_dot(s.astype(BF16), v))
        o = jnp.where(first, intra[0], intra[1])

        st = st_ref[b, pair]
        outs = []
        for j in range(tt // sub):
            r = slice(j * sub, (j + 1) * sub)
            outs.append(o[r] + _dot_nt(qt[r], st.astype(BF16)))
            upd = jnp.where(same_head, _dot_tn(v[r], k_state[r]), 0.0)
            decay = e_q[(j + 1) * sub - 1:(j + 1) * sub, :]
            st = decay * st + upd if exact else decay * (st + upd)
        st_ref[b, pair] = st
        o = jnp.concatenate(outs, axis=0)

        o = o * lax.rsqrt(_head_mean_sq(o, ones) + EPS) * gain_ref[:, sl]
        g = g_ref[b, :, sl].astype(F32)
        o_ref[b, :, sl] = (o * (g * _sigmoid(g))).astype(BF16)

    def all_tiles(exact):
        for b in range(nb):
            for pair in range(H_HGRN // 2):
                pair_tile(b, pair, exact)

    pl.when(exact_fallback)(functools.partial(all_tiles, True))
    pl.when(jnp.logical_not(exact_fallback))(functools.partial(all_tiles, False))


def _hgrn(pa, za, lb, gain, mats, ones, batch, seq_len, layer):
    n = pa.shape[0]
    tt = TT_HGRN
    nb = NB_HGRN
    const = lambda b, t: (0, 0)
    tile_f32 = pltpu.VMEM((nb, tt, D_HGRN), F32)
    pa3 = pa.reshape(batch, seq_len, 3 * D_HGRN)
    blk = lambda j: pl.BlockSpec((nb, tt, D_HGRN), lambda b, t: (b, t, j))
    out = pl.pallas_call(
        functools.partial(_hgrn_kernel, layer=layer),
        grid=(batch // nb, seq_len // tt),
        in_specs=[
            blk(0), blk(1), blk(2), blk(0),
            pl.BlockSpec(lb.shape, const),
            pl.BlockSpec((1, D_HGRN), const),
            pl.BlockSpec(mats.shape, lambda b, t: (0, 0, 0)),
            pl.BlockSpec((LANES, LANES), const),
        ],
        out_specs=blk(0),
        out_shape=jax.ShapeDtypeStruct((batch, seq_len, D_HGRN), BF16),
        scratch_shapes=[pltpu.VMEM((nb, H_HGRN // 2, LANES, LANES), F32),
                        tile_f32, tile_f32, tile_f32],
        compiler_params=pltpu.CompilerParams(
            dimension_semantics=("arbitrary", "arbitrary"), vmem_limit_bytes=VMEM_LIMIT),
        name="hgrn",
    )(pa3, pa3, pa3, za.reshape(batch, seq_len, D_HGRN), lb, gain, mats, ones)
    return out.reshape(n, D_HGRN)


def _ret_kernel(q_ref, k_ref, v_ref, g_ref, dec_ref, qs_ref, ks_ref, cd_ref, gain_ref, ones_ref,
                o_ref, st_ref):
    @pl.when(pl.program_id(1) == 0)
    def _():
        st_ref[...] = jnp.zeros_like(st_ref)

    nb, c = q_ref.shape[0], q_ref.shape[1]
    first = _first_head_lanes((c, LANES))
    rr = lax.broadcasted_iota(jnp.int32, (LANES, LANES), 0)
    cc = lax.broadcasted_iota(jnp.int32, (LANES, LANES), 1)
    same_head = ((rr ^ cc) & HEAD_DIM) == 0
    ones = ones_ref[...]

    for b, pair in [(b, pair) for b in range(nb) for pair in range(H_RET // 2)]:
        sl = slice(pair * LANES, (pair + 1) * LANES)
        q = q_ref[b, :, sl]
        k = k_ref[b, :, sl]
        v = v_ref[b, :, sl]
        intra = []
        for hh, keep in enumerate((first, ~first)):
            s = _dot_nt(jnp.where(keep, q, jnp.zeros_like(q)), k) * dec_ref[2 * pair + hh]
            intra.append(_dot(s.astype(BF16), v))
        st = st_ref[b, pair]
        q_dec = (q.astype(F32) * qs_ref[:, sl]).astype(BF16)
        o = jnp.where(first, intra[0], intra[1]) + _dot_nt(q_dec, st.astype(BF16))
        k_dec = (k.astype(F32) * ks_ref[:, sl]).astype(BF16)
        st_ref[b, pair] = cd_ref[:, sl] * st + jnp.where(same_head, _dot_tn(v, k_dec), 0.0)

        o = o * lax.rsqrt(_head_mean_sq(o, ones) + EPS) * gain_ref[:, sl]
        g = g_ref[b, :, sl].astype(F32)
        o_ref[b, :, sl] = (o * (g * _sigmoid(g))).astype(BF16)


def _retention(pc, dec, qs, ks, cd, gain, ones, batch, seq_len):
    n = pc.shape[0]
    c = C_RET
    nb = NB_RET
    const2 = lambda b, t: (0, 0)
    pc3 = pc.reshape(batch, seq_len, 4 * D_RET)
    blk = lambda j: pl.BlockSpec((nb, c, D_RET), lambda b, t: (b, t, j))
    out = pl.pallas_call(
        _ret_kernel,
        grid=(batch // nb, seq_len // c),
        in_specs=[
            blk(0), blk(1), blk(2), blk(3),
            pl.BlockSpec((H_RET, c, c), lambda b, t: (0, 0, 0)),
            pl.BlockSpec((c, D_RET), const2),
            pl.BlockSpec((c, D_RET), const2),
            pl.BlockSpec((1, D_RET), const2),
            pl.BlockSpec((1, D_RET), const2),
            pl.BlockSpec((LANES, LANES), const2),
        ],
        out_specs=blk(0),
        out_shape=jax.ShapeDtypeStruct((batch, seq_len, D_RET), BF16),
        scratch_shapes=[pltpu.VMEM((nb, H_RET // 2, LANES, LANES), F32)],
        compiler_params=pltpu.CompilerParams(
            dimension_semantics=("arbitrary", "arbitrary"), vmem_limit_bytes=VMEM_LIMIT),
        name="retention",
    )(pc3, pc3, pc3, pc3, dec, qs, ks, cd, gain, ones)
    return out.reshape(n, D_RET)


def _fox_kernel(qt_ref, k_ref, vt_ref, g_ref, o_ref, s_scr, p_scr, acc_scr):
    tq = TQ_FOX
    nq = o_ref.shape[0] // tq
    blocks = [(qi, kj) for qi in range(nq) for kj in range(qi + 1)]
    key = lax.broadcasted_iota(jnp.int32, (tq, tq), 0)
    qry = lax.broadcasted_iota(jnp.int32, (tq, tq), 1)
    causal = qry >= key

    def scores(t):
        qi, kj = blocks[t]
        for hh in range(2):
            slot = slice(hh * LANES, (hh + 1) * LANES)
            s_scr[t % 2, hh] = _dot(k_ref[kj * tq:(kj + 1) * tq, slot],
                                    qt_ref[0, slot, qi * tq:(qi + 1) * tq])

    def softmax(t, m_old):
        qi, kj = blocks[t]
        m_new, alpha = [], []
        for hh in range(2):
            s = s_scr[t % 2, hh]
            if kj == qi:
                s = jnp.where(causal, s, MASK_VALUE)
            m_blk = jnp.max(s, axis=0, keepdims=True)
            if kj > 0:
                m_blk = jnp.maximum(m_old[hh], m_blk)
                alpha.append(jnp.exp2(m_old[hh] - m_blk))
            p_scr[t % 2, hh] = jnp.exp2(s - m_blk).astype(BF16)
            m_new.append(m_blk)
        return m_new, alpha

    def accumulate(t, alpha):
        qi, kj = blocks[t]
        for hh in range(2):
            pv = _dot(vt_ref[0, hh * VT_ROWS:(hh + 1) * VT_ROWS, kj * tq:(kj + 1) * tq],
                      p_scr[t % 2, hh])
            acc_scr[qi % 2, hh] = pv if kj == 0 else alpha[hh] * acc_scr[qi % 2, hh] + pv
        if kj == qi:
            heads = []
            for hh in range(2):
                total = acc_scr[qi % 2, hh]
                heads.append(total[:HEAD_DIM] / total[HEAD_DIM:HEAD_DIM + 1])
            o_t = jnp.concatenate(heads, axis=0)
            rows = slice(qi * tq, (qi + 1) * tq)
            o_ref[rows, :] = (o_t.T * _sigmoid(g_ref[rows, :].astype(F32))).astype(BF16)

    scores(0)
    m_run, alpha_prev = None, None
    for t in range(len(blocks)):
        m_run, alpha_t = softmax(t, m_run)
        if t + 1 < len(blocks):
            scores(t + 1)
        if t > 0:
            accumulate(t - 1, alpha_prev)
        alpha_prev = alpha_t
    accumulate(len(blocks) - 1, alpha_prev)


def _fox(qt, ka, vt, gb, batch, seq_len):
    n = ka.shape[0]
    tq = TQ_FOX
    pairs = H_FOX // 2
    return pl.pallas_call(
        _fox_kernel,
        grid=(batch, pairs),
        in_specs=[
            pl.BlockSpec((1, 2 * LANES, seq_len), lambda b, p: (b, p, 0)),
            pl.BlockSpec((seq_len, 2 * LANES), lambda b, p: (b, p)),
            pl.BlockSpec((1, 2 * VT_ROWS, seq_len), lambda b, p: (b, p, 0)),
            pl.BlockSpec((seq_len, LANES), lambda b, p: (b, p)),
        ],
        out_specs=pl.BlockSpec((seq_len, LANES), lambda b, p: (b, p)),
        out_shape=jax.ShapeDtypeStruct((n, D_FOX), BF16),
        scratch_shapes=[
            pltpu.VMEM((2, 2, tq, tq), F32),
            pltpu.VMEM((2, 2, tq, tq), BF16),
            pltpu.VMEM((2, 2, VT_ROWS, tq), F32),
        ],
        compiler_params=pltpu.CompilerParams(
            dimension_semantics=("arbitrary", "arbitrary"), vmem_limit_bytes=VMEM_LIMIT),
        name="fox",
    )(qt, ka, vt, gb)


def _mix_ffn_kernel(oa_ref, ob_ref, oc_ref, wa_ref, wb_ref, wc_ref, gmix_ref, x_ref, gpre_ref,
                    wup_ref, cw_ref, cb_ref, wd_ref, gpost_ref, out_ref, xp_scr, act_scr, tail_scr,
                    *, tiles_per_seq):
    i = pl.program_id(0)
    tm = x_ref.shape[0]
    fc = FC_FFN

    mixed = (_dot(oa_ref[...], wa_ref[...]) + _dot(ob_ref[...], wb_ref[...])
             + _dot(oc_ref[...], wc_ref[...]))
    x_mid = x_ref[...] + _rmsnorm_rows(mixed, gmix_ref[...])
    xp_scr[...] = pltpu.einshape("(ab)f->(ba)f", x_mid, a=SUBLANES)
    h = _rmsnorm_rows(xp_scr[...], gpre_ref[...]).astype(BF16)

    @pl.when(i == 0)
    def _():
        tail_scr[...] = jnp.zeros_like(tail_scr)

    seq_start = i % tiles_per_seq == 0
    first_sublane = lax.broadcasted_iota(jnp.int32, (SUBLANES, fc), 0) == 0

    def causal_conv(u, col, slot):
        prev = jnp.where(seq_start, 0.0, tail_scr[slot])
        last = u[tm - 2 * SUBLANES:, :]
        tail_scr[slot] = last
        head = []
        for g in range(2):
            rows = slice(g * SUBLANES, (g + 1) * SUBLANES)
            entering = prev[(g + 1) * SUBLANES - 1:(g + 1) * SUBLANES, :]
            head.append(jnp.where(first_sublane, entering, pltpu.roll(last[rows], 1, 0)))
        u1 = jnp.concatenate([head[1], u[:tm - SUBLANES]], axis=0)
        u2 = jnp.concatenate([head[0], head[1], u[:tm - 2 * SUBLANES]], axis=0)
        cols = slice(col, col + fc)
        return (cb_ref[:, cols] + cw_ref[0:1, cols] * u2 + cw_ref[1:2, cols] * u1
                + cw_ref[2:3, cols] * u)

    for c in range(D_FF // fc):
        gate = causal_conv(_dot(h, wup_ref[:, c * fc:(c + 1) * fc]), c * fc, 2 * c)
        val = causal_conv(_dot(h, wup_ref[:, D_FF + c * fc:D_FF + (c + 1) * fc]),
                          D_FF + c * fc, 2 * c + 1)
        act_scr[:, c * fc:(c + 1) * fc] = (gate * _sigmoid(gate) * val).astype(BF16)

    y = _dot(act_scr[...], wd_ref[...])
    res = xp_scr[...] + _rmsnorm_rows(y, gpost_ref[...])
    out_ref[...] = pltpu.einshape("(ba)f->(ab)f", res, a=SUBLANES)


def _mix_ffn(oa, ob, oc, wa, wb, wc, gmix, x2, gpre, w_up, conv_w, conv_b, w_down, gpost, seq_len):
    n = x2.shape[0]
    tm = TM_FFN
    nf = D_FF // FC_FFN
    tiles_per_seq = seq_len // tm
    row = lambda i: (i, 0)
    resident = lambda shape: pl.BlockSpec(shape, lambda i: (0, 0), pipeline_mode=pl.Buffered(1))
    return pl.pallas_call(
        functools.partial(_mix_ffn_kernel, tiles_per_seq=tiles_per_seq),
        grid=(n // tm,),
        in_specs=[
            pl.BlockSpec((tm, D_HGRN), row),
            pl.BlockSpec((tm, D_FOX), row),
            pl.BlockSpec((tm, D_RET), row),
            resident((D_HGRN, D_MODEL)),
            resident((D_FOX, D_MODEL)),
            resident((D_RET, D_MODEL)),
            resident((1, D_MODEL)),
            pl.BlockSpec((tm, D_MODEL), row),
            resident((1, D_MODEL)),
            resident((D_MODEL, 2 * D_FF)),
            resident((CONV_W, 2 * D_FF)),
            resident((1, 2 * D_FF)),
            resident((D_FF, D_MODEL)),
            resident((1, D_MODEL)),
        ],
        out_specs=pl.BlockSpec((tm, D_MODEL), row),
        out_shape=jax.ShapeDtypeStruct((n, D_MODEL), F32),
        scratch_shapes=[
            pltpu.VMEM((tm, D_MODEL), F32),
            pltpu.VMEM((tm, D_FF), BF16),
            pltpu.VMEM((2 * nf, 2 * SUBLANES, FC_FFN), F32),
        ],
        compiler_params=pltpu.CompilerParams(
            dimension_semantics=("arbitrary",), vmem_limit_bytes=VMEM_LIMIT),
        name="mix_ffn",
    )(oa, ob, oc, wa, wb, wc, gmix, x2, gpre, w_up, conv_w, conv_b, w_down, gpost)


def _block_ones(n):
    idx = jnp.arange(n) // HEAD_DIM
    return (idx[:, None] == idx[None, :]).astype(BF16)


def _lower_tri(n, block):
    r = jnp.arange(n)
    return ((r[:, None] >= r[None, :]) & (r[:, None] // block == r[None, :] // block)).astype(BF16)


def _rotary_tables(seq_len):
    inv_freq = 1.0 / (ROPE_BASE ** (jnp.arange(0, HEAD_DIM, 2, dtype=F32) / HEAD_DIM))
    ang = jnp.arange(seq_len, dtype=F32)[:, None] * inv_freq[None, :]
    cos, sin = jnp.cos(ang), jnp.sin(ang)
    cos_full = jnp.tile(jnp.concatenate([cos, cos], axis=-1), (1, H_RET))
    sin_signed = jnp.tile(jnp.concatenate([-sin, sin], axis=-1), (1, H_RET))
    return cos_full, sin_signed


def _retention_tables(c):
    log_gamma = jnp.log1p(-jnp.exp2(-5.0 - jnp.arange(H_RET, dtype=F32)))
    pos = jnp.arange(c, dtype=F32)
    rel = pos[:, None] - pos[None, :]
    dec = jnp.where(rel >= 0, jnp.exp(log_gamma[:, None, None] * jnp.maximum(rel, 0.0)), 0.0)
    per_lane = lambda a: jnp.repeat(a, HEAD_DIM, axis=-1)
    qs = per_lane(jnp.exp(log_gamma[None, :] * (pos[:, None] + 1.0)))
    ks = per_lane(jnp.exp(log_gamma[None, :] * (c - 1.0 - pos[:, None])))
    cd = per_lane(jnp.exp(log_gamma * c)[None, :])
    return dec, qs, ks, cd


def _reorder_in_weight_rows(w_t):
    widths = [D_HGRN] * 4 + [D_FOX] * 3 + [H_FOX] + [D_FOX] + [D_RET] * 4
    offs = [0]
    for wd in widths:
        offs.append(offs[-1] + wd)
    part = lambda j: w_t[offs[j]:offs[j + 1]]
    a_q, a_f, a_i, a_g, b_q, b_k, b_v, b_f, b_g, c_q, c_k, c_v, c_g = (part(j) for j in range(13))
    b_f = jnp.pad(b_f, ((0, FGATE_PAD - H_FOX), (0, 0)))
    return jnp.concatenate(
        [a_q, a_i, a_g, a_f, b_q, b_k, b_v, b_g, b_f, c_q, c_k, c_v, c_g], axis=0)


def _transpose_cast_kernel(w_ref, o_ref):
    o_ref[...] = w_ref[...].T.astype(BF16)


def _transpose_cast(w_t):
    rows = w_t.shape[0]
    blk = D_HGRN
    return pl.pallas_call(
        _transpose_cast_kernel,
        grid=(rows // blk,),
        in_specs=[pl.BlockSpec((blk, D_MODEL), lambda i: (i, 0))],
        out_specs=pl.BlockSpec((D_MODEL, blk), lambda i: (0, i)),
        out_shape=jax.ShapeDtypeStruct((D_MODEL, rows), BF16),
        compiler_params=pltpu.CompilerParams(
            dimension_semantics=("arbitrary",), vmem_limit_bytes=VMEM_LIMIT),
        name="wprep",
    )(w_t)


def kernel(x, w_in, b_fox_f, fox_q_gain, fox_k_gain, hgrn_lb, hgrn_out_gain, ret_out_gain, w_out,
           g_mix_pre, g_mix_post, w_up, conv_w, conv_b, w_down, g_ffn_pre, g_ffn_post):
    batch, seq_len, d_model = x.shape
    assert d_model == D_MODEL
    assert seq_len % TM_FFN == 0 and seq_len % TQ_FOX == 0
    assert batch % NB_HGRN == 0 and batch % NB_RET == 0
    depth = w_in.shape[0]
    n = batch * seq_len

    cos_full, sin_signed = _rotary_tables(seq_len)
    dec, qs, ks, cd = _retention_tables(C_RET)
    tri_proj = _lower_tri(TM_PROJ, TM_PROJ)
    mats_hgrn = _hgrn_matrices(TT_HGRN)
    ones_pair = _block_ones(LANES)
    lb = hgrn_lb.astype(F32)

    w_in_t = jnp.transpose(w_in, (2, 0, 1))

    x2 = x.reshape(n, D_MODEL)
    for l in range(depth):
        w = _transpose_cast(_reorder_in_weight_rows(w_in_t[:, l, :]))
        bf = jnp.pad(b_fox_f[l].astype(F32), (0, FGATE_PAD - H_FOX))[None, :]
        qg = jnp.tile(fox_q_gain[l].astype(F32), H_FOX)[None, :]
        kg = jnp.tile(fox_k_gain[l].astype(F32), H_FOX)[None, :]
        pa, za, qt, ka, vt, gb, pc = _inproj(x2, g_mix_pre[l][None, :], w, bf, qg, kg, cos_full,
                                             sin_signed, tri_proj, ones_pair, seq_len)

        o_a = _hgrn(pa, za, lb, hgrn_out_gain[l].reshape(1, D_HGRN), mats_hgrn, ones_pair,
                    batch, seq_len, l)
        o_b = _fox(qt, ka, vt, gb, batch, seq_len)
        o_c = _retention(pc, dec, qs, ks, cd, ret_out_gain[l].reshape(1, D_RET), ones_pair,
                         batch, seq_len)

        wo = w_out[l].astype(BF16)
        x2 = _mix_ffn(o_a, o_b, o_c, wo[:D_HGRN], wo[D_HGRN:D_HGRN + D_FOX], wo[D_HGRN + D_FOX:],
                      g_mix_post[l][None, :], x2, g_ffn_pre[l][None, :], w_up[l].astype(BF16),
                      conv_w[l], conv_b[l][None, :], w_down[l].astype(BF16),
                      g_ffn_post[l][None, :], seq_len)
    return x2.reshape(batch, seq_len, D_MODEL)
```

```python
import functools

import jax
import jax.numpy as jnp
from jax import lax
from jax.experimental import pallas as pl
from jax.experimental.pallas import tpu as pltpu

F32 = jnp.float32
BF16 = jnp.bfloat16

D_MODEL = 1024
HEAD_DIM = 64
H_HGRN, H_FOX, H_RET = 6, 6, 4
D_HGRN, D_FOX, D_RET = H_HGRN * HEAD_DIM, H_FOX * HEAD_DIM, H_RET * HEAD_DIM
D_FF = 2816
CONV_W = 3
ROPE_BASE = 10000.0
EPS = 1e-6
MIN_FORGET = 1e-12
MASK_VALUE = -1e30

LANES = 128
SUBLANES = 8
FGATE_PAD = LANES
BF16_SUBLANES = 16
VT_ROWS = HEAD_DIM + BF16_SUBLANES
LOG2E = 1.4426950408889634

_A_MAIN = (0, 3 * D_HGRN)
_A_GATE = (_A_MAIN[1], _A_MAIN[1] + D_HGRN)
_B_Q = (_A_GATE[1], _A_GATE[1] + D_FOX)
_B_K = (_B_Q[1], _B_Q[1] + D_FOX)
_B_V = (_B_K[1], _B_K[1] + D_FOX)
_B_G = (_B_V[1], _B_V[1] + D_FOX)
_B_F = (_B_G[1], _B_G[1] + FGATE_PAD)
_C_Q = (_B_F[1], _B_F[1] + D_RET)
_C_K = (_C_Q[1], _C_Q[1] + D_RET)
_C_VG = (_C_K[1], _C_K[1] + 2 * D_RET)
IN_COLS_PADDED = _C_VG[1]

TM_PROJ = 512
TM_FFN = 1024
FC_FFN = 256
TT_HGRN = 256
NB_HGRN = 4
SUB_HGRN = 32
SAFE_DIAG = 4
FAST_PATH_MAX_DECAY = 60.0
C_RET = 256
NB_RET = 4
TQ_FOX = 512
VMEM_LIMIT = 56 * 1024 * 1024


def _dot(a, b):
    return jnp.dot(a, b, preferred_element_type=F32)


def _dot_nt(a, b):
    return lax.dot_general(a, b, (((1,), (1,)), ((), ())), preferred_element_type=F32)


def _dot_tn(a, b):
    return lax.dot_general(a, b, (((0,), (0,)), ((), ())), preferred_element_type=F32)


def _dot_01_lhs(m01, x):
    hi = x.astype(BF16)
    r = x - hi.astype(F32)
    mid = r.astype(BF16)
    lo = (r - mid.astype(F32)).astype(BF16)
    return _dot(m01, hi) + _dot(m01, mid) + _dot(m01, lo)


def _head_mean_sq(y, ones_bd):
    sq = y * y
    hi = sq.astype(BF16)
    lo = (sq - hi.astype(F32)).astype(BF16)
    return (_dot(hi, ones_bd) + _dot(lo, ones_bd)) * (1.0 / HEAD_DIM)


def _rmsnorm_rows(x, gain):
    return x * lax.rsqrt(jnp.mean(x * x, axis=-1, keepdims=True) + EPS) * gain


def _sigmoid(x):
    return 1.0 / (1.0 + jnp.exp(-x))


def _first_head_lanes(shape):
    return (lax.broadcasted_iota(jnp.int32, shape, len(shape) - 1) & HEAD_DIM) == 0


def _bf16_terms(x):
    hi = x.astype(BF16).astype(F32)
    mid = (x - hi).astype(BF16).astype(F32)
    lo = (x - hi - mid).astype(BF16).astype(F32)
    return hi, mid, lo


def _inproj_kernel(x_ref, gpre_ref, w_ref, bf_ref, qg_ref, kg_ref, cos_ref, sin_ref, tri_ref,
                   ones_ref, pa_ref, za_ref, qt_ref, ka_ref, vt_ref, gb_ref, pc_ref, carry_ref,
                   *, tiles_per_seq):
    i = pl.program_id(0)
    h = _rmsnorm_rows(x_ref[...], gpre_ref[...]).astype(BF16)
    tm = h.shape[0]

    def proj(cols):
        return _dot(h, w_ref[:, cols[0]:cols[1]])

    pa_ref[...] = proj(_A_MAIN).astype(BF16)
    za_ref[...] = proj(_A_GATE)

    zf = proj(_B_F) + bf_ref[...]
    log_f = -(jnp.maximum(-zf, 0.0) + jnp.log1p(jnp.exp(-jnp.abs(zf))))

    @pl.when(i % tiles_per_seq == 0)
    def _():
        carry_ref[...] = jnp.zeros_like(carry_ref)

    cum = _dot_01_lhs(tri_ref[...], log_f) + carry_ref[...]
    carry_ref[...] = cum[tm - 1:, :]

    ones = ones_ref[...]
    bq = proj(_B_Q)
    bk = proj(_B_K)
    lane = lax.broadcasted_iota(jnp.int32, (tm, LANES), 1)
    is_feature = lane < HEAD_DIM
    is_one_q = (lane >= HEAD_DIM + 3) & (lane < HEAD_DIM + 6)
    is_one_k = (lane >= HEAD_DIM) & (lane < HEAD_DIM + 3)
    for pair in range(H_FOX // 2):
        sl = slice(pair * LANES, (pair + 1) * LANES)
        q2 = bq[:, sl]
        q2 = (q2 * lax.rsqrt(_head_mean_sq(q2, ones) + EPS) * qg_ref[:, sl]
              * (HEAD_DIM ** -0.5 * LOG2E))
        k2 = bk[:, sl]
        k2 = k2 * lax.rsqrt(_head_mean_sq(k2, ones) + EPS) * kg_ref[:, sl]
        for odd in range(2):
            head = 2 * pair + odd
            slot = slice(head * LANES, (head + 1) * LANES)
            qh = pltpu.roll(q2, HEAD_DIM, 1) if odd else q2
            kh = pltpu.roll(k2, HEAD_DIM, 1) if odd else k2
            hi, mid, lo = _bf16_terms(
                jnp.broadcast_to(cum[:, head:head + 1], (tm, LANES)) * LOG2E)
            q_bias = jnp.where(lane == HEAD_DIM, hi, jnp.where(lane == HEAD_DIM + 1, mid, jnp.where(
                lane == HEAD_DIM + 2, lo, jnp.where(is_one_q, 1.0, 0.0))))
            k_bias = jnp.where(lane == HEAD_DIM + 3, -hi, jnp.where(lane == HEAD_DIM + 4, -mid, jnp.where(
                lane == HEAD_DIM + 5, -lo, jnp.where(is_one_k, 1.0, 0.0))))
            qt_ref[0, slot, :] = jnp.where(is_feature, qh, q_bias).T.astype(BF16)
            ka_ref[:, slot] = jnp.where(is_feature, kh, k_bias).astype(BF16)
    v_t = proj(_B_V).T.astype(BF16)
    ones_row = jnp.where(lax.broadcasted_iota(jnp.int32, (BF16_SUBLANES, tm), 0) == 0,
                         1.0, 0.0).astype(BF16)
    for head in range(H_FOX):
        vt_ref[0, head * VT_ROWS:head * VT_ROWS + HEAD_DIM, :] = (
            v_t[head * HEAD_DIM:(head + 1) * HEAD_DIM, :])
        vt_ref[0, head * VT_ROWS + HEAD_DIM:(head + 1) * VT_ROWS, :] = ones_row
    gb_ref[...] = proj(_B_G).astype(BF16)

    cos = cos_ref[...]
    sin = sin_ref[...]
    low_half = (lax.broadcasted_iota(jnp.int32, cos.shape, 1) & (HEAD_DIM // 2)) == 0

    def rotary(y):
        swapped = jnp.where(low_half, pltpu.roll(y, D_RET - HEAD_DIM // 2, 1),
                            pltpu.roll(y, HEAD_DIM // 2, 1))
        return y * cos + swapped * sin

    pc_ref[:, 0:D_RET] = rotary(proj(_C_Q)).astype(BF16)
    pc_ref[:, D_RET:2 * D_RET] = (rotary(proj(_C_K)) * (HEAD_DIM ** -0.5)).astype(BF16)
    pc_ref[:, 2 * D_RET:4 * D_RET] = proj(_C_VG).astype(BF16)


def _inproj(x2, gpre, w, bf, qg, kg, cos, sin, tri, ones, seq_len):
    n = x2.shape[0]
    batch = n // seq_len
    tm = TM_PROJ
    tiles_per_seq = seq_len // tm
    const = lambda i: (0, 0)
    row = lambda i: (i, 0)
    seq_t = lambda i: (i // tiles_per_seq, 0, i % tiles_per_seq)
    return pl.pallas_call(
        functools.partial(_inproj_kernel, tiles_per_seq=tiles_per_seq),
        grid=(n // tm,),
        in_specs=[
            pl.BlockSpec((tm, D_MODEL), row),
            pl.BlockSpec((1, D_MODEL), const),
            pl.BlockSpec((D_MODEL, IN_COLS_PADDED), const),
            pl.BlockSpec((1, FGATE_PAD), const),
            pl.BlockSpec((1, D_FOX), const),
            pl.BlockSpec((1, D_FOX), const),
            pl.BlockSpec((tm, D_RET), lambda i: (i % tiles_per_seq, 0)),
            pl.BlockSpec((tm, D_RET), lambda i: (i % tiles_per_seq, 0)),
            pl.BlockSpec((tm, tm), const),
            pl.BlockSpec((LANES, LANES), const),
        ],
        out_specs=[
            pl.BlockSpec((tm, 3 * D_HGRN), row),
            pl.BlockSpec((tm, D_HGRN), row),
            pl.BlockSpec((1, H_FOX * LANES, tm), seq_t),
            pl.BlockSpec((tm, H_FOX * LANES), row),
            pl.BlockSpec((1, H_FOX * VT_ROWS, tm), seq_t),
            pl.BlockSpec((tm, D_FOX), row),
            pl.BlockSpec((tm, 4 * D_RET), row),
        ],
        out_shape=[
            jax.ShapeDtypeStruct((n, 3 * D_HGRN), BF16),
            jax.ShapeDtypeStruct((n, D_HGRN), F32),
            jax.ShapeDtypeStruct((batch, H_FOX * LANES, seq_len), BF16),
            jax.ShapeDtypeStruct((n, H_FOX * LANES), BF16),
            jax.ShapeDtypeStruct((batch, H_FOX * VT_ROWS, seq_len), BF16),
            jax.ShapeDtypeStruct((n, D_FOX), BF16),
            jax.ShapeDtypeStruct((n, 4 * D_RET), BF16),
        ],
        scratch_shapes=[pltpu.VMEM((1, FGATE_PAD), F32)],
        compiler_params=pltpu.CompilerParams(
            dimension_semantics=("arbitrary",), vmem_limit_bytes=VMEM_LIMIT),
        name="inproj",
    )(x2, gpre, w, bf, qg, kg, cos, sin, tri, ones)


_M_CUM, _M_AFTER, _M_DIAG, _M_MERGE0 = 0, 1, 2, 3
_MERGE_BLOCKS = (2 * SAFE_DIAG, 4 * SAFE_DIAG, 8 * SAFE_DIAG)
assert _MERGE_BLOCKS[-1] == SUB_HGRN


def _hgrn_matrices(tt):
    t = jnp.arange(tt)[:, None]
    u = jnp.arange(tt)[None, :]
    same = lambda m: (t // m) == (u // m)
    mats = [same(SUB_HGRN) & (u <= t), same(SUB_HGRN) & (u > t),
            same(SAFE_DIAG) & (u <= t) & (u > (t // SAFE_DIAG) * SAFE_DIAG)]
    for m in _MERGE_BLOCKS:
        mid = (t // m) * m + m // 2 - 1
        mats.append(same(m) & (((t > mid) & (u > mid) & (u <= t)) | ((t <= mid) & (u > t) & (u <= mid))))
    return jnp.stack(mats).astype(BF16)


def _hgrn_kernel(q_ref, v_ref, g_ref, z_ref, lb_ref, gain_ref, mats_ref, ones_ref, o_ref, st_ref,
                 logf_scr, k_scr, cum_scr, *, layer):
    @pl.when(pl.program_id(1) == 0)
    def _():
        st_ref[...] = jnp.zeros_like(st_ref)

    nb, tt = q_ref.shape[0], q_ref.shape[1]
    sub = SUB_HGRN

    lb = lb_ref[...]
    e = jnp.exp(lb - jnp.max(lb, axis=0, keepdims=True))
    p = e / jnp.sum(e, axis=0, keepdims=True)
    run = p[0:1]
    for i in range(1, layer + 1):
        run = run + p[i:i + 1]
    lower = run - p[0:1]

    lowest = None
    for b in range(nb):
        sg = _sigmoid(z_ref[b])
        f = lower + (1.0 - lower) * sg
        log_f = jnp.log(jnp.maximum(f, MIN_FORGET))
        cum = _dot_01_lhs(mats_ref[_M_CUM], log_f)
        logf_scr[b] = log_f
        k_scr[b] = (1.0 - lower) * (1.0 - sg)
        cum_scr[b] = cum
        low_b = jnp.min(cum)
        lowest = low_b if lowest is None else jnp.minimum(lowest, low_b)
    exact_fallback = lowest < -FAST_PATH_MAX_DECAY

    row = lax.broadcasted_iota(jnp.int32, (tt, tt), 0)
    col = lax.broadcasted_iota(jnp.int32, (tt, tt), 1)

    def diag_mask(m):
        in_block = row & (m - 1)
        dist = row - col
        return ((in_block - dist) | dist) >= 0

    def merge_mask(m):
        half = m // 2
        return (((row ^ col) & ~(m - 1)) | ((row & half) ^ half) | (col & half)) == 0

    first = _first_head_lanes((tt, LANES))
    rr = lax.broadcasted_iota(jnp.int32, (LANES, LANES), 0)
    cc = lax.broadcasted_iota(jnp.int32, (LANES, LANES), 1)
    same_head = ((rr ^ cc) & HEAD_DIM) == 0
    ones = ones_ref[...]

    def factors(c, exact):
        b, sl = c["b"], c["sl"]
        q = q_ref[b, :, sl].astype(F32)
        k = k_scr[b, :, sl]
        cum = cum_scr[b, :, sl]
        c["v"] = v_ref[b, :, sl]
        c["e_q"] = jnp.exp(cum)
        c["qt"] = (q * c["e_q"]).astype(BF16)
        if exact:
            log_f = logf_scr[b, :, sl]
            since = _dot_01_lhs(mats_ref[_M_DIAG], log_f)
            pieces = [((q * jnp.exp(since)).astype(BF16), (k * jnp.exp(-since)).astype(BF16),
                       diag_mask(SAFE_DIAG))]
            for idx, m in enumerate(_MERGE_BLOCKS):
                w = jnp.exp(_dot_01_lhs(mats_ref[_M_MERGE0 + idx], log_f))
                pieces.append(((q * w).astype(BF16), (k * w).astype(BF16), merge_mask(m)))
            c["k_state"] = (k * jnp.exp(_dot_01_lhs(mats_ref[_M_AFTER], log_f))).astype(BF16)
        else:
            c["k_state"] = (k * jnp.exp(-cum)).astype(BF16)
            pieces = [(c["qt"], c["k_state"], diag_mask(sub))]
        c["pieces"] = pieces

    def intra_scores(c):
        c["s"] = []
        for keep in (first, ~first):
            s = jnp.zeros((tt, tt), F32)
            for q_f, k_f, mask in c["pieces"]:
                s = jnp.where(mask, _dot_nt(jnp.where(keep, q_f, jnp.zeros_like(q_f)), k_f), s)
            c["s"].append(s.astype(BF16))

    def intra_values(c):
        c["o"] = jnp.where(first, _dot(c["s"][0], c["v"]), _dot(c["s"][1], c["v"]))

    def state_updates(c):
        c["upd"] = [jnp.where(same_head, _dot_tn(c["v"][j * sub:(j + 1) * sub],
                                                 c["k_state"][j * sub:(j + 1) * sub]), 0.0)
                    for j in range(tt // sub)]

    def state_chain(c, exact):
        st = st_ref[c["b"], c["pair"]]
        outs = []
        for j in range(tt // sub):
            r = slice(j * sub, (j + 1) * sub)
            outs.append(c["o"][r] + _dot_nt(c["qt"][r], st.astype(BF16)))
            decay = c["e_q"][(j + 1) * sub - 1:(j + 1) * sub, :]
            st = decay * st + c["upd"][j] if exact else decay * (st + c["upd"][j])
        st_ref[c["b"], c["pair"]] = st
        c["o"] = jnp.concatenate(outs, axis=0)

    def head_norm_and_gate(c):
        b, sl = c["b"], c["sl"]
        o = c["o"] * lax.rsqrt(_head_mean_sq(c["o"], ones) + EPS) * gain_ref[:, sl]
        g = g_ref[b, :, sl].astype(F32)
        o_ref[b, :, sl] = (o * (g * _sigmoid(g))).astype(BF16)

    def all_tiles(exact):
        tiles = [dict(b=b, pair=pair, sl=slice(pair * LANES, (pair + 1) * LANES))
                 for b in range(nb) for pair in range(H_HGRN // 2)]
        for stage in (functools.partial(factors, exact=exact), intra_scores, intra_values,
                      state_updates, functools.partial(state_chain, exact=exact),
                      head_norm_and_gate):
            for c in tiles:
                stage(c)

    pl.when(exact_fallback)(functools.partial(all_tiles, True))
    pl.when(jnp.logical_not(exact_fallback))(functools.partial(all_tiles, False))


def _hgrn(pa, za, lb, gain, mats, ones, batch, seq_len, layer):
    n = pa.shape[0]
    tt = TT_HGRN
    nb = NB_HGRN
    const = lambda b, t: (0, 0)
    tile_f32 = pltpu.VMEM((nb, tt, D_HGRN), F32)
    pa3 = pa.reshape(batch, seq_len, 3 * D_HGRN)
    blk = lambda j: pl.BlockSpec((nb, tt, D_HGRN), lambda b, t: (b, t, j))
    out = pl.pallas_call(
        functools.partial(_hgrn_kernel, layer=layer),
        grid=(batch // nb, seq_len // tt),
        in_specs=[
            blk(0), blk(1), blk(2), blk(0),
            pl.BlockSpec(lb.shape, const),
            pl.BlockSpec((1, D_HGRN), const),
            pl.BlockSpec(mats.shape, lambda b, t: (0, 0, 0)),
            pl.BlockSpec((LANES, LANES), const),
        ],
        out_specs=blk(0),
        out_shape=jax.ShapeDtypeStruct((batch, seq_len, D_HGRN), BF16),
        scratch_shapes=[pltpu.VMEM((nb, H_HGRN // 2, LANES, LANES), F32),
                        tile_f32, tile_f32, tile_f32],
        compiler_params=pltpu.CompilerParams(
            dimension_semantics=("arbitrary", "arbitrary"), vmem_limit_bytes=VMEM_LIMIT),
        name="hgrn",
    )(pa3, pa3, pa3, za.reshape(batch, seq_len, D_HGRN), lb, gain, mats, ones)
    return out.reshape(n, D_HGRN)


def _ret_kernel(q_ref, k_ref, v_ref, g_ref, dec_ref, qs_ref, ks_ref, cd_ref, gain_ref, ones_ref,
                o_ref, st_ref):
    @pl.when(pl.program_id(1) == 0)
    def _():
        st_ref[...] = jnp.zeros_like(st_ref)

    nb, c = q_ref.shape[0], q_ref.shape[1]
    first = _first_head_lanes((c, LANES))
    rr = lax.broadcasted_iota(jnp.int32, (LANES, LANES), 0)
    cc = lax.broadcasted_iota(jnp.int32, (LANES, LANES), 1)
    same_head = ((rr ^ cc) & HEAD_DIM) == 0
    ones = ones_ref[...]

    for b, pair in [(b, pair) for b in range(nb) for pair in range(H_RET // 2)]:
        sl = slice(pair * LANES, (pair + 1) * LANES)
        q = q_ref[b, :, sl]
        k = k_ref[b, :, sl]
        v = v_ref[b, :, sl]
        intra = []
        for hh, keep in enumerate((first, ~first)):
            s = _dot_nt(jnp.where(keep, q, jnp.zeros_like(q)), k) * dec_ref[2 * pair + hh]
            intra.append(_dot(s.astype(BF16), v))
        st = st_ref[b, pair]
        q_dec = (q.astype(F32) * qs_ref[:, sl]).astype(BF16)
        o = jnp.where(first, intra[0], intra[1]) + _dot_nt(q_dec, st.astype(BF16))
        k_dec = (k.astype(F32) * ks_ref[:, sl]).astype(BF16)
        st_ref[b, pair] = cd_ref[:, sl] * st + jnp.where(same_head, _dot_tn(v, k_dec), 0.0)

        o = o * lax.rsqrt(_head_mean_sq(o, ones) + EPS) * gain_ref[:, sl]
        g = g_ref[b, :, sl].astype(F32)
        o_ref[b, :, sl] = (o * (g * _sigmoid(g))).astype(BF16)


def _retention(pc, dec, qs, ks, cd, gain, ones, batch, seq_len):
    n = pc.shape[0]
    c = C_RET
    nb = NB_RET
    const2 = lambda b, t: (0, 0)
    pc3 = pc.reshape(batch, seq_len, 4 * D_RET)
    blk = lambda j: pl.BlockSpec((nb, c, D_RET), lambda b, t: (b, t, j))
    out = pl.pallas_call(
        _ret_kernel,
        grid=(batch // nb, seq_len // c),
        in_specs=[
            blk(0), blk(1), blk(2), blk(3),
            pl.BlockSpec((H_RET, c, c), lambda b, t: (0, 0, 0)),
            pl.BlockSpec((c, D_RET), const2),
            pl.BlockSpec((c, D_RET), const2),
            pl.BlockSpec((1, D_RET), const2),
            pl.BlockSpec((1, D_RET), const2),
            pl.BlockSpec((LANES, LANES), const2),
        ],
        out_specs=blk(0),
        out_shape=jax.ShapeDtypeStruct((batch, seq_len, D_RET), BF16),
        scratch_shapes=[pltpu.VMEM((nb, H_RET // 2, LANES, LANES), F32)],
        compiler_params=pltpu.CompilerParams(
            dimension_semantics=("arbitrary", "arbitrary"), vmem_limit_bytes=VMEM_LIMIT),
        name="retention",
    )(pc3, pc3, pc3, pc3, dec, qs, ks, cd, gain, ones)
    return out.reshape(n, D_RET)


def _fox_kernel(qt_ref, k_ref, vt_ref, g_ref, o_ref, s_scr, p_scr, acc_scr):
    tq = TQ_FOX
    nq = o_ref.shape[0] // tq
    blocks = [(qi, kj) for qi in range(nq) for kj in range(qi + 1)]
    key = lax.broadcasted_iota(jnp.int32, (tq, tq), 0)
    qry = lax.broadcasted_iota(jnp.int32, (tq, tq), 1)
    causal = qry >= key

    def scores(t):
        qi, kj = blocks[t]
        for hh in range(2):
            slot = slice(hh * LANES, (hh + 1) * LANES)
            s_scr[t % 2, hh] = _dot(k_ref[kj * tq:(kj + 1) * tq, slot],
                                    qt_ref[0, slot, qi * tq:(qi + 1) * tq])

    def softmax(t, m_old):
        qi, kj = blocks[t]
        m_new, alpha = [], []
        for hh in range(2):
            s = s_scr[t % 2, hh]
            if kj == qi:
                s = jnp.where(causal, s, MASK_VALUE)
            m_blk = jnp.max(s, axis=0, keepdims=True)
            if kj > 0:
                m_blk = jnp.maximum(m_old[hh], m_blk)
                alpha.append(jnp.exp2(m_old[hh] - m_blk))
            p_scr[t % 2, hh] = jnp.exp2(s - m_blk).astype(BF16)
            m_new.append(m_blk)
        return m_new, alpha

    def accumulate(t, alpha):
        qi, kj = blocks[t]
        for hh in range(2):
            pv = _dot(vt_ref[0, hh * VT_ROWS:(hh + 1) * VT_ROWS, kj * tq:(kj + 1) * tq],
                      p_scr[t % 2, hh])
            acc_scr[qi % 2, hh] = pv if kj == 0 else alpha[hh] * acc_scr[qi % 2, hh] + pv
        if kj == qi:
            heads = []
            for hh in range(2):
                total = acc_scr[qi % 2, hh]
                heads.append(total[:HEAD_DIM] / total[HEAD_DIM:HEAD_DIM + 1])
            o_t = jnp.concatenate(heads, axis=0)
            rows = slice(qi * tq, (qi + 1) * tq)
            o_ref[rows, :] = (o_t.T * _sigmoid(g_ref[rows, :].astype(F32))).astype(BF16)

    scores(0)
    m_run, alpha_prev = None, None
    for t in range(len(blocks)):
        m_run, alpha_t = softmax(t, m_run)
        if t + 1 < len(blocks):
            scores(t + 1)
        if t > 0:
            accumulate(t - 1, alpha_prev)
        alpha_prev = alpha_t
    accumulate(len(blocks) - 1, alpha_prev)


def _fox(qt, ka, vt, gb, batch, seq_len):
    n = ka.shape[0]
    tq = TQ_FOX
    pairs = H_FOX // 2
    return pl.pallas_call(
        _fox_kernel,
        grid=(batch, pairs),
        in_specs=[
            pl.BlockSpec((1, 2 * LANES, seq_len), lambda b, p: (b, p, 0)),
            pl.BlockSpec((seq_len, 2 * LANES), lambda b, p: (b, p)),
            pl.BlockSpec((1, 2 * VT_ROWS, seq_len), lambda b, p: (b, p, 0)),
            pl.BlockSpec((seq_len, LANES), lambda b, p: (b, p)),
        ],
        out_specs=pl.BlockSpec((seq_len, LANES), lambda b, p: (b, p)),
        out_shape=jax.ShapeDtypeStruct((n, D_FOX), BF16),
        scratch_shapes=[
            pltpu.VMEM((2, 2, tq, tq), F32),
            pltpu.VMEM((2, 2, tq, tq), BF16),
            pltpu.VMEM((2, 2, VT_ROWS, tq), F32),
        ],
        compiler_params=pltpu.CompilerParams(
            dimension_semantics=("arbitrary", "arbitrary"), vmem_limit_bytes=VMEM_LIMIT),
        name="fox",
    )(qt, ka, vt, gb)


def _mix_ffn_kernel(oa_ref, ob_ref, oc_ref, wa_ref, wb_ref, wc_ref, gmix_ref, x_ref, gpre_ref,
                    wup_ref, cw_ref, cb_ref, wd_ref, gpost_ref, out_ref, xp_scr, act_scr, tail_scr,
                    *, tiles_per_seq):
    i = pl.program_id(0)
    tm = x_ref.shape[0]
    fc = FC_FFN

    mixed = (_dot(oa_ref[...], wa_ref[...]) + _dot(ob_ref[...], wb_ref[...])
             + _dot(oc_ref[...], wc_ref[...]))
    x_mid = x_ref[...] + _rmsnorm_rows(mixed, gmix_ref[...])
    xp_scr[...] = pltpu.einshape("(ab)f->(ba)f", x_mid, a=SUBLANES)
    h = _rmsnorm_rows(xp_scr[...], gpre_ref[...]).astype(BF16)

    @pl.when(i == 0)
    def _():
        tail_scr[...] = jnp.zeros_like(tail_scr)

    seq_start = i % tiles_per_seq == 0
    first_sublane = lax.broadcasted_iota(jnp.int32, (SUBLANES, fc), 0) == 0

    def causal_conv(u, col, slot):
        prev = jnp.where(seq_start, 0.0, tail_scr[slot])
        last = u[tm - 2 * SUBLANES:, :]
        tail_scr[slot] = last
        head = []
        for g in range(2):
            rows = slice(g * SUBLANES, (g + 1) * SUBLANES)
            entering = prev[(g + 1) * SUBLANES - 1:(g + 1) * SUBLANES, :]
            head.append(jnp.where(first_sublane, entering, pltpu.roll(last[rows], 1, 0)))
        u1 = jnp.concatenate([head[1], u[:tm - SUBLANES]], axis=0)
        u2 = jnp.concatenate([head[0], head[1], u[:tm - 2 * SUBLANES]], axis=0)
        cols = slice(col, col + fc)
        return (cb_ref[:, cols] + cw_ref[0:1, cols] * u2 + cw_ref[1:2, cols] * u1
                + cw_ref[2:3, cols] * u)

    for c in range(D_FF // fc):
        gate = causal_conv(_dot(h, wup_ref[:, c * fc:(c + 1) * fc]), c * fc, 2 * c)
        val = causal_conv(_dot(h, wup_ref[:, D_FF + c * fc:D_FF + (c + 1) * fc]),
                          D_FF + c * fc, 2 * c + 1)
        act_scr[:, c * fc:(c + 1) * fc] = (gate * _sigmoid(gate) * val).astype(BF16)

    y = _dot(act_scr[...], wd_ref[...])
    res = xp_scr[...] + _rmsnorm_rows(y, gpost_ref[...])
    out_ref[...] = pltpu.einshape("(ba)f->(ab)f", res, a=SUBLANES)


def _mix_ffn(oa, ob, oc, wa, wb, wc, gmix, x2, gpre, w_up, conv_w, conv_b, w_down, gpost, seq_len):
    n = x2.shape[0]
    tm = TM_FFN
    nf = D_FF // FC_FFN
    tiles_per_seq = seq_len // tm
    row = lambda i: (i, 0)
    resident = lambda shape: pl.BlockSpec(shape, lambda i: (0, 0), pipeline_mode=pl.Buffered(1))
    return pl.pallas_call(
        functools.partial(_mix_ffn_kernel, tiles_per_seq=tiles_per_seq),
        grid=(n // tm,),
        in_specs=[
            pl.BlockSpec((tm, D_HGRN), row),
            pl.BlockSpec((tm, D_FOX), row),
            pl.BlockSpec((tm, D_RET), row),
            resident((D_HGRN, D_MODEL)),
            resident((D_FOX, D_MODEL)),
            resident((D_RET, D_MODEL)),
            resident((1, D_MODEL)),
            pl.BlockSpec((tm, D_MODEL), row),
            resident((1, D_MODEL)),
            resident((D_MODEL, 2 * D_FF)),
            resident((CONV_W, 2 * D_FF)),
            resident((1, 2 * D_FF)),
            resident((D_FF, D_MODEL)),
            resident((1, D_MODEL)),
        ],
        out_specs=pl.BlockSpec((tm, D_MODEL), row),
        out_shape=jax.ShapeDtypeStruct((n, D_MODEL), F32),
        scratch_shapes=[
            pltpu.VMEM((tm, D_MODEL), F32),
            pltpu.VMEM((tm, D_FF), BF16),
            pltpu.VMEM((2 * nf, 2 * SUBLANES, FC_FFN), F32),
        ],
        compiler_params=pltpu.CompilerParams(
            dimension_semantics=("arbitrary",), vmem_limit_bytes=VMEM_LIMIT),
        name="mix_ffn",
    )(oa, ob, oc, wa, wb, wc, gmix, x2, gpre, w_up, conv_w, conv_b, w_down, gpost)


def _block_ones(n):
    idx = jnp.arange(n) // HEAD_DIM
    return (idx[:, None] == idx[None, :]).astype(BF16)


def _lower_tri(n, block):
    r = jnp.arange(n)
    return ((r[:, None] >= r[None, :]) & (r[:, None] // block == r[None, :] // block)).astype(BF16)


def _rotary_tables(seq_len):
    inv_freq = 1.0 / (ROPE_BASE ** (jnp.arange(0, HEAD_DIM, 2, dtype=F32) / HEAD_DIM))
    ang = jnp.arange(seq_len, dtype=F32)[:, None] * inv_freq[None, :]
    cos, sin = jnp.cos(ang), jnp.sin(ang)
    cos_full = jnp.tile(jnp.concatenate([cos, cos], axis=-1), (1, H_RET))
    sin_signed = jnp.tile(jnp.concatenate([-sin, sin], axis=-1), (1, H_RET))
    return cos_full, sin_signed


def _retention_tables(c):
    log_gamma = jnp.log1p(-jnp.exp2(-5.0 - jnp.arange(H_RET, dtype=F32)))
    pos = jnp.arange(c, dtype=F32)
    rel = pos[:, None] - pos[None, :]
    dec = jnp.where(rel >= 0, jnp.exp(log_gamma[:, None, None] * jnp.maximum(rel, 0.0)), 0.0)
    per_lane = lambda a: jnp.repeat(a, HEAD_DIM, axis=-1)
    qs = per_lane(jnp.exp(log_gamma[None, :] * (pos[:, None] + 1.0)))
    ks = per_lane(jnp.exp(log_gamma[None, :] * (c - 1.0 - pos[:, None])))
    cd = per_lane(jnp.exp(log_gamma * c)[None, :])
    return dec, qs, ks, cd


def _reorder_in_weight_rows(w_t):
    widths = [D_HGRN] * 4 + [D_FOX] * 3 + [H_FOX] + [D_FOX] + [D_RET] * 4
    offs = [0]
    for wd in widths:
        offs.append(offs[-1] + wd)
    part = lambda j: w_t[offs[j]:offs[j + 1]]
    a_q, a_f, a_i, a_g, b_q, b_k, b_v, b_f, b_g, c_q, c_k, c_v, c_g = (part(j) for j in range(13))
    b_f = jnp.pad(b_f, ((0, FGATE_PAD - H_FOX), (0, 0)))
    return jnp.concatenate(
        [a_q, a_i, a_g, a_f, b_q, b_k, b_v, b_g, b_f, c_q, c_k, c_v, c_g], axis=0)


def _transpose_cast_kernel(w_ref, o_ref):
    o_ref[...] = w_ref[...].T.astype(BF16)


def _transpose_cast(w_t):
    rows = w_t.shape[0]
    blk = D_HGRN
    return pl.pallas_call(
        _transpose_cast_kernel,
        grid=(rows // blk,),
        in_specs=[pl.BlockSpec((blk, D_MODEL), lambda i: (i, 0))],
        out_specs=pl.BlockSpec((D_MODEL, blk), lambda i: (0, i)),
        out_shape=jax.ShapeDtypeStruct((D_MODEL, rows), BF16),
        compiler_params=pltpu.CompilerParams(
            dimension_semantics=("arbitrary",), vmem_limit_bytes=VMEM_LIMIT),
        name="wprep",
    )(w_t)


def kernel(x, w_in, b_fox_f, fox_q_gain, fox_k_gain, hgrn_lb, hgrn_out_gain, ret_out_gain, w_out,
           g_mix_pre, g_mix_post, w_up, conv_w, conv_b, w_down, g_ffn_pre, g_ffn_post):
    batch, seq_len, d_model = x.shape
    assert d_model == D_MODEL
    assert seq_len % TM_FFN == 0 and seq_len % TQ_FOX == 0
    assert batch % NB_HGRN == 0 and batch % NB_RET == 0
    depth = w_in.shape[0]
    n = batch * seq_len

    cos_full, sin_signed = _rotary_tables(seq_len)
    dec, qs, ks, cd = _retention_tables(C_RET)
    tri_proj = _lower_tri(TM_PROJ, TM_PROJ)
    mats_hgrn = _hgrn_matrices(TT_HGRN)
    ones_pair = _block_ones(LANES)
    lb = hgrn_lb.astype(F32)

    w_in_t = jnp.transpose(w_in, (2, 0, 1))

    x2 = x.reshape(n, D_MODEL)
    for l in range(depth):
        w = _transpose_cast(_reorder_in_weight_rows(w_in_t[:, l, :]))
        bf = jnp.pad(b_fox_f[l].astype(F32), (0, FGATE_PAD - H_FOX))[None, :]
        qg = jnp.tile(fox_q_gain[l].astype(F32), H_FOX)[None, :]
        kg = jnp.tile(fox_k_gain[l].astype(F32), H_FOX)[None, :]
        pa, za, qt, ka, vt, gb, pc = _inproj(x2, g_mix_pre[l][None, :], w, bf, qg, kg, cos_full,
                                             sin_signed, tri_proj, ones_pair, seq_len)

        o_a = _hgrn(pa, za, lb, hgrn_out_gain[l].reshape(1, D_HGRN), mats_hgrn, ones_pair,
                    batch, seq_len, l)
        o_b = _fox(qt, ka, vt, gb, batch, seq_len)
        o_c = _retention(pc, dec, qs, ks, cd, ret_out_gain[l].reshape(1, D_RET), ones_pair,
                         batch, seq_len)

        wo = w_out[l].astype(BF16)
        x2 = _mix_ffn(o_a, o_b, o_c, wo[:D_HGRN], wo[D_HGRN:D_HGRN + D_FOX], wo[D_HGRN + D_FOX:],
                      g_mix_post[l][None, :], x2, g_ffn_pre[l][None, :], w_up[l].astype(BF16),
                      conv_w[l], conv_b[l][None, :], w_down[l].astype(BF16),
                      g_ffn_post[l][None, :], seq_len)
    return x2.reshape(batch, seq_len, D_MODEL)
```

```python
import functools

import jax
import jax.numpy as jnp
from jax import lax
from jax.experimental import pallas as pl
from jax.experimental.pallas import tpu as pltpu

F32 = jnp.float32
BF16 = jnp.bfloat16

D_MODEL = 1024
HEAD_DIM = 64
H_HGRN, H_FOX, H_RET = 6, 6, 4
D_HGRN, D_FOX, D_RET = H_HGRN * HEAD_DIM, H_FOX * HEAD_DIM, H_RET * HEAD_DIM
D_FF = 2816
CONV_W = 3
ROPE_BASE = 10000.0
EPS = 1e-6
MIN_FORGET = 1e-12
MASK_VALUE = -1e30

LANES = 128
SUBLANES = 8
FGATE_PAD = LANES
BF16_SUBLANES = 16
VT_ROWS = HEAD_DIM + BF16_SUBLANES
LOG2E = 1.4426950408889634

_A_MAIN = (0, 3 * D_HGRN)
_A_GATE = (_A_MAIN[1], _A_MAIN[1] + D_HGRN)
_B_Q = (_A_GATE[1], _A_GATE[1] + D_FOX)
_B_K = (_B_Q[1], _B_Q[1] + D_FOX)
_B_V = (_B_K[1], _B_K[1] + D_FOX)
_B_G = (_B_V[1], _B_V[1] + D_FOX)
_B_F = (_B_G[1], _B_G[1] + FGATE_PAD)
_C_Q = (_B_F[1], _B_F[1] + D_RET)
_C_K = (_C_Q[1], _C_Q[1] + D_RET)
_C_VG = (_C_K[1], _C_K[1] + 2 * D_RET)
IN_COLS_PADDED = _C_VG[1]

TM_PROJ = 512
TM_FFN = 1024
FC_FFN = 256
TT_HGRN = 256
NB_HGRN = 4
SUB_HGRN = 32
SAFE_DIAG = 4
FAST_PATH_MAX_DECAY = 60.0
C_RET = 256
NB_RET = 4
TQ_FOX = 512
VMEM_LIMIT = 56 * 1024 * 1024


def _dot(a, b):
    return jnp.dot(a, b, preferred_element_type=F32)


def _dot_nt(a, b):
    return lax.dot_general(a, b, (((1,), (1,)), ((), ())), preferred_element_type=F32)


def _dot_tn(a, b):
    return lax.dot_general(a, b, (((0,), (0,)), ((), ())), preferred_element_type=F32)


def _dot_01_lhs(m01, x):
    hi = x.astype(BF16)
    r = x - hi.astype(F32)
    mid = r.astype(BF16)
    lo = (r - mid.astype(F32)).astype(BF16)
    return _dot(m01, hi) + _dot(m01, mid) + _dot(m01, lo)


def _head_mean_sq(y, ones_bd):
    sq = y * y
    hi = sq.astype(BF16)
    lo = (sq - hi.astype(F32)).astype(BF16)
    return (_dot(hi, ones_bd) + _dot(lo, ones_bd)) * (1.0 / HEAD_DIM)


def _rmsnorm_rows(x, gain):
    return x * lax.rsqrt(jnp.mean(x * x, axis=-1, keepdims=True) + EPS) * gain


def _sigmoid(x):
    return 1.0 / (1.0 + jnp.exp(-x))


def _first_head_lanes(shape):
    return (lax.broadcasted_iota(jnp.int32, shape, len(shape) - 1) & HEAD_DIM) == 0


def _bf16_terms(x):
    hi = x.astype(BF16).astype(F32)
    mid = (x - hi).astype(BF16).astype(F32)
    lo = (x - hi - mid).astype(BF16).astype(F32)
    return hi, mid, lo


def _inproj_kernel(x_ref, gpre_ref, w_ref, bf_ref, qg_ref, kg_ref, cos_ref, sin_ref, tri_ref,
                   ones_ref, pa_ref, za_ref, qt_ref, ka_ref, vt_ref, gb_ref, pc_ref, carry_ref,
                   *, tiles_per_seq):
    @pl.when(pl.program_id(0) % tiles_per_seq == 0)
    def _():
        carry_ref[...] = jnp.zeros_like(carry_ref)

    h = _rmsnorm_rows(x_ref[...], gpre_ref[...]).astype(BF16)
    tm = h.shape[0]

    def proj(cols):
        return _dot(h, w_ref[:, cols[0]:cols[1]])

    zf = proj(_B_F) + bf_ref[...]
    log_f = -(jnp.maximum(-zf, 0.0) + jnp.log1p(jnp.exp(-jnp.abs(zf))))
    cum = _dot_01_lhs(tri_ref[...], log_f) + carry_ref[...]
    carry_ref[...] = cum[tm - 1:, :]

    ones = ones_ref[...]
    bq = proj(_B_Q)
    bk = proj(_B_K)
    lane = lax.broadcasted_iota(jnp.int32, (tm, LANES), 1)
    is_feature = lane < HEAD_DIM
    is_one_q = (lane >= HEAD_DIM + 3) & (lane < HEAD_DIM + 6)
    is_one_k = (lane >= HEAD_DIM) & (lane < HEAD_DIM + 3)
    for pair in range(H_FOX // 2):
        sl = slice(pair * LANES, (pair + 1) * LANES)
        q2 = bq[:, sl]
        q2 = (q2 * lax.rsqrt(_head_mean_sq(q2, ones) + EPS) * qg_ref[:, sl]
              * (HEAD_DIM ** -0.5 * LOG2E))
        k2 = bk[:, sl]
        k2 = k2 * lax.rsqrt(_head_mean_sq(k2, ones) + EPS) * kg_ref[:, sl]
        for odd in range(2):
            head = 2 * pair + odd
            slot = slice(head * LANES, (head + 1) * LANES)
            qh = pltpu.roll(q2, HEAD_DIM, 1) if odd else q2
            kh = pltpu.roll(k2, HEAD_DIM, 1) if odd else k2
            hi, mid, lo = _bf16_terms(
                jnp.broadcast_to(cum[:, head:head + 1], (tm, LANES)) * LOG2E)
            q_bias = jnp.where(lane == HEAD_DIM, hi, jnp.where(lane == HEAD_DIM + 1, mid, jnp.where(
                lane == HEAD_DIM + 2, lo, jnp.where(is_one_q, 1.0, 0.0))))
            k_bias = jnp.where(lane == HEAD_DIM + 3, -hi, jnp.where(lane == HEAD_DIM + 4, -mid, jnp.where(
                lane == HEAD_DIM + 5, -lo, jnp.where(is_one_k, 1.0, 0.0))))
            qt_ref[0, slot, :] = jnp.where(is_feature, qh, q_bias).T.astype(BF16)
            ka_ref[:, slot] = jnp.where(is_feature, kh, k_bias).astype(BF16)
    pa_ref[...] = proj(_A_MAIN).astype(BF16)
    za_ref[...] = proj(_A_GATE)

    v_t = proj(_B_V).T.astype(BF16)
    ones_row = jnp.where(lax.broadcasted_iota(jnp.int32, (BF16_SUBLANES, tm), 0) == 0,
                         1.0, 0.0).astype(BF16)
    for head in range(H_FOX):
        vt_ref[0, head * VT_ROWS:head * VT_ROWS + HEAD_DIM, :] = (
            v_t[head * HEAD_DIM:(head + 1) * HEAD_DIM, :])
        vt_ref[0, head * VT_ROWS + HEAD_DIM:(head + 1) * VT_ROWS, :] = ones_row
    gb_ref[...] = proj(_B_G).astype(BF16)

    cos = cos_ref[...]
    sin = sin_ref[...]
    low_half = (lax.broadcasted_iota(jnp.int32, cos.shape, 1) & (HEAD_DIM // 2)) == 0

    def rotary(y):
        swapped = jnp.where(low_half, pltpu.roll(y, D_RET - HEAD_DIM // 2, 1),
                            pltpu.roll(y, HEAD_DIM // 2, 1))
        return y * cos + swapped * sin

    pc_ref[:, 0:D_RET] = rotary(proj(_C_Q)).astype(BF16)
    pc_ref[:, D_RET:2 * D_RET] = (rotary(proj(_C_K)) * (HEAD_DIM ** -0.5)).astype(BF16)
    pc_ref[:, 2 * D_RET:4 * D_RET] = proj(_C_VG).astype(BF16)


def _inproj(x2, gpre, w, bf, qg, kg, cos, sin, tri, ones, seq_len):
    n = x2.shape[0]
    batch = n // seq_len
    tm = TM_PROJ
    tiles_per_seq = seq_len // tm
    const = lambda i: (0, 0)
    row = lambda i: (i, 0)
    seq_t = lambda i: (i // tiles_per_seq, 0, i % tiles_per_seq)
    return pl.pallas_call(
        functools.partial(_inproj_kernel, tiles_per_seq=tiles_per_seq),
        grid=(n // tm,),
        in_specs=[
            pl.BlockSpec((tm, D_MODEL), row),
            pl.BlockSpec((1, D_MODEL), const),
            pl.BlockSpec((D_MODEL, IN_COLS_PADDED), const),
            pl.BlockSpec((1, FGATE_PAD), const),
            pl.BlockSpec((1, D_FOX), const),
            pl.BlockSpec((1, D_FOX), const),
            pl.BlockSpec((tm, D_RET), lambda i: (i % tiles_per_seq, 0)),
            pl.BlockSpec((tm, D_RET), lambda i: (i % tiles_per_seq, 0)),
            pl.BlockSpec((tm, tm), const),
            pl.BlockSpec((LANES, LANES), const),
        ],
        out_specs=[
            pl.BlockSpec((tm, 3 * D_HGRN), row),
            pl.BlockSpec((tm, D_HGRN), row),
            pl.BlockSpec((1, H_FOX * LANES, tm), seq_t),
            pl.BlockSpec((tm, H_FOX * LANES), row),
            pl.BlockSpec((1, H_FOX * VT_ROWS, tm), seq_t),
            pl.BlockSpec((tm, D_FOX), row),
            pl.BlockSpec((tm, 4 * D_RET), row),
        ],
        out_shape=[
            jax.ShapeDtypeStruct((n, 3 * D_HGRN), BF16),
            jax.ShapeDtypeStruct((n, D_HGRN), F32),
            jax.ShapeDtypeStruct((batch, H_FOX * LANES, seq_len), BF16),
            jax.ShapeDtypeStruct((n, H_FOX * LANES), BF16),
            jax.ShapeDtypeStruct((batch, H_FOX * VT_ROWS, seq_len), BF16),
            jax.ShapeDtypeStruct((n, D_FOX), BF16),
            jax.ShapeDtypeStruct((n, 4 * D_RET), BF16),
        ],
        scratch_shapes=[pltpu.VMEM((1, FGATE_PAD), F32)],
        compiler_params=pltpu.CompilerParams(
            dimension_semantics=("arbitrary",), vmem_limit_bytes=VMEM_LIMIT),
        name="inproj",
    )(x2, gpre, w, bf, qg, kg, cos, sin, tri, ones)


_M_CUM, _M_AFTER, _M_DIAG, _M_MERGE0 = 0, 1, 2, 3
_MERGE_BLOCKS = (2 * SAFE_DIAG, 4 * SAFE_DIAG, 8 * SAFE_DIAG)
assert _MERGE_BLOCKS[-1] == SUB_HGRN


def _hgrn_matrices(tt):
    t = jnp.arange(tt)[:, None]
    u = jnp.arange(tt)[None, :]
    same = lambda m: (t // m) == (u // m)
    mats = [same(SUB_HGRN) & (u <= t), same(SUB_HGRN) & (u > t),
            same(SAFE_DIAG) & (u <= t) & (u > (t // SAFE_DIAG) * SAFE_DIAG)]
    for m in _MERGE_BLOCKS:
        mid = (t // m) * m + m // 2 - 1
        mats.append(same(m) & (((t > mid) & (u > mid) & (u <= t)) | ((t <= mid) & (u > t) & (u <= mid))))
    return jnp.stack(mats).astype(BF16)


def _hgrn_kernel(q_ref, v_ref, g_ref, z_ref, lb_ref, gain_ref, mats_ref, ones_ref, o_ref, st_ref,
                 logf_scr, k_scr, cum_scr, *, layer):
    @pl.when(pl.program_id(1) == 0)
    def _():
        st_ref[...] = jnp.zeros_like(st_ref)

    nb, tt = q_ref.shape[0], q_ref.shape[1]
    sub = SUB_HGRN

    lb = lb_ref[...]
    e = jnp.exp(lb - jnp.max(lb, axis=0, keepdims=True))
    p = e / jnp.sum(e, axis=0, keepdims=True)
    run = p[0:1]
    for i in range(1, layer + 1):
        run = run + p[i:i + 1]
    lower = run - p[0:1]

    lowest = None
    for b in range(nb):
        sg = _sigmoid(z_ref[b])
        f = lower + (1.0 - lower) * sg
        log_f = jnp.log(jnp.maximum(f, MIN_FORGET))
        cum = _dot_01_lhs(mats_ref[_M_CUM], log_f)
        logf_scr[b] = log_f
        k_scr[b] = (1.0 - lower) * (1.0 - sg)
        cum_scr[b] = cum
        low_b = jnp.min(cum)
        lowest = low_b if lowest is None else jnp.minimum(lowest, low_b)
    exact_fallback = lowest < -FAST_PATH_MAX_DECAY

    row = lax.broadcasted_iota(jnp.int32, (tt, tt), 0)
    col = lax.broadcasted_iota(jnp.int32, (tt, tt), 1)

    def diag_mask(m):
        in_block = row & (m - 1)
        dist = row - col
        return ((in_block - dist) | dist) >= 0

    def merge_mask(m):
        half = m // 2
        return (((row ^ col) & ~(m - 1)) | ((row & half) ^ half) | (col & half)) == 0

    first = _first_head_lanes((tt, LANES))
    rr = lax.broadcasted_iota(jnp.int32, (LANES, LANES), 0)
    cc = lax.broadcasted_iota(jnp.int32, (LANES, LANES), 1)
    same_head = ((rr ^ cc) & HEAD_DIM) == 0
    ones = ones_ref[...]

    def factors(c, exact):
        b, sl = c["b"], c["sl"]
        q = q_ref[b, :, sl].astype(F32)
        k = k_scr[b, :, sl]
        cum = cum_scr[b, :, sl]
        c["v"] = v_ref[b, :, sl]
        c["e_q"] = jnp.exp(cum)
        c["qt"] = (q * c["e_q"]).astype(BF16)
        if exact:
            log_f = logf_scr[b, :, sl]
            since = _dot_01_lhs(mats_ref[_M_DIAG], log_f)
            pieces = [((q * jnp.exp(since)).astype(BF16), (k * jnp.exp(-since)).astype(BF16),
                       diag_mask(SAFE_DIAG))]
            for idx, m in enumerate(_MERGE_BLOCKS):
                w = jnp.exp(_dot_01_lhs(mats_ref[_M_MERGE0 + idx], log_f))
                pieces.append(((q * w).astype(BF16), (k * w).astype(BF16), merge_mask(m)))
            c["k_state"] = (k * jnp.exp(_dot_01_lhs(mats_ref[_M_AFTER], log_f))).astype(BF16)
        else:
            c["k_state"] = (k * jnp.exp(-cum)).astype(BF16)
            pieces = [(c["qt"], c["k_state"], diag_mask(sub))]
        c["pieces"] = pieces

    def intra_scores(c):
        c["s"] = []
        for keep in (first, ~first):
            s = jnp.zeros((tt, tt), F32)
            for q_f, k_f, mask in c["pieces"]:
                s = jnp.where(mask, _dot_nt(jnp.where(keep, q_f, jnp.zeros_like(q_f)), k_f), s)
            c["s"].append(s.astype(BF16))

    def intra_values(c):
        c["o"] = jnp.where(first, _dot(c["s"][0], c["v"]), _dot(c["s"][1], c["v"]))

    def state_updates(c):
        c["upd"] = [jnp.where(same_head, _dot_tn(c["v"][j * sub:(j + 1) * sub],
                                                 c["k_state"][j * sub:(j + 1) * sub]), 0.0)
                    for j in range(tt // sub)]

    def state_chain(c, exact):
        st = st_ref[c["b"], c["pair"]]
        outs = []
        for j in range(tt // sub):
            r = slice(j * sub, (j + 1) * sub)
            outs.append(c["o"][r] + _dot_nt(c["qt"][r], st.astype(BF16)))
            decay = c["e_q"][(j + 1) * sub - 1:(j + 1) * sub, :]
            st = decay * st + c["upd"][j] if exact else decay * (st + c["upd"][j])
        st_ref[c["b"], c["pair"]] = st
        c["o"] = jnp.concatenate(outs, axis=0)

    def head_norm_and_gate(c):
        b, sl = c["b"], c["sl"]
        o = c["o"] * lax.rsqrt(_head_mean_sq(c["o"], ones) + EPS) * gain_ref[:, sl]
        g = g_ref[b, :, sl].astype(F32)
        o_ref[b, :, sl] = (o * (g * _sigmoid(g))).astype(BF16)

    def all_tiles(exact):
        tiles = [dict(b=b, pair=pair, sl=slice(pair * LANES, (pair + 1) * LANES))
                 for b in range(nb) for pair in range(H_HGRN // 2)]
        for stage in (functools.partial(factors, exact=exact), intra_scores, intra_values,
                      state_updates, functools.partial(state_chain, exact=exact),
                      head_norm_and_gate):
            for c in tiles:
                stage(c)

    pl.when(exact_fallback)(functools.partial(all_tiles, True))
    pl.when(jnp.logical_not(exact_fallback))(functools.partial(all_tiles, False))


def _hgrn(pa, za, lb, gain, mats, ones, batch, seq_len, layer):
    n = pa.shape[0]
    tt = TT_HGRN
    nb = NB_HGRN
    const = lambda b, t: (0, 0)
    tile_f32 = pltpu.VMEM((nb, tt, D_HGRN), F32)
    pa3 = pa.reshape(batch, seq_len, 3 * D_HGRN)
    blk = lambda j: pl.BlockSpec((nb, tt, D_HGRN), lambda b, t: (b, t, j))
    out = pl.pallas_call(
        functools.partial(_hgrn_kernel, layer=layer),
        grid=(batch // nb, seq_len // tt),
        in_specs=[
            blk(0), blk(1), blk(2), blk(0),
            pl.BlockSpec(lb.shape, const),
            pl.BlockSpec((1, D_HGRN), const),
            pl.BlockSpec(mats.shape, lambda b, t: (0, 0, 0)),
            pl.BlockSpec((LANES, LANES), const),
        ],
        out_specs=blk(0),
        out_shape=jax.ShapeDtypeStruct((batch, seq_len, D_HGRN), BF16),
        scratch_shapes=[pltpu.VMEM((nb, H_HGRN // 2, LANES, LANES), F32),
                        tile_f32, tile_f32, tile_f32],
        compiler_params=pltpu.CompilerParams(
            dimension_semantics=("arbitrary", "arbitrary"), vmem_limit_bytes=VMEM_LIMIT),
        name="hgrn",
    )(pa3, pa3, pa3, za.reshape(batch, seq_len, D_HGRN), lb, gain, mats, ones)
    return out.reshape(n, D_HGRN)


def _ret_kernel(q_ref, k_ref, v_ref, g_ref, dec_ref, qs_ref, ks_ref, cd_ref, gain_ref, ones_ref,
                o_ref, st_ref):
    @pl.when(pl.program_id(1) == 0)
    def _():
        st_ref[...] = jnp.zeros_like(st_ref)

    nb, c = q_ref.shape[0], q_ref.shape[1]
    first = _first_head_lanes((c, LANES))
    rr = lax.broadcasted_iota(jnp.int32, (LANES, LANES), 0)
    cc = lax.broadcasted_iota(jnp.int32, (LANES, LANES), 1)
    same_head = ((rr ^ cc) & HEAD_DIM) == 0
    ones = ones_ref[...]

    def scores(c):
        q, k = q_ref[c["b"], :, c["sl"]], k_ref[c["b"], :, c["sl"]]
        c["s"] = [(_dot_nt(jnp.where(keep, q, jnp.zeros_like(q)), k)
                   * dec_ref[2 * c["pair"] + hh]).astype(BF16)
                  for hh, keep in enumerate((first, ~first))]

    def values_and_state(c):
        b, pair, sl = c["b"], c["pair"], c["sl"]
        q, k, v = q_ref[b, :, sl], k_ref[b, :, sl], v_ref[b, :, sl]
        st = st_ref[b, pair]
        q_dec = (q.astype(F32) * qs_ref[:, sl]).astype(BF16)
        c["o"] = (jnp.where(first, _dot(c["s"][0], v), _dot(c["s"][1], v))
                  + _dot_nt(q_dec, st.astype(BF16)))
        k_dec = (k.astype(F32) * ks_ref[:, sl]).astype(BF16)
        st_ref[b, pair] = cd_ref[:, sl] * st + jnp.where(same_head, _dot_tn(v, k_dec), 0.0)

    def head_norm_and_gate(c):
        b, sl = c["b"], c["sl"]
        o = c["o"] * lax.rsqrt(_head_mean_sq(c["o"], ones) + EPS) * gain_ref[:, sl]
        g = g_ref[b, :, sl].astype(F32)
        o_ref[b, :, sl] = (o * (g * _sigmoid(g))).astype(BF16)

    tiles = [dict(b=b, pair=pair, sl=slice(pair * LANES, (pair + 1) * LANES))
             for b in range(nb) for pair in range(H_RET // 2)]
    for stage in (scores, values_and_state, head_norm_and_gate):
        for c in tiles:
            stage(c)


def _retention(pc, dec, qs, ks, cd, gain, ones, batch, seq_len):
    n = pc.shape[0]
    c = C_RET
    nb = NB_RET
    const2 = lambda b, t: (0, 0)
    pc3 = pc.reshape(batch, seq_len, 4 * D_RET)
    blk = lambda j: pl.BlockSpec((nb, c, D_RET), lambda b, t: (b, t, j))
    out = pl.pallas_call(
        _ret_kernel,
        grid=(batch // nb, seq_len // c),
        in_specs=[
            blk(0), blk(1), blk(2), blk(3),
            pl.BlockSpec((H_RET, c, c), lambda b, t: (0, 0, 0)),
            pl.BlockSpec((c, D_RET), const2),
            pl.BlockSpec((c, D_RET), const2),
            pl.BlockSpec((1, D_RET), const2),
            pl.BlockSpec((1, D_RET), const2),
            pl.BlockSpec((LANES, LANES), const2),
        ],
        out_specs=blk(0),
        out_shape=jax.ShapeDtypeStruct((batch, seq_len, D_RET), BF16),
        scratch_shapes=[pltpu.VMEM((nb, H_RET // 2, LANES, LANES), F32)],
        compiler_params=pltpu.CompilerParams(
            dimension_semantics=("arbitrary", "arbitrary"), vmem_limit_bytes=VMEM_LIMIT),
        name="retention",
    )(pc3, pc3, pc3, pc3, dec, qs, ks, cd, gain, ones)
    return out.reshape(n, D_RET)


def _fox_kernel(qt_ref, k_ref, vt_ref, g_ref, o_ref, s_scr, p_scr, acc_scr):
    tq = TQ_FOX
    nq = o_ref.shape[0] // tq
    blocks = [(qi, kj) for qi in range(nq) for kj in range(qi + 1)]
    key = lax.broadcasted_iota(jnp.int32, (tq, tq), 0)
    qry = lax.broadcasted_iota(jnp.int32, (tq, tq), 1)
    causal = qry >= key

    def scores(t):
        qi, kj = blocks[t]
        for hh in range(2):
            slot = slice(hh * LANES, (hh + 1) * LANES)
            s_scr[t % 2, hh] = _dot(k_ref[kj * tq:(kj + 1) * tq, slot],
                                    qt_ref[0, slot, qi * tq:(qi + 1) * tq])

    def softmax(t, m_old):
        qi, kj = blocks[t]
        m_new, alpha = [], []
        for hh in range(2):
            s = s_scr[t % 2, hh]
            if kj == qi:
                s = jnp.where(causal, s, MASK_VALUE)
            m_blk = jnp.max(s, axis=0, keepdims=True)
            if kj > 0:
                m_blk = jnp.maximum(m_old[hh], m_blk)
                alpha.append(jnp.exp2(m_old[hh] - m_blk))
            p_scr[t % 2, hh] = jnp.exp2(s - m_blk).astype(BF16)
            m_new.append(m_blk)
        return m_new, alpha

    def accumulate(t, alpha):
        qi, kj = blocks[t]
        for hh in range(2):
            pv = _dot(vt_ref[0, hh * VT_ROWS:(hh + 1) * VT_ROWS, kj * tq:(kj + 1) * tq],
                      p_scr[t % 2, hh])
            acc_scr[qi % 2, hh] = pv if kj == 0 else alpha[hh] * acc_scr[qi % 2, hh] + pv
        if kj == qi:
            heads = []
            for hh in range(2):
                total = acc_scr[qi % 2, hh]
                heads.append(total[:HEAD_DIM] / total[HEAD_DIM:HEAD_DIM + 1])
            o_t = jnp.concatenate(heads, axis=0)
            rows = slice(qi * tq, (qi + 1) * tq)
            o_ref[rows, :] = (o_t.T * _sigmoid(g_ref[rows, :].astype(F32))).astype(BF16)

    scores(0)
    m_run, alpha_prev = None, None
    for t in range(len(blocks)):
        m_run, alpha_t = softmax(t, m_run)
        if t + 1 < len(blocks):
            scores(t + 1)
        if t > 0:
            accumulate(t - 1, alpha_prev)
        alpha_prev = alpha_t
    accumulate(len(blocks) - 1, alpha_prev)


def _fox(qt, ka, vt, gb, batch, seq_len):
    n = ka.shape[0]
    tq = TQ_FOX
    pairs = H_FOX // 2
    return pl.pallas_call(
        _fox_kernel,
        grid=(batch, pairs),
        in_specs=[
            pl.BlockSpec((1, 2 * LANES, seq_len), lambda b, p: (b, p, 0)),
            pl.BlockSpec((seq_len, 2 * LANES), lambda b, p: (b, p)),
            pl.BlockSpec((1, 2 * VT_ROWS, seq_len), lambda b, p: (b, p, 0)),
            pl.BlockSpec((seq_len, LANES), lambda b, p: (b, p)),
        ],
        out_specs=pl.BlockSpec((seq_len, LANES), lambda b, p: (b, p)),
        out_shape=jax.ShapeDtypeStruct((n, D_FOX), BF16),
        scratch_shapes=[
            pltpu.VMEM((2, 2, tq, tq), F32),
            pltpu.VMEM((2, 2, tq, tq), BF16),
            pltpu.VMEM((2, 2, VT_ROWS, tq), F32),
        ],
        compiler_params=pltpu.CompilerParams(
            dimension_semantics=("arbitrary", "arbitrary"), vmem_limit_bytes=VMEM_LIMIT),
        name="fox",
    )(qt, ka, vt, gb)


def _mix_ffn_kernel(oa_ref, ob_ref, oc_ref, wa_ref, wb_ref, wc_ref, gmix_ref, x_ref, gpre_ref,
                    wup_ref, cw_ref, cb_ref, wd_ref, gpost_ref, out_ref, xp_scr, act_scr, tail_scr,
                    *, tiles_per_seq):
    i = pl.program_id(0)
    tm = x_ref.shape[0]
    fc = FC_FFN

    mixed = (_dot(oa_ref[...], wa_ref[...]) + _dot(ob_ref[...], wb_ref[...])
             + _dot(oc_ref[...], wc_ref[...]))
    x_mid = x_ref[...] + _rmsnorm_rows(mixed, gmix_ref[...])
    xp_scr[...] = pltpu.einshape("(ab)f->(ba)f", x_mid, a=SUBLANES)
    h = _rmsnorm_rows(xp_scr[...], gpre_ref[...]).astype(BF16)

    @pl.when(i == 0)
    def _():
        tail_scr[...] = jnp.zeros_like(tail_scr)

    seq_start = i % tiles_per_seq == 0
    first_sublane = lax.broadcasted_iota(jnp.int32, (SUBLANES, fc), 0) == 0

    def causal_conv(u, col, slot):
        prev = jnp.where(seq_start, 0.0, tail_scr[slot])
        last = u[tm - 2 * SUBLANES:, :]
        tail_scr[slot] = last
        head = []
        for g in range(2):
            rows = slice(g * SUBLANES, (g + 1) * SUBLANES)
            entering = prev[(g + 1) * SUBLANES - 1:(g + 1) * SUBLANES, :]
            head.append(jnp.where(first_sublane, entering, pltpu.roll(last[rows], 1, 0)))
        u1 = jnp.concatenate([head[1], u[:tm - SUBLANES]], axis=0)
        u2 = jnp.concatenate([head[0], head[1], u[:tm - 2 * SUBLANES]], axis=0)
        cols = slice(col, col + fc)
        return (cb_ref[:, cols] + cw_ref[0:1, cols] * u2 + cw_ref[1:2, cols] * u1
                + cw_ref[2:3, cols] * u)

    for c in range(D_FF // fc):
        gate = causal_conv(_dot(h, wup_ref[:, c * fc:(c + 1) * fc]), c * fc, 2 * c)
        val = causal_conv(_dot(h, wup_ref[:, D_FF + c * fc:D_FF + (c + 1) * fc]),
                          D_FF + c * fc, 2 * c + 1)
        act_scr[:, c * fc:(c + 1) * fc] = (gate * _sigmoid(gate) * val).astype(BF16)

    y = _dot(act_scr[...], wd_ref[...])
    res = xp_scr[...] + _rmsnorm_rows(y, gpost_ref[...])
    out_ref[...] = pltpu.einshape("(ba)f->(ab)f", res, a=SUBLANES)


def _mix_ffn(oa, ob, oc, wa, wb, wc, gmix, x2, gpre, w_up, conv_w, conv_b, w_down, gpost, seq_len):
    n = x2.shape[0]
    tm = TM_FFN
    nf = D_FF // FC_FFN
    tiles_per_seq = seq_len // tm
    row = lambda i: (i, 0)
    resident = lambda shape: pl.BlockSpec(shape, lambda i: (0, 0), pipeline_mode=pl.Buffered(1))
    return pl.pallas_call(
        functools.partial(_mix_ffn_kernel, tiles_per_seq=tiles_per_seq),
        grid=(n // tm,),
        in_specs=[
            pl.BlockSpec((tm, D_HGRN), row),
            pl.BlockSpec((tm, D_FOX), row),
            pl.BlockSpec((tm, D_RET), row),
            resident((D_HGRN, D_MODEL)),
            resident((D_FOX, D_MODEL)),
            resident((D_RET, D_MODEL)),
            resident((1, D_MODEL)),
            pl.BlockSpec((tm, D_MODEL), row),
            resident((1, D_MODEL)),
            resident((D_MODEL, 2 * D_FF)),
            resident((CONV_W, 2 * D_FF)),
            resident((1, 2 * D_FF)),
            resident((D_FF, D_MODEL)),
            resident((1, D_MODEL)),
        ],
        out_specs=pl.BlockSpec((tm, D_MODEL), row),
        out_shape=jax.ShapeDtypeStruct((n, D_MODEL), F32),
        scratch_shapes=[
            pltpu.VMEM((tm, D_MODEL), F32),
            pltpu.VMEM((tm, D_FF), BF16),
            pltpu.VMEM((2 * nf, 2 * SUBLANES, FC_FFN), F32),
        ],
        compiler_params=pltpu.CompilerParams(
            dimension_semantics=("arbitrary",), vmem_limit_bytes=VMEM_LIMIT),
        name="mix_ffn",
    )(oa, ob, oc, wa, wb, wc, gmix, x2, gpre, w_up, conv_w, conv_b, w_down, gpost)


def _block_ones(n):
    idx = jnp.arange(n) // HEAD_DIM
    return (idx[:, None] == idx[None, :]).astype(BF16)


def _lower_tri(n, block):
    r = jnp.arange(n)
    return ((r[:, None] >= r[None, :]) & (r[:, None] // block == r[None, :] // block)).astype(BF16)


def _rotary_tables(seq_len):
    inv_freq = 1.0 / (ROPE_BASE ** (jnp.arange(0, HEAD_DIM, 2, dtype=F32) / HEAD_DIM))
    ang = jnp.arange(seq_len, dtype=F32)[:, None] * inv_freq[None, :]
    cos, sin = jnp.cos(ang), jnp.sin(ang)
    cos_full = jnp.tile(jnp.concatenate([cos, cos], axis=-1), (1, H_RET))
    sin_signed = jnp.tile(jnp.concatenate([-sin, sin], axis=-1), (1, H_RET))
    return cos_full, sin_signed


def _retention_tables(c):
    log_gamma = jnp.log1p(-jnp.exp2(-5.0 - jnp.arange(H_RET, dtype=F32)))
    pos = jnp.arange(c, dtype=F32)
    rel = pos[:, None] - pos[None, :]
    dec = jnp.where(rel >= 0, jnp.exp(log_gamma[:, None, None] * jnp.maximum(rel, 0.0)), 0.0)
    per_lane = lambda a: jnp.repeat(a, HEAD_DIM, axis=-1)
    qs = per_lane(jnp.exp(log_gamma[None, :] * (pos[:, None] + 1.0)))
    ks = per_lane(jnp.exp(log_gamma[None, :] * (c - 1.0 - pos[:, None])))
    cd = per_lane(jnp.exp(log_gamma * c)[None, :])
    return dec, qs, ks, cd


def _reorder_in_weight_rows(w_t):
    widths = [D_HGRN] * 4 + [D_FOX] * 3 + [H_FOX] + [D_FOX] + [D_RET] * 4
    offs = [0]
    for wd in widths:
        offs.append(offs[-1] + wd)
    part = lambda j: w_t[offs[j]:offs[j + 1]]
    a_q, a_f, a_i, a_g, b_q, b_k, b_v, b_f, b_g, c_q, c_k, c_v, c_g = (part(j) for j in range(13))
    b_f = jnp.pad(b_f, ((0, FGATE_PAD - H_FOX), (0, 0)))
    return jnp.concatenate(
        [a_q, a_i, a_g, a_f, b_q, b_k, b_v, b_g, b_f, c_q, c_k, c_v, c_g], axis=0)


def _transpose_cast_kernel(w_ref, o_ref):
    o_ref[...] = w_ref[...].T.astype(BF16)


def _transpose_cast(w_t):
    rows = w_t.shape[0]
    blk = D_HGRN
    return pl.pallas_call(
        _transpose_cast_kernel,
        grid=(rows // blk,),
        in_specs=[pl.BlockSpec((blk, D_MODEL), lambda i: (i, 0))],
        out_specs=pl.BlockSpec((D_MODEL, blk), lambda i: (0, i)),
        out_shape=jax.ShapeDtypeStruct((D_MODEL, rows), BF16),
        compiler_params=pltpu.CompilerParams(
            dimension_semantics=("arbitrary",), vmem_limit_bytes=VMEM_LIMIT),
        name="wprep",
    )(w_t)


def kernel(x, w_in, b_fox_f, fox_q_gain, fox_k_gain, hgrn_lb, hgrn_out_gain, ret_out_gain, w_out,
           g_mix_pre, g_mix_post, w_up, conv_w, conv_b, w_down, g_ffn_pre, g_ffn_post):
    batch, seq_len, d_model = x.shape
    assert d_model == D_MODEL
    assert seq_len % TM_FFN == 0 and seq_len % TQ_FOX == 0
    assert batch % NB_HGRN == 0 and batch % NB_RET == 0
    depth = w_in.shape[0]
    n = batch * seq_len

    cos_full, sin_signed = _rotary_tables(seq_len)
    dec, qs, ks, cd = _retention_tables(C_RET)
    tri_proj = _lower_tri(TM_PROJ, TM_PROJ)
    mats_hgrn = _hgrn_matrices(TT_HGRN)
    ones_pair = _block_ones(LANES)
    lb = hgrn_lb.astype(F32)

    w_in_t = jnp.transpose(w_in, (2, 0, 1))

    x2 = x.reshape(n, D_MODEL)
    for l in range(depth):
        w = _transpose_cast(_reorder_in_weight_rows(w_in_t[:, l, :]))
        bf = jnp.pad(b_fox_f[l].astype(F32), (0, FGATE_PAD - H_FOX))[None, :]
        qg = jnp.tile(fox_q_gain[l].astype(F32), H_FOX)[None, :]
        kg = jnp.tile(fox_k_gain[l].astype(F32), H_FOX)[None, :]
        pa, za, qt, ka, vt, gb, pc = _inproj(x2, g_mix_pre[l][None, :], w, bf, qg, kg, cos_full,
                                             sin_signed, tri_proj, ones_pair, seq_len)

        o_a = _hgrn(pa, za, lb, hgrn_out_gain[l].reshape(1, D_HGRN), mats_hgrn, ones_pair,
                    batch, seq_len, l)
        o_b = _fox(qt, ka, vt, gb, batch, seq_len)
        o_c = _retention(pc, dec, qs, ks, cd, ret_out_gain[l].reshape(1, D_RET), ones_pair,
                         batch, seq_len)

        wo = w_out[l].astype(BF16)
        x2 = _mix_ffn(o_a, o_b, o_c, wo[:D_HGRN], wo[D_HGRN:D_HGRN + D_FOX], wo[D_HGRN + D_FOX:],
                      g_mix_post[l][None, :], x2, g_ffn_pre[l][None, :], w_up[l].astype(BF16),
                      conv_w[l], conv_b[l][None, :], w_down[l].astype(BF16),
                      g_ffn_post[l][None, :], seq_len)
    return x2.reshape(batch, seq_len, D_MODEL)
```

```python
import functools

import jax
import jax.numpy as jnp
from jax import lax
from jax.experimental import pallas as pl
from jax.experimental.pallas import tpu as pltpu

F32 = jnp.float32
BF16 = jnp.bfloat16

D_MODEL = 1024
HEAD_DIM = 64
H_HGRN, H_FOX, H_RET = 6, 6, 4
D_HGRN, D_FOX, D_RET = H_HGRN * HEAD_DIM, H_FOX * HEAD_DIM, H_RET * HEAD_DIM
D_FF = 2816
CONV_W = 3
ROPE_BASE = 10000.0
EPS = 1e-6
MIN_FORGET = 1e-12
MASK_VALUE = -1e30

LANES = 128
SUBLANES = 8
FGATE_PAD = LANES
BF16_SUBLANES = 16
VT_ROWS = HEAD_DIM + BF16_SUBLANES
LOG2E = 1.4426950408889634

_A_MAIN = (0, 3 * D_HGRN)
_A_GATE = (_A_MAIN[1], _A_MAIN[1] + D_HGRN)
_B_Q = (_A_GATE[1], _A_GATE[1] + D_FOX)
_B_K = (_B_Q[1], _B_Q[1] + D_FOX)
_B_V = (_B_K[1], _B_K[1] + D_FOX)
_B_G = (_B_V[1], _B_V[1] + D_FOX)
_B_F = (_B_G[1], _B_G[1] + FGATE_PAD)
_C_Q = (_B_F[1], _B_F[1] + D_RET)
_C_K = (_C_Q[1], _C_Q[1] + D_RET)
_C_VG = (_C_K[1], _C_K[1] + 2 * D_RET)
IN_COLS_PADDED = _C_VG[1]

TM_PROJ = 512
TM_FFN = 1024
FC_FFN = 256
TT_HGRN = 256
NB_HGRN = 4
SUB_HGRN = 32
SAFE_DIAG = 4
FAST_PATH_MAX_DECAY = 60.0
C_RET = 256
NB_RET = 4
TQ_FOX = 512
VMEM_LIMIT = 56 * 1024 * 1024


def _dot(a, b):
    return jnp.dot(a, b, preferred_element_type=F32)


def _dot_nt(a, b):
    return lax.dot_general(a, b, (((1,), (1,)), ((), ())), preferred_element_type=F32)


def _dot_tn(a, b):
    return lax.dot_general(a, b, (((0,), (0,)), ((), ())), preferred_element_type=F32)


def _dot_01_lhs(m01, x):
    hi = x.astype(BF16)
    r = x - hi.astype(F32)
    mid = r.astype(BF16)
    lo = (r - mid.astype(F32)).astype(BF16)
    return _dot(m01, hi) + _dot(m01, mid) + _dot(m01, lo)


def _head_mean_sq(y, ones_bd):
    sq = y * y
    hi = sq.astype(BF16)
    lo = (sq - hi.astype(F32)).astype(BF16)
    return (_dot(hi, ones_bd) + _dot(lo, ones_bd)) * (1.0 / HEAD_DIM)


def _rmsnorm_rows(x, gain):
    return x * lax.rsqrt(jnp.mean(x * x, axis=-1, keepdims=True) + EPS) * gain


def _sigmoid(x):
    return 1.0 / (1.0 + jnp.exp(-x))


def _first_head_lanes(shape):
    return (lax.broadcasted_iota(jnp.int32, shape, len(shape) - 1) & HEAD_DIM) == 0


def _bf16_terms(x):
    hi = x.astype(BF16).astype(F32)
    mid = (x - hi).astype(BF16).astype(F32)
    lo = (x - hi - mid).astype(BF16).astype(F32)
    return hi, mid, lo


def _inproj_kernel(x_ref, gpre_ref, w_ref, bf_ref, qg_ref, kg_ref, cos_ref, sin_ref, tri_ref,
                   ones_ref, pa_ref, za_ref, qt_ref, ka_ref, vt_ref, gb_ref, pc_ref, carry_ref,
                   *, tiles_per_seq):
    @pl.when(pl.program_id(0) % tiles_per_seq == 0)
    def _():
        carry_ref[...] = jnp.zeros_like(carry_ref)

    h = _rmsnorm_rows(x_ref[...], gpre_ref[...]).astype(BF16)
    tm = h.shape[0]

    group_ends = (_A_GATE[1], _B_K[1], _B_G[1], IN_COLS_PADDED)
    group_result = {}

    def proj(cols):
        start = 0
        for end in group_ends:
            if cols[1] <= end:
                break
            start = end
        if start not in group_result:
            group_result[start] = _dot(h, w_ref[:, start:end])
        return group_result[start][:, cols[0] - start:cols[1] - start]

    zf = proj(_B_F) + bf_ref[...]
    log_f = -(jnp.maximum(-zf, 0.0) + jnp.log1p(jnp.exp(-jnp.abs(zf))))
    cum = _dot_01_lhs(tri_ref[...], log_f) + carry_ref[...]
    carry_ref[...] = cum[tm - 1:, :]

    ones = ones_ref[...]
    bq = proj(_B_Q)
    bk = proj(_B_K)
    lane = lax.broadcasted_iota(jnp.int32, (tm, LANES), 1)
    is_feature = lane < HEAD_DIM
    is_one_q = (lane >= HEAD_DIM + 3) & (lane < HEAD_DIM + 6)
    is_one_k = (lane >= HEAD_DIM) & (lane < HEAD_DIM + 3)
    for pair in range(H_FOX // 2):
        sl = slice(pair * LANES, (pair + 1) * LANES)
        q2 = bq[:, sl]
        q2 = (q2 * lax.rsqrt(_head_mean_sq(q2, ones) + EPS) * qg_ref[:, sl]
              * (HEAD_DIM ** -0.5 * LOG2E))
        k2 = bk[:, sl]
        k2 = k2 * lax.rsqrt(_head_mean_sq(k2, ones) + EPS) * kg_ref[:, sl]
        for odd in range(2):
            head = 2 * pair + odd
            slot = slice(head * LANES, (head + 1) * LANES)
            qh = pltpu.roll(q2, HEAD_DIM, 1) if odd else q2
            kh = pltpu.roll(k2, HEAD_DIM, 1) if odd else k2
            hi, mid, lo = _bf16_terms(
                jnp.broadcast_to(cum[:, head:head + 1], (tm, LANES)) * LOG2E)
            q_bias = jnp.where(lane == HEAD_DIM, hi, jnp.where(lane == HEAD_DIM + 1, mid, jnp.where(
                lane == HEAD_DIM + 2, lo, jnp.where(is_one_q, 1.0, 0.0))))
            k_bias = jnp.where(lane == HEAD_DIM + 3, -hi, jnp.where(lane == HEAD_DIM + 4, -mid, jnp.where(
                lane == HEAD_DIM + 5, -lo, jnp.where(is_one_k, 1.0, 0.0))))
            qt_ref[0, slot, :] = jnp.where(is_feature, qh, q_bias).T.astype(BF16)
            ka_ref[:, slot] = jnp.where(is_feature, kh, k_bias).astype(BF16)
    pa_ref[...] = proj(_A_MAIN).astype(BF16)
    za_ref[...] = proj(_A_GATE)

    v_t = proj(_B_V).T.astype(BF16)
    ones_row = jnp.where(lax.broadcasted_iota(jnp.int32, (BF16_SUBLANES, tm), 0) == 0,
                         1.0, 0.0).astype(BF16)
    for head in range(H_FOX):
        vt_ref[0, head * VT_ROWS:head * VT_ROWS + HEAD_DIM, :] = (
            v_t[head * HEAD_DIM:(head + 1) * HEAD_DIM, :])
        vt_ref[0, head * VT_ROWS + HEAD_DIM:(head + 1) * VT_ROWS, :] = ones_row
    gb_ref[...] = proj(_B_G).astype(BF16)

    cos = cos_ref[...]
    sin = sin_ref[...]
    low_half = (lax.broadcasted_iota(jnp.int32, cos.shape, 1) & (HEAD_DIM // 2)) == 0

    def rotary(y):
        swapped = jnp.where(low_half, pltpu.roll(y, D_RET - HEAD_DIM // 2, 1),
                            pltpu.roll(y, HEAD_DIM // 2, 1))
        return y * cos + swapped * sin

    pc_ref[:, 0:D_RET] = rotary(proj(_C_Q)).astype(BF16)
    pc_ref[:, D_RET:2 * D_RET] = (rotary(proj(_C_K)) * (HEAD_DIM ** -0.5)).astype(BF16)
    pc_ref[:, 2 * D_RET:4 * D_RET] = proj(_C_VG).astype(BF16)


def _inproj(x2, gpre, w, bf, qg, kg, cos, sin, tri, ones, seq_len):
    n = x2.shape[0]
    batch = n // seq_len
    tm = TM_PROJ
    tiles_per_seq = seq_len // tm
    const = lambda i: (0, 0)
    row = lambda i: (i, 0)
    seq_t = lambda i: (i // tiles_per_seq, 0, i % tiles_per_seq)
    return pl.pallas_call(
        functools.partial(_inproj_kernel, tiles_per_seq=tiles_per_seq),
        grid=(n // tm,),
        in_specs=[
            pl.BlockSpec((tm, D_MODEL), row),
            pl.BlockSpec((1, D_MODEL), const),
            pl.BlockSpec((D_MODEL, IN_COLS_PADDED), const),
            pl.BlockSpec((1, FGATE_PAD), const),
            pl.BlockSpec((1, D_FOX), const),
            pl.BlockSpec((1, D_FOX), const),
            pl.BlockSpec((tm, D_RET), lambda i: (i % tiles_per_seq, 0)),
            pl.BlockSpec((tm, D_RET), lambda i: (i % tiles_per_seq, 0)),
            pl.BlockSpec((tm, tm), const),
            pl.BlockSpec((LANES, LANES), const),
        ],
        out_specs=[
            pl.BlockSpec((tm, 3 * D_HGRN), row),
            pl.BlockSpec((tm, D_HGRN), row),
            pl.BlockSpec((1, H_FOX * LANES, tm), seq_t),
            pl.BlockSpec((tm, H_FOX * LANES), row),
            pl.BlockSpec((1, H_FOX * VT_ROWS, tm), seq_t),
            pl.BlockSpec((tm, D_FOX), row),
            pl.BlockSpec((tm, 4 * D_RET), row),
        ],
        out_shape=[
            jax.ShapeDtypeStruct((n, 3 * D_HGRN), BF16),
            jax.ShapeDtypeStruct((n, D_HGRN), F32),
            jax.ShapeDtypeStruct((batch, H_FOX * LANES, seq_len), BF16),
            jax.ShapeDtypeStruct((n, H_FOX * LANES), BF16),
            jax.ShapeDtypeStruct((batch, H_FOX * VT_ROWS, seq_len), BF16),
            jax.ShapeDtypeStruct((n, D_FOX), BF16),
            jax.ShapeDtypeStruct((n, 4 * D_RET), BF16),
        ],
        scratch_shapes=[pltpu.VMEM((1, FGATE_PAD), F32)],
        compiler_params=pltpu.CompilerParams(
            dimension_semantics=("arbitrary",), vmem_limit_bytes=VMEM_LIMIT),
        name="inproj",
    )(x2, gpre, w, bf, qg, kg, cos, sin, tri, ones)


_M_CUM, _M_AFTER, _M_DIAG, _M_MERGE0 = 0, 1, 2, 3
_MERGE_BLOCKS = (2 * SAFE_DIAG, 4 * SAFE_DIAG, 8 * SAFE_DIAG)
assert _MERGE_BLOCKS[-1] == SUB_HGRN


def _hgrn_matrices(tt):
    t = jnp.arange(tt)[:, None]
    u = jnp.arange(tt)[None, :]
    same = lambda m: (t // m) == (u // m)
    mats = [same(SUB_HGRN) & (u <= t), same(SUB_HGRN) & (u > t),
            same(SAFE_DIAG) & (u <= t) & (u > (t // SAFE_DIAG) * SAFE_DIAG)]
    for m in _MERGE_BLOCKS:
        mid = (t // m) * m + m // 2 - 1
        mats.append(same(m) & (((t > mid) & (u > mid) & (u <= t)) | ((t <= mid) & (u > t) & (u <= mid))))
    return jnp.stack(mats).astype(BF16)


def _hgrn_kernel(q_ref, v_ref, g_ref, z_ref, lb_ref, gain_ref, mats_ref, ones_ref, o_ref, st_ref,
                 logf_scr, k_scr, cum_scr, *, layer):
    @pl.when(pl.program_id(1) == 0)
    def _():
        st_ref[...] = jnp.zeros_like(st_ref)

    nb, tt = q_ref.shape[0], q_ref.shape[1]
    sub = SUB_HGRN

    lb = lb_ref[...]
    e = jnp.exp(lb - jnp.max(lb, axis=0, keepdims=True))
    p = e / jnp.sum(e, axis=0, keepdims=True)
    run = p[0:1]
    for i in range(1, layer + 1):
        run = run + p[i:i + 1]
    lower = run - p[0:1]

    lowest = None
    for b in range(nb):
        sg = _sigmoid(z_ref[b])
        f = lower + (1.0 - lower) * sg
        log_f = jnp.log(jnp.maximum(f, MIN_FORGET))
        cum = _dot_01_lhs(mats_ref[_M_CUM], log_f)
        logf_scr[b] = log_f
        k_scr[b] = (1.0 - lower) * (1.0 - sg)
        cum_scr[b] = cum
        low_b = jnp.min(cum)
        lowest = low_b if lowest is None else jnp.minimum(lowest, low_b)
    exact_fallback = lowest < -FAST_PATH_MAX_DECAY

    row = lax.broadcasted_iota(jnp.int32, (tt, tt), 0)
    col = lax.broadcasted_iota(jnp.int32, (tt, tt), 1)

    def diag_mask(m):
        in_block = row & (m - 1)
        dist = row - col
        return ((in_block - dist) | dist) >= 0

    def merge_mask(m):
        half = m // 2
        return (((row ^ col) & ~(m - 1)) | ((row & half) ^ half) | (col & half)) == 0

    first = _first_head_lanes((tt, LANES))
    rr = lax.broadcasted_iota(jnp.int32, (LANES, LANES), 0)
    cc = lax.broadcasted_iota(jnp.int32, (LANES, LANES), 1)
    same_head = ((rr ^ cc) & HEAD_DIM) == 0
    ones = ones_ref[...]

    def factors(c, exact):
        b, sl = c["b"], c["sl"]
        q = q_ref[b, :, sl].astype(F32)
        k = k_scr[b, :, sl]
        cum = cum_scr[b, :, sl]
        c["v"] = v_ref[b, :, sl]
        c["e_q"] = jnp.exp(cum)
        c["qt"] = (q * c["e_q"]).astype(BF16)
        if exact:
            log_f = logf_scr[b, :, sl]
            since = _dot_01_lhs(mats_ref[_M_DIAG], log_f)
            pieces = [((q * jnp.exp(since)).astype(BF16), (k * jnp.exp(-since)).astype(BF16),
                       diag_mask(SAFE_DIAG))]
            for idx, m in enumerate(_MERGE_BLOCKS):
                w = jnp.exp(_dot_01_lhs(mats_ref[_M_MERGE0 + idx], log_f))
                pieces.append(((q * w).astype(BF16), (k * w).astype(BF16), merge_mask(m)))
            c["k_state"] = (k * jnp.exp(_dot_01_lhs(mats_ref[_M_AFTER], log_f))).astype(BF16)
        else:
            c["k_state"] = (k * jnp.exp(-cum)).astype(BF16)
            pieces = [(c["qt"], c["k_state"], diag_mask(sub))]
        c["pieces"] = pieces

    def intra_scores(c):
        c["s"] = []
        for keep in (first, ~first):
            s = jnp.zeros((tt, tt), F32)
            for q_f, k_f, mask in c["pieces"]:
                s = jnp.where(mask, _dot_nt(jnp.where(keep, q_f, jnp.zeros_like(q_f)), k_f), s)
            c["s"].append(s.astype(BF16))

    def intra_values(c):
        c["o"] = jnp.where(first, _dot(c["s"][0], c["v"]), _dot(c["s"][1], c["v"]))

    def state_updates(c):
        c["upd"] = [jnp.where(same_head, _dot_tn(c["v"][j * sub:(j + 1) * sub],
                                                 c["k_state"][j * sub:(j + 1) * sub]), 0.0)
                    for j in range(tt // sub)]

    def state_chain(c, exact):
        st = st_ref[c["b"], c["pair"]]
        outs = []
        for j in range(tt // sub):
            r = slice(j * sub, (j + 1) * sub)
            outs.append(c["o"][r] + _dot_nt(c["qt"][r], st.astype(BF16)))
            decay = c["e_q"][(j + 1) * sub - 1:(j + 1) * sub, :]
            st = decay * st + c["upd"][j] if exact else decay * (st + c["upd"][j])
        st_ref[c["b"], c["pair"]] = st
        c["o"] = jnp.concatenate(outs, axis=0)

    def head_norm_and_gate(c):
        b, sl = c["b"], c["sl"]
        o = c["o"] * lax.rsqrt(_head_mean_sq(c["o"], ones) + EPS) * gain_ref[:, sl]
        g = g_ref[b, :, sl].astype(F32)
        o_ref[b, :, sl] = (o * (g * _sigmoid(g))).astype(BF16)

    def all_tiles(exact):
        tiles = [dict(b=b, pair=pair, sl=slice(pair * LANES, (pair + 1) * LANES))
                 for b in range(nb) for pair in range(H_HGRN // 2)]
        for stage in (functools.partial(factors, exact=exact), intra_scores, intra_values,
                      state_updates, functools.partial(state_chain, exact=exact),
                      head_norm_and_gate):
            for c in tiles:
                stage(c)

    pl.when(exact_fallback)(functools.partial(all_tiles, True))
    pl.when(jnp.logical_not(exact_fallback))(functools.partial(all_tiles, False))


def _hgrn(pa, za, lb, gain, mats, ones, batch, seq_len, layer):
    n = pa.shape[0]
    tt = TT_HGRN
    nb = NB_HGRN
    const = lambda b, t: (0, 0)
    tile_f32 = pltpu.VMEM((nb, tt, D_HGRN), F32)
    pa3 = pa.reshape(batch, seq_len, 3 * D_HGRN)
    blk = lambda j: pl.BlockSpec((nb, tt, D_HGRN), lambda b, t: (b, t, j))
    out = pl.pallas_call(
        functools.partial(_hgrn_kernel, layer=layer),
        grid=(batch // nb, seq_len // tt),
        in_specs=[
            blk(0), blk(1), blk(2), blk(0),
            pl.BlockSpec(lb.shape, const),
            pl.BlockSpec((1, D_HGRN), const),
            pl.BlockSpec(mats.shape, lambda b, t: (0, 0, 0)),
            pl.BlockSpec((LANES, LANES), const),
        ],
        out_specs=blk(0),
        out_shape=jax.ShapeDtypeStruct((batch, seq_len, D_HGRN), BF16),
        scratch_shapes=[pltpu.VMEM((nb, H_HGRN // 2, LANES, LANES), F32),
                        tile_f32, tile_f32, tile_f32],
        compiler_params=pltpu.CompilerParams(
            dimension_semantics=("arbitrary", "arbitrary"), vmem_limit_bytes=VMEM_LIMIT),
        name="hgrn",
    )(pa3, pa3, pa3, za.reshape(batch, seq_len, D_HGRN), lb, gain, mats, ones)
    return out.reshape(n, D_HGRN)


def _ret_kernel(q_ref, k_ref, v_ref, g_ref, dec_ref, qs_ref, ks_ref, cd_ref, gain_ref, ones_ref,
                o_ref, st_ref):
    @pl.when(pl.program_id(1) == 0)
    def _():
        st_ref[...] = jnp.zeros_like(st_ref)

    nb, c = q_ref.shape[0], q_ref.shape[1]
    first = _first_head_lanes((c, LANES))
    rr = lax.broadcasted_iota(jnp.int32, (LANES, LANES), 0)
    cc = lax.broadcasted_iota(jnp.int32, (LANES, LANES), 1)
    same_head = ((rr ^ cc) & HEAD_DIM) == 0
    ones = ones_ref[...]

    def scores(c):
        q, k = q_ref[c["b"], :, c["sl"]], k_ref[c["b"], :, c["sl"]]
        c["s"] = [(_dot_nt(jnp.where(keep, q, jnp.zeros_like(q)), k)
                   * dec_ref[2 * c["pair"] + hh]).astype(BF16)
                  for hh, keep in enumerate((first, ~first))]

    def values_and_state(c):
        b, pair, sl = c["b"], c["pair"], c["sl"]
        q, k, v = q_ref[b, :, sl], k_ref[b, :, sl], v_ref[b, :, sl]
        st = st_ref[b, pair]
        q_dec = (q.astype(F32) * qs_ref[:, sl]).astype(BF16)
        c["o"] = (jnp.where(first, _dot(c["s"][0], v), _dot(c["s"][1], v))
                  + _dot_nt(q_dec, st.astype(BF16)))
        k_dec = (k.astype(F32) * ks_ref[:, sl]).astype(BF16)
        st_ref[b, pair] = cd_ref[:, sl] * st + jnp.where(same_head, _dot_tn(v, k_dec), 0.0)

    def head_norm_and_gate(c):
        b, sl = c["b"], c["sl"]
        o = c["o"] * lax.rsqrt(_head_mean_sq(c["o"], ones) + EPS) * gain_ref[:, sl]
        g = g_ref[b, :, sl].astype(F32)
        o_ref[b, :, sl] = (o * (g * _sigmoid(g))).astype(BF16)

    tiles = [dict(b=b, pair=pair, sl=slice(pair * LANES, (pair + 1) * LANES))
             for b in range(nb) for pair in range(H_RET // 2)]
    for stage in (scores, values_and_state, head_norm_and_gate):
        for c in tiles:
            stage(c)


def _retention(pc, dec, qs, ks, cd, gain, ones, batch, seq_len):
    n = pc.shape[0]
    c = C_RET
    nb = NB_RET
    const2 = lambda b, t: (0, 0)
    pc3 = pc.reshape(batch, seq_len, 4 * D_RET)
    blk = lambda j: pl.BlockSpec((nb, c, D_RET), lambda b, t: (b, t, j))
    out = pl.pallas_call(
        _ret_kernel,
        grid=(batch // nb, seq_len // c),
        in_specs=[
            blk(0), blk(1), blk(2), blk(3),
            pl.BlockSpec((H_RET, c, c), lambda b, t: (0, 0, 0)),
            pl.BlockSpec((c, D_RET), const2),
            pl.BlockSpec((c, D_RET), const2),
            pl.BlockSpec((1, D_RET), const2),
            pl.BlockSpec((1, D_RET), const2),
            pl.BlockSpec((LANES, LANES), const2),
        ],
        out_specs=blk(0),
        out_shape=jax.ShapeDtypeStruct((batch, seq_len, D_RET), BF16),
        scratch_shapes=[pltpu.VMEM((nb, H_RET // 2, LANES, LANES), F32)],
        compiler_params=pltpu.CompilerParams(
            dimension_semantics=("arbitrary", "arbitrary"), vmem_limit_bytes=VMEM_LIMIT),
        name="retention",
    )(pc3, pc3, pc3, pc3, dec, qs, ks, cd, gain, ones)
    return out.reshape(n, D_RET)


def _fox_kernel(qt_ref, k_ref, vt_ref, g_ref, o_ref, s_scr, p_scr, acc_scr):
    tq = TQ_FOX
    nq = o_ref.shape[0] // tq
    blocks = [(qi, kj) for qi in range(nq) for kj in range(qi + 1)]
    key = lax.broadcasted_iota(jnp.int32, (tq, tq), 0)
    qry = lax.broadcasted_iota(jnp.int32, (tq, tq), 1)
    causal = qry >= key

    def scores(t):
        qi, kj = blocks[t]
        for hh in range(2):
            slot = slice(hh * LANES, (hh + 1) * LANES)
            s_scr[t % 2, hh] = _dot(k_ref[kj * tq:(kj + 1) * tq, slot],
                                    qt_ref[0, slot, qi * tq:(qi + 1) * tq])

    def softmax(t, m_old):
        qi, kj = blocks[t]
        m_new, alpha = [], []
        for hh in range(2):
            s = s_scr[t % 2, hh]
            if kj == qi:
                s = jnp.where(causal, s, MASK_VALUE)
            m_blk = jnp.max(s, axis=0, keepdims=True)
            if kj > 0:
                m_blk = jnp.maximum(m_old[hh], m_blk)
                alpha.append(jnp.exp2(m_old[hh] - m_blk))
            p_scr[t % 2, hh] = jnp.exp2(s - m_blk).astype(BF16)
            m_new.append(m_blk)
        return m_new, alpha

    def accumulate(t, alpha):
        qi, kj = blocks[t]
        for hh in range(2):
            pv = _dot(vt_ref[0, hh * VT_ROWS:(hh + 1) * VT_ROWS, kj * tq:(kj + 1) * tq],
                      p_scr[t % 2, hh])
            acc_scr[qi % 2, hh] = pv if kj == 0 else alpha[hh] * acc_scr[qi % 2, hh] + pv
        if kj == qi:
            heads = []
            for hh in range(2):
                total = acc_scr[qi % 2, hh]
                heads.append(total[:HEAD_DIM] / total[HEAD_DIM:HEAD_DIM + 1])
            o_t = jnp.concatenate(heads, axis=0)
            rows = slice(qi * tq, (qi + 1) * tq)
            o_ref[rows, :] = (o_t.T * _sigmoid(g_ref[rows, :].astype(F32))).astype(BF16)

    scores(0)
    m_run, alpha_prev = None, None
    for t in range(len(blocks)):
        m_run, alpha_t = softmax(t, m_run)
        if t + 1 < len(blocks):
            scores(t + 1)
        if t > 0:
            accumulate(t - 1, alpha_prev)
        alpha_prev = alpha_t
    accumulate(len(blocks) - 1, alpha_prev)


def _fox(qt, ka, vt, gb, batch, seq_len):
    n = ka.shape[0]
    tq = TQ_FOX
    pairs = H_FOX // 2
    return pl.pallas_call(
        _fox_kernel,
        grid=(batch, pairs),
        in_specs=[
            pl.BlockSpec((1, 2 * LANES, seq_len), lambda b, p: (b, p, 0)),
            pl.BlockSpec((seq_len, 2 * LANES), lambda b, p: (b, p)),
            pl.BlockSpec((1, 2 * VT_ROWS, seq_len), lambda b, p: (b, p, 0)),
            pl.BlockSpec((seq_len, LANES), lambda b, p: (b, p)),
        ],
        out_specs=pl.BlockSpec((seq_len, LANES), lambda b, p: (b, p)),
        out_shape=jax.ShapeDtypeStruct((n, D_FOX), BF16),
        scratch_shapes=[
            pltpu.VMEM((2, 2, tq, tq), F32),
            pltpu.VMEM((2, 2, tq, tq), BF16),
            pltpu.VMEM((2, 2, VT_ROWS, tq), F32),
        ],
        compiler_params=pltpu.CompilerParams(
            dimension_semantics=("arbitrary", "arbitrary"), vmem_limit_bytes=VMEM_LIMIT),
        name="fox",
    )(qt, ka, vt, gb)


def _mix_ffn_kernel(oa_ref, ob_ref, oc_ref, wo_ref, gmix_ref, x_ref, gpre_ref,
                    wup_ref, cw_ref, cb_ref, wd_ref, gpost_ref, out_ref, xp_scr, act_scr, tail_scr,
                    *, tiles_per_seq):
    i = pl.program_id(0)
    tm = x_ref.shape[0]
    fc = FC_FFN

    mixer_out = jnp.concatenate([oa_ref[...], ob_ref[...], oc_ref[...]], axis=1)
    x_mid = x_ref[...] + _rmsnorm_rows(_dot(mixer_out, wo_ref[...]), gmix_ref[...])
    xp_scr[...] = pltpu.einshape("(ab)f->(ba)f", x_mid, a=SUBLANES)
    h = _rmsnorm_rows(xp_scr[...], gpre_ref[...]).astype(BF16)

    @pl.when(i == 0)
    def _():
        tail_scr[...] = jnp.zeros_like(tail_scr)

    seq_start = i % tiles_per_seq == 0
    first_sublane = lax.broadcasted_iota(jnp.int32, (SUBLANES, fc), 0) == 0

    def causal_conv(u, col, slot):
        prev = jnp.where(seq_start, 0.0, tail_scr[slot])
        last = u[tm - 2 * SUBLANES:, :]
        tail_scr[slot] = last
        head = []
        for g in range(2):
            rows = slice(g * SUBLANES, (g + 1) * SUBLANES)
            entering = prev[(g + 1) * SUBLANES - 1:(g + 1) * SUBLANES, :]
            head.append(jnp.where(first_sublane, entering, pltpu.roll(last[rows], 1, 0)))
        u1 = jnp.concatenate([head[1], u[:tm - SUBLANES]], axis=0)
        u2 = jnp.concatenate([head[0], head[1], u[:tm - 2 * SUBLANES]], axis=0)
        cols = slice(col, col + fc)
        return (cb_ref[:, cols] + cw_ref[0:1, cols] * u2 + cw_ref[1:2, cols] * u1
                + cw_ref[2:3, cols] * u)

    for c in range(D_FF // fc):
        gate = causal_conv(_dot(h, wup_ref[:, c * fc:(c + 1) * fc]), c * fc, 2 * c)
        val = causal_conv(_dot(h, wup_ref[:, D_FF + c * fc:D_FF + (c + 1) * fc]),
                          D_FF + c * fc, 2 * c + 1)
        act_scr[:, c * fc:(c + 1) * fc] = (gate * _sigmoid(gate) * val).astype(BF16)

    y = _dot(act_scr[...], wd_ref[...])
    res = xp_scr[...] + _rmsnorm_rows(y, gpost_ref[...])
    out_ref[...] = pltpu.einshape("(ba)f->(ab)f", res, a=SUBLANES)


def _mix_ffn(oa, ob, oc, wo, gmix, x2, gpre, w_up, conv_w, conv_b, w_down, gpost, seq_len):
    n = x2.shape[0]
    tm = TM_FFN
    nf = D_FF // FC_FFN
    tiles_per_seq = seq_len // tm
    row = lambda i: (i, 0)
    resident = lambda shape: pl.BlockSpec(shape, lambda i: (0, 0), pipeline_mode=pl.Buffered(1))
    return pl.pallas_call(
        functools.partial(_mix_ffn_kernel, tiles_per_seq=tiles_per_seq),
        grid=(n // tm,),
        in_specs=[
            pl.BlockSpec((tm, D_HGRN), row),
            pl.BlockSpec((tm, D_FOX), row),
            pl.BlockSpec((tm, D_RET), row),
            resident((D_HGRN + D_FOX + D_RET, D_MODEL)),
            resident((1, D_MODEL)),
            pl.BlockSpec((tm, D_MODEL), row),
            resident((1, D_MODEL)),
            resident((D_MODEL, 2 * D_FF)),
            resident((CONV_W, 2 * D_FF)),
            resident((1, 2 * D_FF)),
            resident((D_FF, D_MODEL)),
            resident((1, D_MODEL)),
        ],
        out_specs=pl.BlockSpec((tm, D_MODEL), row),
        out_shape=jax.ShapeDtypeStruct((n, D_MODEL), F32),
        scratch_shapes=[
            pltpu.VMEM((tm, D_MODEL), F32),
            pltpu.VMEM((tm, D_FF), BF16),
            pltpu.VMEM((2 * nf, 2 * SUBLANES, FC_FFN), F32),
        ],
        compiler_params=pltpu.CompilerParams(
            dimension_semantics=("arbitrary",), vmem_limit_bytes=VMEM_LIMIT),
        name="mix_ffn",
    )(oa, ob, oc, wo, gmix, x2, gpre, w_up, conv_w, conv_b, w_down, gpost)


def _block_ones(n):
    idx = jnp.arange(n) // HEAD_DIM
    return (idx[:, None] == idx[None, :]).astype(BF16)


def _lower_tri(n, block):
    r = jnp.arange(n)
    return ((r[:, None] >= r[None, :]) & (r[:, None] // block == r[None, :] // block)).astype(BF16)


def _rotary_tables(seq_len):
    inv_freq = 1.0 / (ROPE_BASE ** (jnp.arange(0, HEAD_DIM, 2, dtype=F32) / HEAD_DIM))
    ang = jnp.arange(seq_len, dtype=F32)[:, None] * inv_freq[None, :]
    cos, sin = jnp.cos(ang), jnp.sin(ang)
    cos_full = jnp.tile(jnp.concatenate([cos, cos], axis=-1), (1, H_RET))
    sin_signed = jnp.tile(jnp.concatenate([-sin, sin], axis=-1), (1, H_RET))
    return cos_full, sin_signed


def _retention_tables(c):
    log_gamma = jnp.log1p(-jnp.exp2(-5.0 - jnp.arange(H_RET, dtype=F32)))
    pos = jnp.arange(c, dtype=F32)
    rel = pos[:, None] - pos[None, :]
    dec = jnp.where(rel >= 0, jnp.exp(log_gamma[:, None, None] * jnp.maximum(rel, 0.0)), 0.0)
    per_lane = lambda a: jnp.repeat(a, HEAD_DIM, axis=-1)
    qs = per_lane(jnp.exp(log_gamma[None, :] * (pos[:, None] + 1.0)))
    ks = per_lane(jnp.exp(log_gamma[None, :] * (c - 1.0 - pos[:, None])))
    cd = per_lane(jnp.exp(log_gamma * c)[None, :])
    return dec, qs, ks, cd


def _reorder_in_weight_rows(w_t):
    widths = [D_HGRN] * 4 + [D_FOX] * 3 + [H_FOX] + [D_FOX] + [D_RET] * 4
    offs = [0]
    for wd in widths:
        offs.append(offs[-1] + wd)
    part = lambda j: w_t[offs[j]:offs[j + 1]]
    a_q, a_f, a_i, a_g, b_q, b_k, b_v, b_f, b_g, c_q, c_k, c_v, c_g = (part(j) for j in range(13))
    b_f = jnp.pad(b_f, ((0, FGATE_PAD - H_FOX), (0, 0)))
    return jnp.concatenate(
        [a_q, a_i, a_g, a_f, b_q, b_k, b_v, b_g, b_f, c_q, c_k, c_v, c_g], axis=0)


def _transpose_cast_kernel(w_ref, o_ref):
    o_ref[...] = w_ref[...].T.astype(BF16)


def _transpose_cast(w_t):
    rows = w_t.shape[0]
    blk = D_HGRN
    return pl.pallas_call(
        _transpose_cast_kernel,
        grid=(rows // blk,),
        in_specs=[pl.BlockSpec((blk, D_MODEL), lambda i: (i, 0))],
        out_specs=pl.BlockSpec((D_MODEL, blk), lambda i: (0, i)),
        out_shape=jax.ShapeDtypeStruct((D_MODEL, rows), BF16),
        compiler_params=pltpu.CompilerParams(
            dimension_semantics=("arbitrary",), vmem_limit_bytes=VMEM_LIMIT),
        name="wprep",
    )(w_t)


def kernel(x, w_in, b_fox_f, fox_q_gain, fox_k_gain, hgrn_lb, hgrn_out_gain, ret_out_gain, w_out,
           g_mix_pre, g_mix_post, w_up, conv_w, conv_b, w_down, g_ffn_pre, g_ffn_post):
    batch, seq_len, d_model = x.shape
    assert d_model == D_MODEL
    assert seq_len % TM_FFN == 0 and seq_len % TQ_FOX == 0
    assert batch % NB_HGRN == 0 and batch % NB_RET == 0
    depth = w_in.shape[0]
    n = batch * seq_len

    cos_full, sin_signed = _rotary_tables(seq_len)
    dec, qs, ks, cd = _retention_tables(C_RET)
    tri_proj = _lower_tri(TM_PROJ, TM_PROJ)
    mats_hgrn = _hgrn_matrices(TT_HGRN)
    ones_pair = _block_ones(LANES)
    lb = hgrn_lb.astype(F32)

    w_in_t = jnp.transpose(w_in, (2, 0, 1))

    x2 = x.reshape(n, D_MODEL)
    for l in range(depth):
        w = _transpose_cast(_reorder_in_weight_rows(w_in_t[:, l, :]))
        bf = jnp.pad(b_fox_f[l].astype(F32), (0, FGATE_PAD - H_FOX))[None, :]
        qg = jnp.tile(fox_q_gain[l].astype(F32), H_FOX)[None, :]
        kg = jnp.tile(fox_k_gain[l].astype(F32), H_FOX)[None, :]
        pa, za, qt, ka, vt, gb, pc = _inproj(x2, g_mix_pre[l][None, :], w, bf, qg, kg, cos_full,
                                             sin_signed, tri_proj, ones_pair, seq_len)

        o_a = _hgrn(pa, za, lb, hgrn_out_gain[l].reshape(1, D_HGRN), mats_hgrn, ones_pair,
                    batch, seq_len, l)
        o_b = _fox(qt, ka, vt, gb, batch, seq_len)
        o_c = _retention(pc, dec, qs, ks, cd, ret_out_gain[l].reshape(1, D_RET), ones_pair,
                         batch, seq_len)

        x2 = _mix_ffn(o_a, o_b, o_c, w_out[l].astype(BF16),
                      g_mix_post[l][None, :], x2, g_ffn_pre[l][None, :], w_up[l].astype(BF16),
                      conv_w[l], conv_b[l][None, :], w_down[l].astype(BF16),
                      g_ffn_post[l][None, :], seq_len)
    return x2.reshape(batch, seq_len, D_MODEL)
```

```python
import functools

import jax
import jax.numpy as jnp
from jax import lax
from jax.experimental import pallas as pl
from jax.experimental.pallas import tpu as pltpu

F32 = jnp.float32
BF16 = jnp.bfloat16

D_MODEL = 1024
HEAD_DIM = 64
H_HGRN, H_FOX, H_RET = 6, 6, 4
D_HGRN, D_FOX, D_RET = H_HGRN * HEAD_DIM, H_FOX * HEAD_DIM, H_RET * HEAD_DIM
D_FF = 2816
CONV_W = 3
ROPE_BASE = 10000.0
EPS = 1e-6
MIN_FORGET = 1e-12
MASK_VALUE = -1e30

LANES = 128
SUBLANES = 8
FGATE_PAD = LANES
BF16_SUBLANES = 16
VT_ROWS = HEAD_DIM + BF16_SUBLANES
LOG2E = 1.4426950408889634

_A_MAIN = (0, 3 * D_HGRN)
_A_GATE = (_A_MAIN[1], _A_MAIN[1] + D_HGRN)
_B_Q = (_A_GATE[1], _A_GATE[1] + D_FOX)
_B_K = (_B_Q[1], _B_Q[1] + D_FOX)
_B_V = (_B_K[1], _B_K[1] + D_FOX)
_B_G = (_B_V[1], _B_V[1] + D_FOX)
_B_F = (_B_G[1], _B_G[1] + FGATE_PAD)
_C_Q = (_B_F[1], _B_F[1] + D_RET)
_C_K = (_C_Q[1], _C_Q[1] + D_RET)
_C_VG = (_C_K[1], _C_K[1] + 2 * D_RET)
IN_COLS_PADDED = _C_VG[1]

TM_PROJ = 512
CUM_BLOCK = 128
TM_FFN = 1024
FC_FFN = 256
TT_HGRN = 256
NB_HGRN = 4
SUB_HGRN = 32
MAT_BLOCK = 128
SAFE_DIAG = 4
FAST_PATH_MAX_DECAY = 60.0
C_RET = 256
NB_RET = 4
TQ_FOX = 512
VMEM_LIMIT = 56 * 1024 * 1024


def _dot(a, b):
    return jnp.dot(a, b, preferred_element_type=F32)


def _dot_nt(a, b):
    return lax.dot_general(a, b, (((1,), (1,)), ((), ())), preferred_element_type=F32)


def _dot_tn(a, b):
    return lax.dot_general(a, b, (((0,), (0,)), ((), ())), preferred_element_type=F32)


def _dot_01_lhs(m01, x):
    hi = x.astype(BF16)
    r = x - hi.astype(F32)
    mid = r.astype(BF16)
    lo = (r - mid.astype(F32)).astype(BF16)
    return _dot(m01, hi) + _dot(m01, mid) + _dot(m01, lo)


def _head_mean_sq(y, ones_bd):
    sq = y * y
    hi = sq.astype(BF16)
    lo = (sq - hi.astype(F32)).astype(BF16)
    return (_dot(hi, ones_bd) + _dot(lo, ones_bd)) * (1.0 / HEAD_DIM)


def _rmsnorm_rows(x, gain):
    return x * lax.rsqrt(jnp.mean(x * x, axis=-1, keepdims=True) + EPS) * gain


def _sigmoid(x):
    return 1.0 / (1.0 + jnp.exp(-x))


def _first_head_lanes(shape):
    return (lax.broadcasted_iota(jnp.int32, shape, len(shape) - 1) & HEAD_DIM) == 0


def _bf16_terms(x):
    hi = x.astype(BF16).astype(F32)
    mid = (x - hi).astype(BF16).astype(F32)
    lo = (x - hi - mid).astype(BF16).astype(F32)
    return hi, mid, lo


def _inproj_kernel(x_ref, gpre_ref, w_ref, bf_ref, qg_ref, kg_ref, cos_ref, sin_ref, tri_ref,
                   ones_ref, pa_ref, za_ref, qt_ref, ka_ref, vt_ref, gb_ref, pc_ref, carry_ref,
                   *, tiles_per_seq):
    @pl.when(pl.program_id(0) % tiles_per_seq == 0)
    def _():
        carry_ref[...] = jnp.zeros_like(carry_ref)

    h = _rmsnorm_rows(x_ref[...], gpre_ref[...]).astype(BF16)
    tm = h.shape[0]

    group_ends = (_A_GATE[1], _B_K[1], _B_G[1], IN_COLS_PADDED)
    group_result = {}

    def proj(cols):
        start = 0
        for end in group_ends:
            if cols[1] <= end:
                break
            start = end
        if start not in group_result:
            group_result[start] = _dot(h, w_ref[:, start:end])
        return group_result[start][:, cols[0] - start:cols[1] - start]

    zf = proj(_B_F) + bf_ref[...]
    log_f = -(jnp.maximum(-zf, 0.0) + jnp.log1p(jnp.exp(-jnp.abs(zf))))
    running = carry_ref[...]
    blocks = []
    for r in range(0, tm, CUM_BLOCK):
        blocks.append(_dot_01_lhs(tri_ref[...], log_f[r:r + CUM_BLOCK]) + running)
        running = blocks[-1][CUM_BLOCK - 1:, :]
    cum = jnp.concatenate(blocks, axis=0)
    carry_ref[...] = running

    ones = ones_ref[...]
    bq = proj(_B_Q)
    bk = proj(_B_K)
    lane = lax.broadcasted_iota(jnp.int32, (tm, LANES), 1)
    is_feature = lane < HEAD_DIM
    is_one_q = (lane >= HEAD_DIM + 3) & (lane < HEAD_DIM + 6)
    is_one_k = (lane >= HEAD_DIM) & (lane < HEAD_DIM + 3)
    for pair in range(H_FOX // 2):
        sl = slice(pair * LANES, (pair + 1) * LANES)
        q2 = bq[:, sl]
        q2 = (q2 * lax.rsqrt(_head_mean_sq(q2, ones) + EPS) * qg_ref[:, sl]
              * (HEAD_DIM ** -0.5 * LOG2E))
        k2 = bk[:, sl]
        k2 = k2 * lax.rsqrt(_head_mean_sq(k2, ones) + EPS) * kg_ref[:, sl]
        for odd in range(2):
            head = 2 * pair + odd
            slot = slice(head * LANES, (head + 1) * LANES)
            qh = pltpu.roll(q2, HEAD_DIM, 1) if odd else q2
            kh = pltpu.roll(k2, HEAD_DIM, 1) if odd else k2
            hi, mid, lo = _bf16_terms(
                jnp.broadcast_to(cum[:, head:head + 1], (tm, LANES)) * LOG2E)
            q_bias = jnp.where(lane == HEAD_DIM, hi, jnp.where(lane == HEAD_DIM + 1, mid, jnp.where(
                lane == HEAD_DIM + 2, lo, jnp.where(is_one_q, 1.0, 0.0))))
            k_bias = jnp.where(lane == HEAD_DIM + 3, -hi, jnp.where(lane == HEAD_DIM + 4, -mid, jnp.where(
                lane == HEAD_DIM + 5, -lo, jnp.where(is_one_k, 1.0, 0.0))))
            qt_ref[0, slot, :] = jnp.where(is_feature, qh, q_bias).T.astype(BF16)
            ka_ref[:, slot] = jnp.where(is_feature, kh, k_bias).astype(BF16)
    pa_ref[...] = proj(_A_MAIN).astype(BF16)
    za_ref[...] = proj(_A_GATE)

    v_t = proj(_B_V).T.astype(BF16)
    ones_row = jnp.where(lax.broadcasted_iota(jnp.int32, (BF16_SUBLANES, tm), 0) == 0,
                         1.0, 0.0).astype(BF16)
    for head in range(H_FOX):
        vt_ref[0, head * VT_ROWS:head * VT_ROWS + HEAD_DIM, :] = (
            v_t[head * HEAD_DIM:(head + 1) * HEAD_DIM, :])
        vt_ref[0, head * VT_ROWS + HEAD_DIM:(head + 1) * VT_ROWS, :] = ones_row
    gb_ref[...] = proj(_B_G).astype(BF16)

    cos = cos_ref[...]
    sin = sin_ref[...]
    low_half = (lax.broadcasted_iota(jnp.int32, cos.shape, 1) & (HEAD_DIM // 2)) == 0

    def rotary(y):
        swapped = jnp.where(low_half, pltpu.roll(y, D_RET - HEAD_DIM // 2, 1),
                            pltpu.roll(y, HEAD_DIM // 2, 1))
        return y * cos + swapped * sin

    pc_ref[:, 0:D_RET] = rotary(proj(_C_Q)).astype(BF16)
    pc_ref[:, D_RET:2 * D_RET] = (rotary(proj(_C_K)) * (HEAD_DIM ** -0.5)).astype(BF16)
    pc_ref[:, 2 * D_RET:4 * D_RET] = proj(_C_VG).astype(BF16)


def _inproj(x2, gpre, w, bf, qg, kg, cos, sin, tri, ones, seq_len):
    n = x2.shape[0]
    batch = n // seq_len
    tm = TM_PROJ
    tiles_per_seq = seq_len // tm
    const = lambda i: (0, 0)
    row = lambda i: (i, 0)
    seq_t = lambda i: (i // tiles_per_seq, 0, i % tiles_per_seq)
    return pl.pallas_call(
        functools.partial(_inproj_kernel, tiles_per_seq=tiles_per_seq),
        grid=(n // tm,),
        in_specs=[
            pl.BlockSpec((tm, D_MODEL), row),
            pl.BlockSpec((1, D_MODEL), const),
            pl.BlockSpec((D_MODEL, IN_COLS_PADDED), const),
            pl.BlockSpec((1, FGATE_PAD), const),
            pl.BlockSpec((1, D_FOX), const),
            pl.BlockSpec((1, D_FOX), const),
            pl.BlockSpec((tm, D_RET), lambda i: (i % tiles_per_seq, 0)),
            pl.BlockSpec((tm, D_RET), lambda i: (i % tiles_per_seq, 0)),
            pl.BlockSpec((CUM_BLOCK, CUM_BLOCK), const),
            pl.BlockSpec((LANES, LANES), const),
        ],
        out_specs=[
            pl.BlockSpec((tm, 3 * D_HGRN), row),
            pl.BlockSpec((tm, D_HGRN), row),
            pl.BlockSpec((1, H_FOX * LANES, tm), seq_t),
            pl.BlockSpec((tm, H_FOX * LANES), row),
            pl.BlockSpec((1, H_FOX * VT_ROWS, tm), seq_t),
            pl.BlockSpec((tm, D_FOX), row),
            pl.BlockSpec((tm, 4 * D_RET), row),
        ],
        out_shape=[
            jax.ShapeDtypeStruct((n, 3 * D_HGRN), BF16),
            jax.ShapeDtypeStruct((n, D_HGRN), F32),
            jax.ShapeDtypeStruct((batch, H_FOX * LANES, seq_len), BF16),
            jax.ShapeDtypeStruct((n, H_FOX * LANES), BF16),
            jax.ShapeDtypeStruct((batch, H_FOX * VT_ROWS, seq_len), BF16),
            jax.ShapeDtypeStruct((n, D_FOX), BF16),
            jax.ShapeDtypeStruct((n, 4 * D_RET), BF16),
        ],
        scratch_shapes=[pltpu.VMEM((1, FGATE_PAD), F32)],
        compiler_params=pltpu.CompilerParams(
            dimension_semantics=("arbitrary",), vmem_limit_bytes=VMEM_LIMIT),
        name="inproj",
    )(x2, gpre, w, bf, qg, kg, cos, sin, tri, ones)


_M_CUM, _M_AFTER, _M_DIAG, _M_MERGE0 = 0, 1, 2, 3
_MERGE_BLOCKS = (2 * SAFE_DIAG, 4 * SAFE_DIAG, 8 * SAFE_DIAG)
assert _MERGE_BLOCKS[-1] == SUB_HGRN


def _hgrn_matrices(tt):
    t = jnp.arange(tt)[:, None]
    u = jnp.arange(tt)[None, :]
    same = lambda m: (t // m) == (u // m)
    mats = [same(SUB_HGRN) & (u <= t), same(SUB_HGRN) & (u > t),
            same(SAFE_DIAG) & (u <= t) & (u > (t // SAFE_DIAG) * SAFE_DIAG)]
    for m in _MERGE_BLOCKS:
        mid = (t // m) * m + m // 2 - 1
        mats.append(same(m) & (((t > mid) & (u > mid) & (u <= t)) | ((t <= mid) & (u > t) & (u <= mid))))
    return jnp.stack(mats).astype(BF16)


def _hgrn_kernel(q_ref, v_ref, g_ref, z_ref, lb_ref, gain_ref, mats_ref, ones_ref, o_ref, st_ref,
                 logf_scr, k_scr, cum_scr, *, layer):
    @pl.when(pl.program_id(1) == 0)
    def _():
        st_ref[...] = jnp.zeros_like(st_ref)

    nb, tt = q_ref.shape[0], q_ref.shape[1]
    sub = SUB_HGRN

    def range_sum(which, log_f):
        return jnp.concatenate([_dot_01_lhs(mats_ref[which], log_f[r:r + MAT_BLOCK])
                                for r in range(0, tt, MAT_BLOCK)], axis=0)

    lb = lb_ref[...]
    e = jnp.exp(lb - jnp.max(lb, axis=0, keepdims=True))
    p = e / jnp.sum(e, axis=0, keepdims=True)
    run = p[0:1]
    for i in range(1, layer + 1):
        run = run + p[i:i + 1]
    lower = run - p[0:1]

    lowest = None
    for b in range(nb):
        sg = _sigmoid(z_ref[b])
        f = lower + (1.0 - lower) * sg
        log_f = jnp.log(jnp.maximum(f, MIN_FORGET))
        cum = range_sum(_M_CUM, log_f)
        logf_scr[b] = log_f
        k_scr[b] = (1.0 - lower) * (1.0 - sg)
        cum_scr[b] = cum
        low_b = jnp.min(cum)
        lowest = low_b if lowest is None else jnp.minimum(lowest, low_b)
    exact_fallback = lowest < -FAST_PATH_MAX_DECAY

    row = lax.broadcasted_iota(jnp.int32, (tt, tt), 0)
    col = lax.broadcasted_iota(jnp.int32, (tt, tt), 1)

    def diag_mask(m):
        in_block = row & (m - 1)
        dist = row - col
        return ((in_block - dist) | dist) >= 0

    def merge_mask(m):
        half = m // 2
        return (((row ^ col) & ~(m - 1)) | ((row & half) ^ half) | (col & half)) == 0

    first = _first_head_lanes((tt, LANES))
    rr = lax.broadcasted_iota(jnp.int32, (LANES, LANES), 0)
    cc = lax.broadcasted_iota(jnp.int32, (LANES, LANES), 1)
    same_head = ((rr ^ cc) & HEAD_DIM) == 0
    ones = ones_ref[...]

    def factors(c, exact):
        b, sl = c["b"], c["sl"]
        q = q_ref[b, :, sl].astype(F32)
        k = k_scr[b, :, sl]
        cum = cum_scr[b, :, sl]
        c["v"] = v_ref[b, :, sl]
        c["e_q"] = jnp.exp(cum)
        c["qt"] = (q * c["e_q"]).astype(BF16)
        if exact:
            log_f = logf_scr[b, :, sl]
            since = range_sum(_M_DIAG, log_f)
            pieces = [((q * jnp.exp(since)).astype(BF16), (k * jnp.exp(-since)).astype(BF16),
                       diag_mask(SAFE_DIAG))]
            for idx, m in enumerate(_MERGE_BLOCKS):
                w = jnp.exp(range_sum(_M_MERGE0 + idx, log_f))
                pieces.append(((q * w).astype(BF16), (k * w).astype(BF16), merge_mask(m)))
            c["k_state"] = (k * jnp.exp(range_sum(_M_AFTER, log_f))).astype(BF16)
        else:
            c["k_state"] = (k * jnp.exp(-cum)).astype(BF16)
            pieces = [(c["qt"], c["k_state"], diag_mask(sub))]
        c["pieces"] = pieces

    def intra_scores(c):
        c["s"] = []
        for keep in (first, ~first):
            s = jnp.zeros((tt, tt), F32)
            for q_f, k_f, mask in c["pieces"]:
                s = jnp.where(mask, _dot_nt(jnp.where(keep, q_f, jnp.zeros_like(q_f)), k_f), s)
            c["s"].append(s.astype(BF16))

    def intra_values(c):
        c["o"] = jnp.where(first, _dot(c["s"][0], c["v"]), _dot(c["s"][1], c["v"]))

    def state_updates(c):
        c["upd"] = [jnp.where(same_head, _dot_tn(c["v"][j * sub:(j + 1) * sub],
                                                 c["k_state"][j * sub:(j + 1) * sub]), 0.0)
                    for j in range(tt // sub)]

    def state_chain(c, exact):
        st = st_ref[c["b"], c["pair"]]
        outs = []
        for j in range(tt // sub):
            r = slice(j * sub, (j + 1) * sub)
            outs.append(c["o"][r] + _dot_nt(c["qt"][r], st.astype(BF16)))
            decay = c["e_q"][(j + 1) * sub - 1:(j + 1) * sub, :]
            st = decay * st + c["upd"][j] if exact else decay * (st + c["upd"][j])
        st_ref[c["b"], c["pair"]] = st
        c["o"] = jnp.concatenate(outs, axis=0)

    def head_norm_and_gate(c):
        b, sl = c["b"], c["sl"]
        o = c["o"] * lax.rsqrt(_head_mean_sq(c["o"], ones) + EPS) * gain_ref[:, sl]
        g = g_ref[b, :, sl].astype(F32)
        o_ref[b, :, sl] = (o * (g * _sigmoid(g))).astype(BF16)

    def all_tiles(exact):
        tiles = [dict(b=b, pair=pair, sl=slice(pair * LANES, (pair + 1) * LANES))
                 for b in range(nb) for pair in range(H_HGRN // 2)]
        for stage in (functools.partial(factors, exact=exact), intra_scores, intra_values,
                      state_updates, functools.partial(state_chain, exact=exact),
                      head_norm_and_gate):
            for c in tiles:
                stage(c)

    pl.when(exact_fallback)(functools.partial(all_tiles, True))
    pl.when(jnp.logical_not(exact_fallback))(functools.partial(all_tiles, False))


def _hgrn(pa, za, lb, gain, mats, ones, batch, seq_len, layer):
    n = pa.shape[0]
    tt = TT_HGRN
    nb = NB_HGRN
    const = lambda b, t: (0, 0)
    tile_f32 = pltpu.VMEM((nb, tt, D_HGRN), F32)
    pa3 = pa.reshape(batch, seq_len, 3 * D_HGRN)
    blk = lambda j: pl.BlockSpec((nb, tt, D_HGRN), lambda b, t: (b, t, j))
    out = pl.pallas_call(
        functools.partial(_hgrn_kernel, layer=layer),
        grid=(batch // nb, seq_len // tt),
        in_specs=[
            blk(0), blk(1), blk(2), blk(0),
            pl.BlockSpec(lb.shape, const),
            pl.BlockSpec((1, D_HGRN), const),
            pl.BlockSpec(mats.shape, lambda b, t: (0, 0, 0)),
            pl.BlockSpec((LANES, LANES), const),
        ],
        out_specs=blk(0),
        out_shape=jax.ShapeDtypeStruct((batch, seq_len, D_HGRN), BF16),
        scratch_shapes=[pltpu.VMEM((nb, H_HGRN // 2, LANES, LANES), F32),
                        tile_f32, tile_f32, tile_f32],
        compiler_params=pltpu.CompilerParams(
            dimension_semantics=("arbitrary", "arbitrary"), vmem_limit_bytes=VMEM_LIMIT),
        name="hgrn",
    )(pa3, pa3, pa3, za.reshape(batch, seq_len, D_HGRN), lb, gain, mats, ones)
    return out.reshape(n, D_HGRN)


def _ret_kernel(q_ref, k_ref, v_ref, g_ref, dec_ref, qs_ref, ks_ref, cd_ref, gain_ref, ones_ref,
                o_ref, st_ref):
    @pl.when(pl.program_id(1) == 0)
    def _():
        st_ref[...] = jnp.zeros_like(st_ref)

    nb, c = q_ref.shape[0], q_ref.shape[1]
    first = _first_head_lanes((c, LANES))
    rr = lax.broadcasted_iota(jnp.int32, (LANES, LANES), 0)
    cc = lax.broadcasted_iota(jnp.int32, (LANES, LANES), 1)
    same_head = ((rr ^ cc) & HEAD_DIM) == 0
    ones = ones_ref[...]

    def scores(c):
        q, k = q_ref[c["b"], :, c["sl"]], k_ref[c["b"], :, c["sl"]]
        c["s"] = [(_dot_nt(jnp.where(keep, q, jnp.zeros_like(q)), k)
                   * dec_ref[2 * c["pair"] + hh]).astype(BF16)
                  for hh, keep in enumerate((first, ~first))]

    def values_and_state(c):
        b, pair, sl = c["b"], c["pair"], c["sl"]
        q, k, v = q_ref[b, :, sl], k_ref[b, :, sl], v_ref[b, :, sl]
        st = st_ref[b, pair]
        q_dec = (q.astype(F32) * qs_ref[:, sl]).astype(BF16)
        c["o"] = (jnp.where(first, _dot(c["s"][0], v), _dot(c["s"][1], v))
                  + _dot_nt(q_dec, st.astype(BF16)))
        k_dec = (k.astype(F32) * ks_ref[:, sl]).astype(BF16)
        st_ref[b, pair] = cd_ref[:, sl] * st + jnp.where(same_head, _dot_tn(v, k_dec), 0.0)

    def head_norm_and_gate(c):
        b, sl = c["b"], c["sl"]
        o = c["o"] * lax.rsqrt(_head_mean_sq(c["o"], ones) + EPS) * gain_ref[:, sl]
        g = g_ref[b, :, sl].astype(F32)
        o_ref[b, :, sl] = (o * (g * _sigmoid(g))).astype(BF16)

    tiles = [dict(b=b, pair=pair, sl=slice(pair * LANES, (pair + 1) * LANES))
             for b in range(nb) for pair in range(H_RET // 2)]
    for stage in (scores, values_and_state, head_norm_and_gate):
        for c in tiles:
            stage(c)


def _retention(pc, dec, qs, ks, cd, gain, ones, batch, seq_len):
    n = pc.shape[0]
    c = C_RET
    nb = NB_RET
    const2 = lambda b, t: (0, 0)
    pc3 = pc.reshape(batch, seq_len, 4 * D_RET)
    blk = lambda j: pl.BlockSpec((nb, c, D_RET), lambda b, t: (b, t, j))
    out = pl.pallas_call(
        _ret_kernel,
        grid=(batch // nb, seq_len // c),
        in_specs=[
            blk(0), blk(1), blk(2), blk(3),
            pl.BlockSpec((H_RET, c, c), lambda b, t: (0, 0, 0)),
            pl.BlockSpec((c, D_RET), const2),
            pl.BlockSpec((c, D_RET), const2),
            pl.BlockSpec((1, D_RET), const2),
            pl.BlockSpec((1, D_RET), const2),
            pl.BlockSpec((LANES, LANES), const2),
        ],
        out_specs=blk(0),
        out_shape=jax.ShapeDtypeStruct((batch, seq_len, D_RET), BF16),
        scratch_shapes=[pltpu.VMEM((nb, H_RET // 2, LANES, LANES), F32)],
        compiler_params=pltpu.CompilerParams(
            dimension_semantics=("arbitrary", "arbitrary"), vmem_limit_bytes=VMEM_LIMIT),
        name="retention",
    )(pc3, pc3, pc3, pc3, dec, qs, ks, cd, gain, ones)
    return out.reshape(n, D_RET)


def _fox_kernel(qt_ref, k_ref, vt_ref, g_ref, o_ref, s_scr, p_scr, acc_scr):
    tq = TQ_FOX
    nq = o_ref.shape[0] // tq
    blocks = [(qi, kj) for qi in range(nq) for kj in range(qi + 1)]
    key = lax.broadcasted_iota(jnp.int32, (tq, tq), 0)
    qry = lax.broadcasted_iota(jnp.int32, (tq, tq), 1)
    causal = qry >= key

    def scores(t):
        qi, kj = blocks[t]
        for hh in range(2):
            slot = slice(hh * LANES, (hh + 1) * LANES)
            s_scr[t % 2, hh] = _dot(k_ref[kj * tq:(kj + 1) * tq, slot],
                                    qt_ref[0, slot, qi * tq:(qi + 1) * tq])

    def softmax(t, m_old):
        qi, kj = blocks[t]
        m_new, alpha = [], []
        for hh in range(2):
            s = s_scr[t % 2, hh]
            if kj == qi:
                s = jnp.where(causal, s, MASK_VALUE)
            m_blk = jnp.max(s, axis=0, keepdims=True)
            if kj > 0:
                m_blk = jnp.maximum(m_old[hh], m_blk)
                alpha.append(jnp.exp2(m_old[hh] - m_blk))
            p_scr[t % 2, hh] = jnp.exp2(s - m_blk).astype(BF16)
            m_new.append(m_blk)
        return m_new, alpha

    def accumulate(t, alpha):
        qi, kj = blocks[t]
        for hh in range(2):
            pv = _dot(vt_ref[0, hh * VT_ROWS:(hh + 1) * VT_ROWS, kj * tq:(kj + 1) * tq],
                      p_scr[t % 2, hh])
            acc_scr[qi % 2, hh] = pv if kj == 0 else alpha[hh] * acc_scr[qi % 2, hh] + pv
        if kj == qi:
            heads = []
            for hh in range(2):
                total = acc_scr[qi % 2, hh]
                heads.append(total[:HEAD_DIM] / total[HEAD_DIM:HEAD_DIM + 1])
            o_t = jnp.concatenate(heads, axis=0)
            rows = slice(qi * tq, (qi + 1) * tq)
            o_ref[rows, :] = (o_t.T * _sigmoid(g_ref[rows, :].astype(F32))).astype(BF16)

    scores(0)
    m_run, alpha_prev = None, None
    for t in range(len(blocks)):
        m_run, alpha_t = softmax(t, m_run)
        if t + 1 < len(blocks):
            scores(t + 1)
        if t > 0:
            accumulate(t - 1, alpha_prev)
        alpha_prev = alpha_t
    accumulate(len(blocks) - 1, alpha_prev)


def _fox(qt, ka, vt, gb, batch, seq_len):
    n = ka.shape[0]
    tq = TQ_FOX
    pairs = H_FOX // 2
    return pl.pallas_call(
        _fox_kernel,
        grid=(batch, pairs),
        in_specs=[
            pl.BlockSpec((1, 2 * LANES, seq_len), lambda b, p: (b, p, 0)),
            pl.BlockSpec((seq_len, 2 * LANES), lambda b, p: (b, p)),
            pl.BlockSpec((1, 2 * VT_ROWS, seq_len), lambda b, p: (b, p, 0)),
            pl.BlockSpec((seq_len, LANES), lambda b, p: (b, p)),
        ],
        out_specs=pl.BlockSpec((seq_len, LANES), lambda b, p: (b, p)),
        out_shape=jax.ShapeDtypeStruct((n, D_FOX), BF16),
        scratch_shapes=[
            pltpu.VMEM((2, 2, tq, tq), F32),
            pltpu.VMEM((2, 2, tq, tq), BF16),
            pltpu.VMEM((2, 2, VT_ROWS, tq), F32),
        ],
        compiler_params=pltpu.CompilerParams(
            dimension_semantics=("arbitrary", "arbitrary"), vmem_limit_bytes=VMEM_LIMIT),
        name="fox",
    )(qt, ka, vt, gb)


def _mix_ffn_kernel(oa_ref, ob_ref, oc_ref, wo_ref, gmix_ref, x_ref, gpre_ref,
                    wup_ref, cw_ref, cb_ref, wd_ref, gpost_ref, out_ref, xp_scr, act_scr, tail_scr,
                    *, tiles_per_seq):
    i = pl.program_id(0)
    tm = x_ref.shape[0]
    fc = FC_FFN

    mixer_out = jnp.concatenate([oa_ref[...], ob_ref[...], oc_ref[...]], axis=1)
    x_mid = x_ref[...] + _rmsnorm_rows(_dot(mixer_out, wo_ref[...]), gmix_ref[...])
    xp_scr[...] = pltpu.einshape("(ab)f->(ba)f", x_mid, a=SUBLANES)
    h = _rmsnorm_rows(xp_scr[...], gpre_ref[...]).astype(BF16)

    @pl.when(i == 0)
    def _():
        tail_scr[...] = jnp.zeros_like(tail_scr)

    seq_start = i % tiles_per_seq == 0
    first_sublane = lax.broadcasted_iota(jnp.int32, (SUBLANES, fc), 0) == 0

    def causal_conv(u, col, slot):
        prev = jnp.where(seq_start, 0.0, tail_scr[slot])
        last = u[tm - 2 * SUBLANES:, :]
        tail_scr[slot] = last
        head = []
        for g in range(2):
            rows = slice(g * SUBLANES, (g + 1) * SUBLANES)
            entering = prev[(g + 1) * SUBLANES - 1:(g + 1) * SUBLANES, :]
            head.append(jnp.where(first_sublane, entering, pltpu.roll(last[rows], 1, 0)))
        u1 = jnp.concatenate([head[1], u[:tm - SUBLANES]], axis=0)
        u2 = jnp.concatenate([head[0], head[1], u[:tm - 2 * SUBLANES]], axis=0)
        cols = slice(col, col + fc)
        return (cb_ref[:, cols] + cw_ref[0:1, cols] * u2 + cw_ref[1:2, cols] * u1
                + cw_ref[2:3, cols] * u)

    for c in range(D_FF // fc):
        gate = causal_conv(_dot(h, wup_ref[:, c * fc:(c + 1) * fc]), c * fc, 2 * c)
        val = causal_conv(_dot(h, wup_ref[:, D_FF + c * fc:D_FF + (c + 1) * fc]),
                          D_FF + c * fc, 2 * c + 1)
        act_scr[:, c * fc:(c + 1) * fc] = (gate * _sigmoid(gate) * val).astype(BF16)

    y = _dot(act_scr[...], wd_ref[...])
    res = xp_scr[...] + _rmsnorm_rows(y, gpost_ref[...])
    out_ref[...] = pltpu.einshape("(ba)f->(ab)f", res, a=SUBLANES)


def _mix_ffn(oa, ob, oc, wo, gmix, x2, gpre, w_up, conv_w, conv_b, w_down, gpost, seq_len):
    n = x2.shape[0]
    tm = TM_FFN
    nf = D_FF // FC_FFN
    tiles_per_seq = seq_len // tm
    row = lambda i: (i, 0)
    resident = lambda shape: pl.BlockSpec(shape, lambda i: (0, 0), pipeline_mode=pl.Buffered(1))
    return pl.pallas_call(
        functools.partial(_mix_ffn_kernel, tiles_per_seq=tiles_per_seq),
        grid=(n // tm,),
        in_specs=[
            pl.BlockSpec((tm, D_HGRN), row),
            pl.BlockSpec((tm, D_FOX), row),
            pl.BlockSpec((tm, D_RET), row),
            resident((D_HGRN + D_FOX + D_RET, D_MODEL)),
            resident((1, D_MODEL)),
            pl.BlockSpec((tm, D_MODEL), row),
            resident((1, D_MODEL)),
            resident((D_MODEL, 2 * D_FF)),
            resident((CONV_W, 2 * D_FF)),
            resident((1, 2 * D_FF)),
            resident((D_FF, D_MODEL)),
            resident((1, D_MODEL)),
        ],
        out_specs=pl.BlockSpec((tm, D_MODEL), row),
        out_shape=jax.ShapeDtypeStruct((n, D_MODEL), F32),
        scratch_shapes=[
            pltpu.VMEM((tm, D_MODEL), F32),
            pltpu.VMEM((tm, D_FF), BF16),
            pltpu.VMEM((2 * nf, 2 * SUBLANES, FC_FFN), F32),
        ],
        compiler_params=pltpu.CompilerParams(
            dimension_semantics=("arbitrary",), vmem_limit_bytes=VMEM_LIMIT),
        name="mix_ffn",
    )(oa, ob, oc, wo, gmix, x2, gpre, w_up, conv_w, conv_b, w_down, gpost)


def _block_ones(n):
    idx = jnp.arange(n) // HEAD_DIM
    return (idx[:, None] == idx[None, :]).astype(BF16)


def _lower_tri(n, block):
    r = jnp.arange(n)
    return ((r[:, None] >= r[None, :]) & (r[:, None] // block == r[None, :] // block)).astype(BF16)


def _rotary_tables(seq_len):
    inv_freq = 1.0 / (ROPE_BASE ** (jnp.arange(0, HEAD_DIM, 2, dtype=F32) / HEAD_DIM))
    ang = jnp.arange(seq_len, dtype=F32)[:, None] * inv_freq[None, :]
    cos, sin = jnp.cos(ang), jnp.sin(ang)
    cos_full = jnp.tile(jnp.concatenate([cos, cos], axis=-1), (1, H_RET))
    sin_signed = jnp.tile(jnp.concatenate([-sin, sin], axis=-1), (1, H_RET))
    return cos_full, sin_signed


def _retention_tables(c):
    log_gamma = jnp.log1p(-jnp.exp2(-5.0 - jnp.arange(H_RET, dtype=F32)))
    pos = jnp.arange(c, dtype=F32)
    rel = pos[:, None] - pos[None, :]
    dec = jnp.where(rel >= 0, jnp.exp(log_gamma[:, None, None] * jnp.maximum(rel, 0.0)), 0.0)
    per_lane = lambda a: jnp.repeat(a, HEAD_DIM, axis=-1)
    qs = per_lane(jnp.exp(log_gamma[None, :] * (pos[:, None] + 1.0)))
    ks = per_lane(jnp.exp(log_gamma[None, :] * (c - 1.0 - pos[:, None])))
    cd = per_lane(jnp.exp(log_gamma * c)[None, :])
    return dec, qs, ks, cd


def _reorder_in_weight_rows(w_t):
    widths = [D_HGRN] * 4 + [D_FOX] * 3 + [H_FOX] + [D_FOX] + [D_RET] * 4
    offs = [0]
    for wd in widths:
        offs.append(offs[-1] + wd)
    part = lambda j: w_t[offs[j]:offs[j + 1]]
    a_q, a_f, a_i, a_g, b_q, b_k, b_v, b_f, b_g, c_q, c_k, c_v, c_g = (part(j) for j in range(13))
    b_f = jnp.pad(b_f, ((0, FGATE_PAD - H_FOX), (0, 0)))
    return jnp.concatenate(
        [a_q, a_i, a_g, a_f, b_q, b_k, b_v, b_g, b_f, c_q, c_k, c_v, c_g], axis=0)


def _transpose_cast_kernel(w_ref, o_ref):
    o_ref[...] = w_ref[...].T.astype(BF16)


def _transpose_cast(w_t):
    rows = w_t.shape[0]
    blk = D_HGRN
    return pl.pallas_call(
        _transpose_cast_kernel,
        grid=(rows // blk,),
        in_specs=[pl.BlockSpec((blk, D_MODEL), lambda i: (i, 0))],
        out_specs=pl.BlockSpec((D_MODEL, blk), lambda i: (0, i)),
        out_shape=jax.ShapeDtypeStruct((D_MODEL, rows), BF16),
        compiler_params=pltpu.CompilerParams(
            dimension_semantics=("arbitrary",), vmem_limit_bytes=VMEM_LIMIT),
        name="wprep",
    )(w_t)


def kernel(x, w_in, b_fox_f, fox_q_gain, fox_k_gain, hgrn_lb, hgrn_out_gain, ret_out_gain, w_out,
           g_mix_pre, g_mix_post, w_up, conv_w, conv_b, w_down, g_ffn_pre, g_ffn_post):
    batch, seq_len, d_model = x.shape
    assert d_model == D_MODEL
    assert seq_len % TM_FFN == 0 and seq_len % TQ_FOX == 0
    assert batch % NB_HGRN == 0 and batch % NB_RET == 0
    depth = w_in.shape[0]
    n = batch * seq_len

    cos_full, sin_signed = _rotary_tables(seq_len)
    dec, qs, ks, cd = _retention_tables(C_RET)
    tri_proj = _lower_tri(CUM_BLOCK, CUM_BLOCK)
    mats_hgrn = _hgrn_matrices(MAT_BLOCK)
    ones_pair = _block_ones(LANES)
    lb = hgrn_lb.astype(F32)

    w_in_t = jnp.transpose(w_in, (2, 0, 1))

    x2 = x.reshape(n, D_MODEL)
    for l in range(depth):
        w = _transpose_cast(_reorder_in_weight_rows(w_in_t[:, l, :]))
        bf = jnp.pad(b_fox_f[l].astype(F32), (0, FGATE_PAD - H_FOX))[None, :]
        qg = jnp.tile(fox_q_gain[l].astype(F32), H_FOX)[None, :]
        kg = jnp.tile(fox_k_gain[l].astype(F32), H_FOX)[None, :]
        pa, za, qt, ka, vt, gb, pc = _inproj(x2, g_mix_pre[l][None, :], w, bf, qg, kg, cos_full,
                                             sin_signed, tri_proj, ones_pair, seq_len)

        o_a = _hgrn(pa, za, lb, hgrn_out_gain[l].reshape(1, D_HGRN), mats_hgrn, ones_pair,
                    batch, seq_len, l)
        o_b = _fox(qt, ka, vt, gb, batch, seq_len)
        o_c = _retention(pc, dec, qs, ks, cd, ret_out_gain[l].reshape(1, D_RET), ones_pair,
                         batch, seq_len)

        x2 = _mix_ffn(o_a, o_b, o_c, w_out[l].astype(BF16),
                      g_mix_post[l][None, :], x2, g_ffn_pre[l][None, :], w_up[l].astype(BF16),
                      conv_w[l], conv_b[l][None, :], w_down[l].astype(BF16),
                      g_ffn_post[l][None, :], seq_len)
    return x2.reshape(batch, seq_len, D_MODEL)
```

```python
import functools

import jax
import jax.numpy as jnp
from jax import lax
from jax.experimental import pallas as pl
from jax.experimental.pallas import tpu as pltpu

F32 = jnp.float32
BF16 = jnp.bfloat16

D_MODEL = 1024
HEAD_DIM = 64
H_HGRN, H_FOX, H_RET = 6, 6, 4
D_HGRN, D_FOX, D_RET = H_HGRN * HEAD_DIM, H_FOX * HEAD_DIM, H_RET * HEAD_DIM
D_FF = 2816
CONV_W = 3
ROPE_BASE = 10000.0
EPS = 1e-6
MIN_FORGET = 1e-12
MASK_VALUE = -1e30

LANES = 128
SUBLANES = 8
FGATE_PAD = LANES
BF16_SUBLANES = 16
VT_ROWS = HEAD_DIM + BF16_SUBLANES
LOG2E = 1.4426950408889634

_A_MAIN = (0, 3 * D_HGRN)
_A_GATE = (_A_MAIN[1], _A_MAIN[1] + D_HGRN)
_B_Q = (_A_GATE[1], _A_GATE[1] + D_FOX)
_B_K = (_B_Q[1], _B_Q[1] + D_FOX)
_B_V = (_B_K[1], _B_K[1] + D_FOX)
_B_G = (_B_V[1], _B_V[1] + D_FOX)
_B_F = (_B_G[1], _B_G[1] + FGATE_PAD)
_C_Q = (_B_F[1], _B_F[1] + D_RET)
_C_K = (_C_Q[1], _C_Q[1] + D_RET)
_C_VG = (_C_K[1], _C_K[1] + 2 * D_RET)
IN_COLS_PADDED = _C_VG[1]

TM_PROJ = 512
CUM_BLOCK = 128
TM_FFN = 1024
FC_FFN = 256
TT_HGRN = 256
NB_HGRN = 4
SUB_HGRN = 32
MAT_BLOCK = 128
SAFE_DIAG = 4
FAST_PATH_MAX_DECAY = 60.0
C_RET = 256
NB_RET = 4
TQ_FOX = 512
VMEM_LIMIT = 56 * 1024 * 1024


def _dot(a, b):
    return jnp.dot(a, b, preferred_element_type=F32)


def _dot_nt(a, b):
    return lax.dot_general(a, b, (((1,), (1,)), ((), ())), preferred_element_type=F32)


def _dot_tn(a, b):
    return lax.dot_general(a, b, (((0,), (0,)), ((), ())), preferred_element_type=F32)


def _dot_01_lhs(m01, x):
    hi = x.astype(BF16)
    r = x - hi.astype(F32)
    mid = r.astype(BF16)
    lo = (r - mid.astype(F32)).astype(BF16)
    return _dot(m01, hi) + _dot(m01, mid) + _dot(m01, lo)


def _head_mean_sq(y, ones_bd):
    sq = y * y
    hi = sq.astype(BF16)
    lo = (sq - hi.astype(F32)).astype(BF16)
    return (_dot(hi, ones_bd) + _dot(lo, ones_bd)) * (1.0 / HEAD_DIM)


def _rmsnorm_rows(x, gain):
    return x * lax.rsqrt(jnp.mean(x * x, axis=-1, keepdims=True) + EPS) * gain


def _sigmoid(x):
    return 1.0 / (1.0 + jnp.exp(-x))


def _first_head_lanes(shape):
    return (lax.broadcasted_iota(jnp.int32, shape, len(shape) - 1) & HEAD_DIM) == 0


def _bf16_terms(x):
    hi = x.astype(BF16).astype(F32)
    mid = (x - hi).astype(BF16).astype(F32)
    lo = (x - hi - mid).astype(BF16).astype(F32)
    return hi, mid, lo


def _inproj_kernel(x_ref, gpre_ref, w_ref, bf_ref, qg_ref, kg_ref, cos_ref, sin_ref, tri_ref,
                   ones_ref, pa_ref, za_ref, qt_ref, ka_ref, vt_ref, gb_ref, pc_ref, carry_ref,
                   *, tiles_per_seq):
    @pl.when(pl.program_id(0) % tiles_per_seq == 0)
    def _():
        carry_ref[...] = jnp.zeros_like(carry_ref)

    h = _rmsnorm_rows(x_ref[...], gpre_ref[...]).astype(BF16)
    tm = h.shape[0]

    group_ends = (_A_GATE[1], _B_K[1], _B_G[1], IN_COLS_PADDED)
    group_result = {}

    def proj(cols):
        start = 0
        for end in group_ends:
            if cols[1] <= end:
                break
            start = end
        if start not in group_result:
            group_result[start] = _dot(h, w_ref[:, start:end])
        return group_result[start][:, cols[0] - start:cols[1] - start]

    zf = proj(_B_F) + bf_ref[...]
    log_f = -(jnp.maximum(-zf, 0.0) + jnp.log1p(jnp.exp(-jnp.abs(zf))))
    running = carry_ref[...]
    blocks = []
    for r in range(0, tm, CUM_BLOCK):
        blocks.append(_dot_01_lhs(tri_ref[...], log_f[r:r + CUM_BLOCK]) + running)
        running = blocks[-1][CUM_BLOCK - 1:, :]
    cum = jnp.concatenate(blocks, axis=0)
    carry_ref[...] = running

    ones = ones_ref[...]
    bq = proj(_B_Q)
    bk = proj(_B_K)
    lane = lax.broadcasted_iota(jnp.int32, (tm, LANES), 1)
    is_feature = lane < HEAD_DIM
    is_one_q = (lane >= HEAD_DIM + 3) & (lane < HEAD_DIM + 6)
    is_one_k = (lane >= HEAD_DIM) & (lane < HEAD_DIM + 3)
    for pair in range(H_FOX // 2):
        sl = slice(pair * LANES, (pair + 1) * LANES)
        q2 = bq[:, sl]
        q2 = (q2 * lax.rsqrt(_head_mean_sq(q2, ones) + EPS) * qg_ref[:, sl]
              * (HEAD_DIM ** -0.5 * LOG2E))
        k2 = bk[:, sl]
        k2 = k2 * lax.rsqrt(_head_mean_sq(k2, ones) + EPS) * kg_ref[:, sl]
        for odd in range(2):
            head = 2 * pair + odd
            slot = slice(head * LANES, (head + 1) * LANES)
            qh = pltpu.roll(q2, HEAD_DIM, 1) if odd else q2
            kh = pltpu.roll(k2, HEAD_DIM, 1) if odd else k2
            hi, mid, lo = _bf16_terms(
                jnp.broadcast_to(cum[:, head:head + 1], (tm, LANES)) * LOG2E)
            q_bias = jnp.where(lane == HEAD_DIM, hi, jnp.where(lane == HEAD_DIM + 1, mid, jnp.where(
                lane == HEAD_DIM + 2, lo, jnp.where(is_one_q, 1.0, 0.0))))
            k_bias = jnp.where(lane == HEAD_DIM + 3, -hi, jnp.where(lane == HEAD_DIM + 4, -mid, jnp.where(
                lane == HEAD_DIM + 5, -lo, jnp.where(is_one_k, 1.0, 0.0))))
            qt_ref[0, slot, :] = jnp.where(is_feature, qh, q_bias).T.astype(BF16)
            ka_ref[:, slot] = jnp.where(is_feature, kh, k_bias).astype(BF16)
    pa_ref[...] = proj(_A_MAIN).astype(BF16)
    za_ref[...] = proj(_A_GATE)

    v_t = proj(_B_V).T.astype(BF16)
    ones_row = jnp.where(lax.broadcasted_iota(jnp.int32, (BF16_SUBLANES, tm), 0) == 0,
                         1.0, 0.0).astype(BF16)
    for head in range(H_FOX):
        vt_ref[0, head * VT_ROWS:head * VT_ROWS + HEAD_DIM, :] = (
            v_t[head * HEAD_DIM:(head + 1) * HEAD_DIM, :])
        vt_ref[0, head * VT_ROWS + HEAD_DIM:(head + 1) * VT_ROWS, :] = ones_row
    gb_ref[...] = proj(_B_G).astype(BF16)

    cos = cos_ref[...]
    sin = sin_ref[...]
    low_half = (lax.broadcasted_iota(jnp.int32, cos.shape, 1) & (HEAD_DIM // 2)) == 0

    def rotary(y):
        swapped = jnp.where(low_half, pltpu.roll(y, D_RET - HEAD_DIM // 2, 1),
                            pltpu.roll(y, HEAD_DIM // 2, 1))
        return y * cos + swapped * sin

    pc_ref[:, 0:D_RET] = rotary(proj(_C_Q)).astype(BF16)
    pc_ref[:, D_RET:2 * D_RET] = (rotary(proj(_C_K)) * (HEAD_DIM ** -0.5)).astype(BF16)
    pc_ref[:, 2 * D_RET:4 * D_RET] = proj(_C_VG).astype(BF16)


def _inproj(x2, gpre, w, bf, qg, kg, cos, sin, tri, ones, seq_len):
    n = x2.shape[0]
    batch = n // seq_len
    tm = TM_PROJ
    tiles_per_seq = seq_len // tm
    const = lambda i: (0, 0)
    row = lambda i: (i, 0)
    seq_t = lambda i: (i // tiles_per_seq, 0, i % tiles_per_seq)
    return pl.pallas_call(
        functools.partial(_inproj_kernel, tiles_per_seq=tiles_per_seq),
        grid=(n // tm,),
        in_specs=[
            pl.BlockSpec((tm, D_MODEL), row),
            pl.BlockSpec((1, D_MODEL), const),
            pl.BlockSpec((D_MODEL, IN_COLS_PADDED), const),
            pl.BlockSpec((1, FGATE_PAD), const),
            pl.BlockSpec((1, D_FOX), const),
            pl.BlockSpec((1, D_FOX), const),
            pl.BlockSpec((tm, D_RET), lambda i: (i % tiles_per_seq, 0)),
            pl.BlockSpec((tm, D_RET), lambda i: (i % tiles_per_seq, 0)),
            pl.BlockSpec((CUM_BLOCK, CUM_BLOCK), const),
            pl.BlockSpec((LANES, LANES), const),
        ],
        out_specs=[
            pl.BlockSpec((tm, 3 * D_HGRN), row),
            pl.BlockSpec((tm, D_HGRN), row),
            pl.BlockSpec((1, H_FOX * LANES, tm), seq_t),
            pl.BlockSpec((tm, H_FOX * LANES), row),
            pl.BlockSpec((1, H_FOX * VT_ROWS, tm), seq_t),
            pl.BlockSpec((tm, D_FOX), row),
            pl.BlockSpec((tm, 4 * D_RET), row),
        ],
        out_shape=[
            jax.ShapeDtypeStruct((n, 3 * D_HGRN), BF16),
            jax.ShapeDtypeStruct((n, D_HGRN), F32),
            jax.ShapeDtypeStruct((batch, H_FOX * LANES, seq_len), BF16),
            jax.ShapeDtypeStruct((n, H_FOX * LANES), BF16),
            jax.ShapeDtypeStruct((batch, H_FOX * VT_ROWS, seq_len), BF16),
            jax.ShapeDtypeStruct((n, D_FOX), BF16),
            jax.ShapeDtypeStruct((n, 4 * D_RET), BF16),
        ],
        scratch_shapes=[pltpu.VMEM((1, FGATE_PAD), F32)],
        compiler_params=pltpu.CompilerParams(
            dimension_semantics=("arbitrary",), vmem_limit_bytes=VMEM_LIMIT),
        name="inproj",
    )(x2, gpre, w, bf, qg, kg, cos, sin, tri, ones)


_M_CUM, _M_AFTER, _M_DIAG, _M_MERGE0 = 0, 1, 2, 3
_MERGE_BLOCKS = (2 * SAFE_DIAG, 4 * SAFE_DIAG, 8 * SAFE_DIAG)
assert _MERGE_BLOCKS[-1] == SUB_HGRN


def _hgrn_matrices(tt):
    t = jnp.arange(tt)[:, None]
    u = jnp.arange(tt)[None, :]
    same = lambda m: (t // m) == (u // m)
    mats = [same(SUB_HGRN) & (u <= t), same(SUB_HGRN) & (u > t),
            same(SAFE_DIAG) & (u <= t) & (u > (t // SAFE_DIAG) * SAFE_DIAG)]
    for m in _MERGE_BLOCKS:
        mid = (t // m) * m + m // 2 - 1
        mats.append(same(m) & (((t > mid) & (u > mid) & (u <= t)) | ((t <= mid) & (u > t) & (u <= mid))))
    return jnp.stack(mats).astype(BF16)


def _hgrn_kernel(q_ref, v_ref, g_ref, z_ref, lb_ref, gain_ref, mats_ref, ones_ref, o_ref, st_ref,
                 logf_scr, k_scr, cum_scr, *, layer):
    @pl.when(pl.program_id(1) == 0)
    def _():
        st_ref[...] = jnp.zeros_like(st_ref)

    nb, tt = q_ref.shape[0], q_ref.shape[1]
    sub = SUB_HGRN

    def range_sum(which, log_f):
        return jnp.concatenate([_dot_01_lhs(mats_ref[which], log_f[r:r + MAT_BLOCK])
                                for r in range(0, tt, MAT_BLOCK)], axis=0)

    lb = lb_ref[...]
    e = jnp.exp(lb - jnp.max(lb, axis=0, keepdims=True))
    p = e / jnp.sum(e, axis=0, keepdims=True)
    run = p[0:1]
    for i in range(1, layer + 1):
        run = run + p[i:i + 1]
    lower = run - p[0:1]

    lowest = None
    for b in range(nb):
        sg = _sigmoid(z_ref[b])
        f = lower + (1.0 - lower) * sg
        log_f = jnp.log(jnp.maximum(f, MIN_FORGET))
        cum = range_sum(_M_CUM, log_f)
        logf_scr[b] = log_f
        k_scr[b] = (1.0 - lower) * (1.0 - sg)
        cum_scr[b] = cum
        low_b = jnp.min(cum)
        lowest = low_b if lowest is None else jnp.minimum(lowest, low_b)
    exact_fallback = lowest < -FAST_PATH_MAX_DECAY

    row = lax.broadcasted_iota(jnp.int32, (MAT_BLOCK, MAT_BLOCK), 0)
    col = lax.broadcasted_iota(jnp.int32, (MAT_BLOCK, MAT_BLOCK), 1)

    def diag_mask(m):
        in_block = row & (m - 1)
        dist = row - col
        return ((in_block - dist) | dist) >= 0

    def merge_mask(m):
        half = m // 2
        return (((row ^ col) & ~(m - 1)) | ((row & half) ^ half) | (col & half)) == 0

    first_blk = _first_head_lanes((MAT_BLOCK, LANES))
    rr = lax.broadcasted_iota(jnp.int32, (LANES, LANES), 0)
    cc = lax.broadcasted_iota(jnp.int32, (LANES, LANES), 1)
    same_head = ((rr ^ cc) & HEAD_DIM) == 0
    ones = ones_ref[...]

    def factors(c, exact):
        b, sl = c["b"], c["sl"]
        q = q_ref[b, :, sl].astype(F32)
        k = k_scr[b, :, sl]
        cum = cum_scr[b, :, sl]
        c["v"] = v_ref[b, :, sl]
        c["e_q"] = jnp.exp(cum)
        c["qt"] = (q * c["e_q"]).astype(BF16)
        if exact:
            log_f = logf_scr[b, :, sl]
            since = range_sum(_M_DIAG, log_f)
            pieces = [((q * jnp.exp(since)).astype(BF16), (k * jnp.exp(-since)).astype(BF16),
                       diag_mask(SAFE_DIAG))]
            for idx, m in enumerate(_MERGE_BLOCKS):
                w = jnp.exp(range_sum(_M_MERGE0 + idx, log_f))
                pieces.append(((q * w).astype(BF16), (k * w).astype(BF16), merge_mask(m)))
            c["k_state"] = (k * jnp.exp(range_sum(_M_AFTER, log_f))).astype(BF16)
        else:
            c["k_state"] = (k * jnp.exp(-cum)).astype(BF16)
            pieces = [(c["qt"], c["k_state"], diag_mask(sub))]
        c["pieces"] = pieces

    def intra_scores(c):
        c["s"] = []
        for r0 in range(0, tt, MAT_BLOCK):
            r = slice(r0, r0 + MAT_BLOCK)
            per_head = []
            for keep in (first_blk, ~first_blk):
                s = jnp.zeros((MAT_BLOCK, MAT_BLOCK), F32)
                for q_f, k_f, mask in c["pieces"]:
                    q_h = jnp.where(keep, q_f[r], jnp.zeros_like(q_f[r]))
                    s = jnp.where(mask, _dot_nt(q_h, k_f[r]), s)
                per_head.append(s.astype(BF16))
            c["s"].append(per_head)

    def intra_values(c):
        blocks = []
        for i, r0 in enumerate(range(0, tt, MAT_BLOCK)):
            r = slice(r0, r0 + MAT_BLOCK)
            blocks.append(jnp.where(first_blk, _dot(c["s"][i][0], c["v"][r]),
                                    _dot(c["s"][i][1], c["v"][r])))
        c["o"] = jnp.concatenate(blocks, axis=0)

    def state_updates(c):
        c["upd"] = [jnp.where(same_head, _dot_tn(c["v"][j * sub:(j + 1) * sub],
                                                 c["k_state"][j * sub:(j + 1) * sub]), 0.0)
                    for j in range(tt // sub)]

    def state_chain(c, exact):
        st = st_ref[c["b"], c["pair"]]
        outs = []
        for j in range(tt // sub):
            r = slice(j * sub, (j + 1) * sub)
            outs.append(c["o"][r] + _dot_nt(c["qt"][r], st.astype(BF16)))
            decay = c["e_q"][(j + 1) * sub - 1:(j + 1) * sub, :]
            st = decay * st + c["upd"][j] if exact else decay * (st + c["upd"][j])
        st_ref[c["b"], c["pair"]] = st
        c["o"] = jnp.concatenate(outs, axis=0)

    def head_norm_and_gate(c):
        b, sl = c["b"], c["sl"]
        o = c["o"] * lax.rsqrt(_head_mean_sq(c["o"], ones) + EPS) * gain_ref[:, sl]
        g = g_ref[b, :, sl].astype(F32)
        o_ref[b, :, sl] = (o * (g * _sigmoid(g))).astype(BF16)

    def all_tiles(exact):
        tiles = [dict(b=b, pair=pair, sl=slice(pair * LANES, (pair + 1) * LANES))
                 for b in range(nb) for pair in range(H_HGRN // 2)]
        for stage in (functools.partial(factors, exact=exact), intra_scores, intra_values,
                      state_updates, functools.partial(state_chain, exact=exact),
                      head_norm_and_gate):
            for c in tiles:
                stage(c)

    pl.when(exact_fallback)(functools.partial(all_tiles, True))
    pl.when(jnp.logical_not(exact_fallback))(functools.partial(all_tiles, False))


def _hgrn(pa, za, lb, gain, mats, ones, batch, seq_len, layer):
    n = pa.shape[0]
    tt = TT_HGRN
    nb = NB_HGRN
    const = lambda b, t: (0, 0)
    tile_f32 = pltpu.VMEM((nb, tt, D_HGRN), F32)
    pa3 = pa.reshape(batch, seq_len, 3 * D_HGRN)
    blk = lambda j: pl.BlockSpec((nb, tt, D_HGRN), lambda b, t: (b, t, j))
    out = pl.pallas_call(
        functools.partial(_hgrn_kernel, layer=layer),
        grid=(batch // nb, seq_len // tt),
        in_specs=[
            blk(0), blk(1), blk(2), blk(0),
            pl.BlockSpec(lb.shape, const),
            pl.BlockSpec((1, D_HGRN), const),
            pl.BlockSpec(mats.shape, lambda b, t: (0, 0, 0)),
            pl.BlockSpec((LANES, LANES), const),
        ],
        out_specs=blk(0),
        out_shape=jax.ShapeDtypeStruct((batch, seq_len, D_HGRN), BF16),
        scratch_shapes=[pltpu.VMEM((nb, H_HGRN // 2, LANES, LANES), F32),
                        tile_f32, tile_f32, tile_f32],
        compiler_params=pltpu.CompilerParams(
            dimension_semantics=("arbitrary", "arbitrary"), vmem_limit_bytes=VMEM_LIMIT),
        name="hgrn",
    )(pa3, pa3, pa3, za.reshape(batch, seq_len, D_HGRN), lb, gain, mats, ones)
    return out.reshape(n, D_HGRN)


def _ret_kernel(q_ref, k_ref, v_ref, g_ref, dec_ref, qs_ref, ks_ref, cd_ref, gain_ref, ones_ref,
                o_ref, st_ref):
    @pl.when(pl.program_id(1) == 0)
    def _():
        st_ref[...] = jnp.zeros_like(st_ref)

    nb, c = q_ref.shape[0], q_ref.shape[1]
    first = _first_head_lanes((c, LANES))
    rr = lax.broadcasted_iota(jnp.int32, (LANES, LANES), 0)
    cc = lax.broadcasted_iota(jnp.int32, (LANES, LANES), 1)
    same_head = ((rr ^ cc) & HEAD_DIM) == 0
    ones = ones_ref[...]

    def scores(c):
        q, k = q_ref[c["b"], :, c["sl"]], k_ref[c["b"], :, c["sl"]]
        c["s"] = [(_dot_nt(jnp.where(keep, q, jnp.zeros_like(q)), k)
                   * dec_ref[2 * c["pair"] + hh]).astype(BF16)
                  for hh, keep in enumerate((first, ~first))]

    def values_and_state(c):
        b, pair, sl = c["b"], c["pair"], c["sl"]
        q, k, v = q_ref[b, :, sl], k_ref[b, :, sl], v_ref[b, :, sl]
        st = st_ref[b, pair]
        q_dec = (q.astype(F32) * qs_ref[:, sl]).astype(BF16)
        c["o"] = (jnp.where(first, _dot(c["s"][0], v), _dot(c["s"][1], v))
                  + _dot_nt(q_dec, st.astype(BF16)))
        k_dec = (k.astype(F32) * ks_ref[:, sl]).astype(BF16)
        st_ref[b, pair] = cd_ref[:, sl] * st + jnp.where(same_head, _dot_tn(v, k_dec), 0.0)

    def head_norm_and_gate(c):
        b, sl = c["b"], c["sl"]
        o = c["o"] * lax.rsqrt(_head_mean_sq(c["o"], ones) + EPS) * gain_ref[:, sl]
        g = g_ref[b, :, sl].astype(F32)
        o_ref[b, :, sl] = (o * (g * _sigmoid(g))).astype(BF16)

    tiles = [dict(b=b, pair=pair, sl=slice(pair * LANES, (pair + 1) * LANES))
             for b in range(nb) for pair in range(H_RET // 2)]
    for stage in (scores, values_and_state, head_norm_and_gate):
        for c in tiles:
            stage(c)


def _retention(pc, dec, qs, ks, cd, gain, ones, batch, seq_len):
    n = pc.shape[0]
    c = C_RET
    nb = NB_RET
    const2 = lambda b, t: (0, 0)
    pc3 = pc.reshape(batch, seq_len, 4 * D_RET)
    blk = lambda j: pl.BlockSpec((nb, c, D_RET), lambda b, t: (b, t, j))
    out = pl.pallas_call(
        _ret_kernel,
        grid=(batch // nb, seq_len // c),
        in_specs=[
            blk(0), blk(1), blk(2), blk(3),
            pl.BlockSpec((H_RET, c, c), lambda b, t: (0, 0, 0)),
            pl.BlockSpec((c, D_RET), const2),
            pl.BlockSpec((c, D_RET), const2),
            pl.BlockSpec((1, D_RET), const2),
            pl.BlockSpec((1, D_RET), const2),
            pl.BlockSpec((LANES, LANES), const2),
        ],
        out_specs=blk(0),
        out_shape=jax.ShapeDtypeStruct((batch, seq_len, D_RET), BF16),
        scratch_shapes=[pltpu.VMEM((nb, H_RET // 2, LANES, LANES), F32)],
        compiler_params=pltpu.CompilerParams(
            dimension_semantics=("arbitrary", "arbitrary"), vmem_limit_bytes=VMEM_LIMIT),
        name="retention",
    )(pc3, pc3, pc3, pc3, dec, qs, ks, cd, gain, ones)
    return out.reshape(n, D_RET)


def _fox_kernel(qt_ref, k_ref, vt_ref, g_ref, o_ref, s_scr, p_scr, acc_scr):
    tq = TQ_FOX
    nq = o_ref.shape[0] // tq
    blocks = [(qi, kj) for qi in range(nq) for kj in range(qi + 1)]
    key = lax.broadcasted_iota(jnp.int32, (tq, tq), 0)
    qry = lax.broadcasted_iota(jnp.int32, (tq, tq), 1)
    causal = qry >= key

    def scores(t):
        qi, kj = blocks[t]
        for hh in range(2):
            slot = slice(hh * LANES, (hh + 1) * LANES)
            s_scr[t % 2, hh] = _dot(k_ref[kj * tq:(kj + 1) * tq, slot],
                                    qt_ref[0, slot, qi * tq:(qi + 1) * tq])

    def softmax(t, m_old):
        qi, kj = blocks[t]
        m_new, alpha = [], []
        for hh in range(2):
            s = s_scr[t % 2, hh]
            if kj == qi:
                s = jnp.where(causal, s, MASK_VALUE)
            m_blk = jnp.max(s, axis=0, keepdims=True)
            if kj > 0:
                m_blk = jnp.maximum(m_old[hh], m_blk)
                alpha.append(jnp.exp2(m_old[hh] - m_blk))
            p_scr[t % 2, hh] = jnp.exp2(s - m_blk).astype(BF16)
            m_new.append(m_blk)
        return m_new, alpha

    def accumulate(t, alpha):
        qi, kj = blocks[t]
        for hh in range(2):
            pv = _dot(vt_ref[0, hh * VT_ROWS:(hh + 1) * VT_ROWS, kj * tq:(kj + 1) * tq],
                      p_scr[t % 2, hh])
            acc_scr[qi % 2, hh] = pv if kj == 0 else alpha[hh] * acc_scr[qi % 2, hh] + pv
        if kj == qi:
            heads = []
            for hh in range(2):
                total = acc_scr[qi % 2, hh]
                heads.append(total[:HEAD_DIM] / total[HEAD_DIM:HEAD_DIM + 1])
            o_t = jnp.concatenate(heads, axis=0)
            rows = slice(qi * tq, (qi + 1) * tq)
            o_ref[rows, :] = (o_t.T * _sigmoid(g_ref[rows, :].astype(F32))).astype(BF16)

    scores(0)
    m_run, alpha_prev = None, None
    for t in range(len(blocks)):
        m_run, alpha_t = softmax(t, m_run)
        if t + 1 < len(blocks):
            scores(t + 1)
        if t > 0:
            accumulate(t - 1, alpha_prev)
        alpha_prev = alpha_t
    accumulate(len(blocks) - 1, alpha_prev)


def _fox(qt, ka, vt, gb, batch, seq_len):
    n = ka.shape[0]
    tq = TQ_FOX
    pairs = H_FOX // 2
    return pl.pallas_call(
        _fox_kernel,
        grid=(batch, pairs),
        in_specs=[
            pl.BlockSpec((1, 2 * LANES, seq_len), lambda b, p: (b, p, 0)),
            pl.BlockSpec((seq_len, 2 * LANES), lambda b, p: (b, p)),
            pl.BlockSpec((1, 2 * VT_ROWS, seq_len), lambda b, p: (b, p, 0)),
            pl.BlockSpec((seq_len, LANES), lambda b, p: (b, p)),
        ],
        out_specs=pl.BlockSpec((seq_len, LANES), lambda b, p: (b, p)),
        out_shape=jax.ShapeDtypeStruct((n, D_FOX), BF16),
        scratch_shapes=[
            pltpu.VMEM((2, 2, tq, tq), F32),
            pltpu.VMEM((2, 2, tq, tq), BF16),
            pltpu.VMEM((2, 2, VT_ROWS, tq), F32),
        ],
        compiler_params=pltpu.CompilerParams(
            dimension_semantics=("arbitrary", "arbitrary"), vmem_limit_bytes=VMEM_LIMIT),
        name="fox",
    )(qt, ka, vt, gb)


def _mix_ffn_kernel(oa_ref, ob_ref, oc_ref, wo_ref, gmix_ref, x_ref, gpre_ref,
                    wup_ref, cw_ref, cb_ref, wd_ref, gpost_ref, out_ref, xp_scr, act_scr, tail_scr,
                    *, tiles_per_seq):
    i = pl.program_id(0)
    tm = x_ref.shape[0]
    fc = FC_FFN

    mixer_out = jnp.concatenate([oa_ref[...], ob_ref[...], oc_ref[...]], axis=1)
    x_mid = x_ref[...] + _rmsnorm_rows(_dot(mixer_out, wo_ref[...]), gmix_ref[...])
    xp_scr[...] = pltpu.einshape("(ab)f->(ba)f", x_mid, a=SUBLANES)
    h = _rmsnorm_rows(xp_scr[...], gpre_ref[...]).astype(BF16)

    @pl.when(i == 0)
    def _():
        tail_scr[...] = jnp.zeros_like(tail_scr)

    seq_start = i % tiles_per_seq == 0
    first_sublane = lax.broadcasted_iota(jnp.int32, (SUBLANES, fc), 0) == 0

    def causal_conv(u, col, slot):
        prev = jnp.where(seq_start, 0.0, tail_scr[slot])
        last = u[tm - 2 * SUBLANES:, :]
        tail_scr[slot] = last
        head = []
        for g in range(2):
            rows = slice(g * SUBLANES, (g + 1) * SUBLANES)
            entering = prev[(g + 1) * SUBLANES - 1:(g + 1) * SUBLANES, :]
            head.append(jnp.where(first_sublane, entering, pltpu.roll(last[rows], 1, 0)))
        u1 = jnp.concatenate([head[1], u[:tm - SUBLANES]], axis=0)
        u2 = jnp.concatenate([head[0], head[1], u[:tm - 2 * SUBLANES]], axis=0)
        cols = slice(col, col + fc)
        return (cb_ref[:, cols] + cw_ref[0:1, cols] * u2 + cw_ref[1:2, cols] * u1
                + cw_ref[2:3, cols] * u)

    for c in range(D_FF // fc):
        gate = causal_conv(_dot(h, wup_ref[:, c * fc:(c + 1) * fc]), c * fc, 2 * c)
        val = causal_conv(_dot(h, wup_ref[:, D_FF + c * fc:D_FF + (c + 1) * fc]),
                          D_FF + c * fc, 2 * c + 1)
        act_scr[:, c * fc:(c + 1) * fc] = (gate * _sigmoid(gate) * val).astype(BF16)

    y = _dot(act_scr[...], wd_ref[...])
    res = xp_scr[...] + _rmsnorm_rows(y, gpost_ref[...])
    out_ref[...] = pltpu.einshape("(ba)f->(ab)f", res, a=SUBLANES)


def _mix_ffn(oa, ob, oc, wo, gmix, x2, gpre, w_up, conv_w, conv_b, w_down, gpost, seq_len):
    n = x2.shape[0]
    tm = TM_FFN
    nf = D_FF // FC_FFN
    tiles_per_seq = seq_len // tm
    row = lambda i: (i, 0)
    resident = lambda shape: pl.BlockSpec(shape, lambda i: (0, 0), pipeline_mode=pl.Buffered(1))
    return pl.pallas_call(
        functools.partial(_mix_ffn_kernel, tiles_per_seq=tiles_per_seq),
        grid=(n // tm,),
        in_specs=[
            pl.BlockSpec((tm, D_HGRN), row),
            pl.BlockSpec((tm, D_FOX), row),
            pl.BlockSpec((tm, D_RET), row),
            resident((D_HGRN + D_FOX + D_RET, D_MODEL)),
            resident((1, D_MODEL)),
            pl.BlockSpec((tm, D_MODEL), row),
            resident((1, D_MODEL)),
            resident((D_MODEL, 2 * D_FF)),
            resident((CONV_W, 2 * D_FF)),
            resident((1, 2 * D_FF)),
            resident((D_FF, D_MODEL)),
            resident((1, D_MODEL)),
        ],
        out_specs=pl.BlockSpec((tm, D_MODEL), row),
        out_shape=jax.ShapeDtypeStruct((n, D_MODEL), F32),
        scratch_shapes=[
            pltpu.VMEM((tm, D_MODEL), F32),
            pltpu.VMEM((tm, D_FF), BF16),
            pltpu.VMEM((2 * nf, 2 * SUBLANES, FC_FFN), F32),
        ],
        compiler_params=pltpu.CompilerParams(
            dimension_semantics=("arbitrary",), vmem_limit_bytes=VMEM_LIMIT),
        name="mix_ffn",
    )(oa, ob, oc, wo, gmix, x2, gpre, w_up, conv_w, conv_b, w_down, gpost)


def _block_ones(n):
    idx = jnp.arange(n) // HEAD_DIM
    return (idx[:, None] == idx[None, :]).astype(BF16)


def _lower_tri(n, block):
    r = jnp.arange(n)
    return ((r[:, None] >= r[None, :]) & (r[:, None] // block == r[None, :] // block)).astype(BF16)


def _rotary_tables(seq_len):
    inv_freq = 1.0 / (ROPE_BASE ** (jnp.arange(0, HEAD_DIM, 2, dtype=F32) / HEAD_DIM))
    ang = jnp.arange(seq_len, dtype=F32)[:, None] * inv_freq[None, :]
    cos, sin = jnp.cos(ang), jnp.sin(ang)
    cos_full = jnp.tile(jnp.concatenate([cos, cos], axis=-1), (1, H_RET))
    sin_signed = jnp.tile(jnp.concatenate([-sin, sin], axis=-1), (1, H_RET))
    return cos_full, sin_signed


def _retention_tables(c):
    log_gamma = jnp.log1p(-jnp.exp2(-5.0 - jnp.arange(H_RET, dtype=F32)))
    pos = jnp.arange(c, dtype=F32)
    rel = pos[:, None] - pos[None, :]
    dec = jnp.where(rel >= 0, jnp.exp(log_gamma[:, None, None] * jnp.maximum(rel, 0.0)), 0.0)
    per_lane = lambda a: jnp.repeat(a, HEAD_DIM, axis=-1)
    qs = per_lane(jnp.exp(log_gamma[None, :] * (pos[:, None] + 1.0)))
    ks = per_lane(jnp.exp(log_gamma[None, :] * (c - 1.0 - pos[:, None])))
    cd = per_lane(jnp.exp(log_gamma * c)[None, :])
    return dec, qs, ks, cd


def _reorder_in_weight_rows(w_t):
    widths = [D_HGRN] * 4 + [D_FOX] * 3 + [H_FOX] + [D_FOX] + [D_RET] * 4
    offs = [0]
    for wd in widths:
        offs.append(offs[-1] + wd)
    part = lambda j: w_t[offs[j]:offs[j + 1]]
    a_q, a_f, a_i, a_g, b_q, b_k, b_v, b_f, b_g, c_q, c_k, c_v, c_g = (part(j) for j in range(13))
    b_f = jnp.pad(b_f, ((0, FGATE_PAD - H_FOX), (0, 0)))
    return jnp.concatenate(
        [a_q, a_i, a_g, a_f, b_q, b_k, b_v, b_g, b_f, c_q, c_k, c_v, c_g], axis=0)


def _transpose_cast_kernel(w_ref, o_ref):
    o_ref[...] = w_ref[...].T.astype(BF16)


def _transpose_cast(w_t):
    rows = w_t.shape[0]
    blk = D_HGRN
    return pl.pallas_call(
        _transpose_cast_kernel,
        grid=(rows // blk,),
        in_specs=[pl.BlockSpec((blk, D_MODEL), lambda i: (i, 0))],
        out_specs=pl.BlockSpec((D_MODEL, blk), lambda i: (0, i)),
        out_shape=jax.ShapeDtypeStruct((D_MODEL, rows), BF16),
        compiler_params=pltpu.CompilerParams(
            dimension_semantics=("arbitrary",), vmem_limit_bytes=VMEM_LIMIT),
        name="wprep",
    )(w_t)


def kernel(x, w_in, b_fox_f, fox_q_gain, fox_k_gain, hgrn_lb, hgrn_out_gain, ret_out_gain, w_out,
           g_mix_pre, g_mix_post, w_up, conv_w, conv_b, w_down, g_ffn_pre, g_ffn_post):
    batch, seq_len, d_model = x.shape
    assert d_model == D_MODEL
    assert seq_len % TM_FFN == 0 and seq_len % TQ_FOX == 0
    assert batch % NB_HGRN == 0 and batch % NB_RET == 0
    depth = w_in.shape[0]
    n = batch * seq_len

    cos_full, sin_signed = _rotary_tables(seq_len)
    dec, qs, ks, cd = _retention_tables(C_RET)
    tri_proj = _lower_tri(CUM_BLOCK, CUM_BLOCK)
    mats_hgrn = _hgrn_matrices(MAT_BLOCK)
    ones_pair = _block_ones(LANES)
    lb = hgrn_lb.astype(F32)

    w_in_t = jnp.transpose(w_in, (2, 0, 1))

    x2 = x.reshape(n, D_MODEL)
    for l in range(depth):
        w = _transpose_cast(_reorder_in_weight_rows(w_in_t[:, l, :]))
        bf = jnp.pad(b_fox_f[l].astype(F32), (0, FGATE_PAD - H_FOX))[None, :]
        qg = jnp.tile(fox_q_gain[l].astype(F32), H_FOX)[None, :]
        kg = jnp.tile(fox_k_gain[l].astype(F32), H_FOX)[None, :]
        pa, za, qt, ka, vt, gb, pc = _inproj(x2, g_mix_pre[l][None, :], w, bf, qg, kg, cos_full,
                                             sin_signed, tri_proj, ones_pair, seq_len)

        o_a = _hgrn(pa, za, lb, hgrn_out_gain[l].reshape(1, D_HGRN), mats_hgrn, ones_pair,
                    batch, seq_len, l)
        o_b = _fox(qt, ka, vt, gb, batch, seq_len)
        o_c = _retention(pc, dec, qs, ks, cd, ret_out_gain[l].reshape(1, D_RET), ones_pair,
                         batch, seq_len)

        x2 = _mix_ffn(o_a, o_b, o_c, w_out[l].astype(BF16),
                      g_mix_post[l][None, :], x2, g_ffn_pre[l][None, :], w_up[l].astype(BF16),
                      conv_w[l], conv_b[l][None, :], w_down[l].astype(BF16),
                      g_ffn_post[l][None, :], seq_len)
    return x2.reshape(batch, seq_len, D_MODEL)
```

```python
import functools

import jax
import jax.numpy as jnp
from jax import lax
from jax.experimental import pallas as pl
from jax.experimental.pallas import tpu as pltpu

F32 = jnp.float32
BF16 = jnp.bfloat16

D_MODEL = 1024
HEAD_DIM = 64
H_HGRN, H_FOX, H_RET = 6, 6, 4
D_HGRN, D_FOX, D_RET = H_HGRN * HEAD_DIM, H_FOX * HEAD_DIM, H_RET * HEAD_DIM
D_FF = 2816
CONV_W = 3
ROPE_BASE = 10000.0
EPS = 1e-6
MIN_FORGET = 1e-12
MASK_VALUE = -1e30

LANES = 128
SUBLANES = 8
FGATE_PAD = LANES
BF16_SUBLANES = 16
VT_ROWS = HEAD_DIM + BF16_SUBLANES
LOG2E = 1.4426950408889634

_A_MAIN = (0, 3 * D_HGRN)
_A_GATE = (_A_MAIN[1], _A_MAIN[1] + D_HGRN)
_B_Q = (_A_GATE[1], _A_GATE[1] + D_FOX)
_B_K = (_B_Q[1], _B_Q[1] + D_FOX)
_B_V = (_B_K[1], _B_K[1] + D_FOX)
_B_G = (_B_V[1], _B_V[1] + D_FOX)
_B_F = (_B_G[1], _B_G[1] + FGATE_PAD)
_C_Q = (_B_F[1], _B_F[1] + D_RET)
_C_K = (_C_Q[1], _C_Q[1] + D_RET)
_C_VG = (_C_K[1], _C_K[1] + 2 * D_RET)
IN_COLS_PADDED = _C_VG[1]

TM_PROJ = 512
CUM_BLOCK = 128
TM_FFN = 1024
FC_FFN = 256
TT_HGRN = 256
NB_HGRN = 4
SUB_HGRN = 32
MAT_BLOCK = 128
SAFE_DIAG = 4
FAST_PATH_MAX_DECAY = 60.0
C_RET = 256
NB_RET = 8
TQ_FOX = 512
VMEM_LIMIT = 56 * 1024 * 1024


def _dot(a, b):
    return jnp.dot(a, b, preferred_element_type=F32)


def _dot_nt(a, b):
    return lax.dot_general(a, b, (((1,), (1,)), ((), ())), preferred_element_type=F32)


def _dot_tn(a, b):
    return lax.dot_general(a, b, (((0,), (0,)), ((), ())), preferred_element_type=F32)


def _dot_01_lhs(m01, x):
    hi = x.astype(BF16)
    r = x - hi.astype(F32)
    mid = r.astype(BF16)
    lo = (r - mid.astype(F32)).astype(BF16)
    return _dot(m01, hi) + _dot(m01, mid) + _dot(m01, lo)


def _head_mean_sq(y, ones_bd):
    sq = y * y
    hi = sq.astype(BF16)
    lo = (sq - hi.astype(F32)).astype(BF16)
    return (_dot(hi, ones_bd) + _dot(lo, ones_bd)) * (1.0 / HEAD_DIM)


def _rmsnorm_rows(x, gain):
    return x * lax.rsqrt(jnp.mean(x * x, axis=-1, keepdims=True) + EPS) * gain


def _sigmoid(x):
    return 1.0 / (1.0 + jnp.exp(-x))


def _first_head_lanes(shape):
    return (lax.broadcasted_iota(jnp.int32, shape, len(shape) - 1) & HEAD_DIM) == 0


def _bf16_terms(x):
    hi = x.astype(BF16).astype(F32)
    mid = (x - hi).astype(BF16).astype(F32)
    lo = (x - hi - mid).astype(BF16).astype(F32)
    return hi, mid, lo


def _inproj_kernel(x_ref, gpre_ref, w_ref, bf_ref, qg_ref, kg_ref, cos_ref, sin_ref, tri_ref,
                   ones_ref, pa_ref, za_ref, qt_ref, ka_ref, vt_ref, gb_ref, pc_ref, carry_ref,
                   *, tiles_per_seq):
    @pl.when(pl.program_id(0) % tiles_per_seq == 0)
    def _():
        carry_ref[...] = jnp.zeros_like(carry_ref)

    h = _rmsnorm_rows(x_ref[...], gpre_ref[...]).astype(BF16)
    tm = h.shape[0]

    group_ends = (_A_GATE[1], _B_K[1], _B_G[1], IN_COLS_PADDED)
    group_result = {}

    def proj(cols):
        start = 0
        for end in group_ends:
            if cols[1] <= end:
                break
            start = end
        if start not in group_result:
            group_result[start] = _dot(h, w_ref[:, start:end])
        return group_result[start][:, cols[0] - start:cols[1] - start]

    zf = proj(_B_F) + bf_ref[...]
    log_f = -(jnp.maximum(-zf, 0.0) + jnp.log1p(jnp.exp(-jnp.abs(zf))))
    running = carry_ref[...]
    blocks = []
    for r in range(0, tm, CUM_BLOCK):
        blocks.append(_dot_01_lhs(tri_ref[...], log_f[r:r + CUM_BLOCK]) + running)
        running = blocks[-1][CUM_BLOCK - 1:, :]
    cum = jnp.concatenate(blocks, axis=0)
    carry_ref[...] = running

    ones = ones_ref[...]
    bq = proj(_B_Q)
    bk = proj(_B_K)
    lane = lax.broadcasted_iota(jnp.int32, (tm, LANES), 1)
    is_feature = lane < HEAD_DIM
    is_one_q = (lane >= HEAD_DIM + 3) & (lane < HEAD_DIM + 6)
    is_one_k = (lane >= HEAD_DIM) & (lane < HEAD_DIM + 3)
    for pair in range(H_FOX // 2):
        sl = slice(pair * LANES, (pair + 1) * LANES)
        q2 = bq[:, sl]
        q2 = (q2 * lax.rsqrt(_head_mean_sq(q2, ones) + EPS) * qg_ref[:, sl]
              * (HEAD_DIM ** -0.5 * LOG2E))
        k2 = bk[:, sl]
        k2 = k2 * lax.rsqrt(_head_mean_sq(k2, ones) + EPS) * kg_ref[:, sl]
        for odd in range(2):
            head = 2 * pair + odd
            slot = slice(head * LANES, (head + 1) * LANES)
            qh = pltpu.roll(q2, HEAD_DIM, 1) if odd else q2
            kh = pltpu.roll(k2, HEAD_DIM, 1) if odd else k2
            hi, mid, lo = _bf16_terms(
                jnp.broadcast_to(cum[:, head:head + 1], (tm, LANES)) * LOG2E)
            q_bias = jnp.where(lane == HEAD_DIM, hi, jnp.where(lane == HEAD_DIM + 1, mid, jnp.where(
                lane == HEAD_DIM + 2, lo, jnp.where(is_one_q, 1.0, 0.0))))
            k_bias = jnp.where(lane == HEAD_DIM + 3, -hi, jnp.where(lane == HEAD_DIM + 4, -mid, jnp.where(
                lane == HEAD_DIM + 5, -lo, jnp.where(is_one_k, 1.0, 0.0))))
            qt_ref[0, slot, :] = jnp.where(is_feature, qh, q_bias).T.astype(BF16)
            ka_ref[:, slot] = jnp.where(is_feature, kh, k_bias).astype(BF16)
    pa_ref[...] = proj(_A_MAIN).astype(BF16)
    za_ref[...] = proj(_A_GATE)

    v_t = proj(_B_V).T.astype(BF16)
    ones_row = jnp.where(lax.broadcasted_iota(jnp.int32, (BF16_SUBLANES, tm), 0) == 0,
                         1.0, 0.0).astype(BF16)
    for head in range(H_FOX):
        vt_ref[0, head * VT_ROWS:head * VT_ROWS + HEAD_DIM, :] = (
            v_t[head * HEAD_DIM:(head + 1) * HEAD_DIM, :])
        vt_ref[0, head * VT_ROWS + HEAD_DIM:(head + 1) * VT_ROWS, :] = ones_row
    gb_ref[...] = proj(_B_G).astype(BF16)

    cos = cos_ref[...]
    sin = sin_ref[...]
    low_half = (lax.broadcasted_iota(jnp.int32, cos.shape, 1) & (HEAD_DIM // 2)) == 0

    def rotary(y):
        swapped = jnp.where(low_half, pltpu.roll(y, D_RET - HEAD_DIM // 2, 1),
                            pltpu.roll(y, HEAD_DIM // 2, 1))
        return y * cos + swapped * sin

    pc_ref[:, 0:D_RET] = rotary(proj(_C_Q)).astype(BF16)
    pc_ref[:, D_RET:2 * D_RET] = (rotary(proj(_C_K)) * (HEAD_DIM ** -0.5)).astype(BF16)
    pc_ref[:, 2 * D_RET:4 * D_RET] = proj(_C_VG).astype(BF16)


def _inproj(x2, gpre, w, bf, qg, kg, cos, sin, tri, ones, seq_len):
    n = x2.shape[0]
    batch = n // seq_len
    tm = TM_PROJ
    tiles_per_seq = seq_len // tm
    const = lambda i: (0, 0)
    row = lambda i: (i, 0)
    seq_t = lambda i: (i // tiles_per_seq, 0, i % tiles_per_seq)
    return pl.pallas_call(
        functools.partial(_inproj_kernel, tiles_per_seq=tiles_per_seq),
        grid=(n // tm,),
        in_specs=[
            pl.BlockSpec((tm, D_MODEL), row),
            pl.BlockSpec((1, D_MODEL), const),
            pl.BlockSpec((D_MODEL, IN_COLS_PADDED), const),
            pl.BlockSpec((1, FGATE_PAD), const),
            pl.BlockSpec((1, D_FOX), const),
            pl.BlockSpec((1, D_FOX), const),
            pl.BlockSpec((tm, D_RET), lambda i: (i % tiles_per_seq, 0)),
            pl.BlockSpec((tm, D_RET), lambda i: (i % tiles_per_seq, 0)),
            pl.BlockSpec((CUM_BLOCK, CUM_BLOCK), const),
            pl.BlockSpec((LANES, LANES), const),
        ],
        out_specs=[
            pl.BlockSpec((tm, 3 * D_HGRN), row),
            pl.BlockSpec((tm, D_HGRN), row),
            pl.BlockSpec((1, H_FOX * LANES, tm), seq_t),
            pl.BlockSpec((tm, H_FOX * LANES), row),
            pl.BlockSpec((1, H_FOX * VT_ROWS, tm), seq_t),
            pl.BlockSpec((tm, D_FOX), row),
            pl.BlockSpec((tm, 4 * D_RET), row),
        ],
        out_shape=[
            jax.ShapeDtypeStruct((n, 3 * D_HGRN), BF16),
            jax.ShapeDtypeStruct((n, D_HGRN), F32),
            jax.ShapeDtypeStruct((batch, H_FOX * LANES, seq_len), BF16),
            jax.ShapeDtypeStruct((n, H_FOX * LANES), BF16),
            jax.ShapeDtypeStruct((batch, H_FOX * VT_ROWS, seq_len), BF16),
            jax.ShapeDtypeStruct((n, D_FOX), BF16),
            jax.ShapeDtypeStruct((n, 4 * D_RET), BF16),
        ],
        scratch_shapes=[pltpu.VMEM((1, FGATE_PAD), F32)],
        compiler_params=pltpu.CompilerParams(
            dimension_semantics=("arbitrary",), vmem_limit_bytes=VMEM_LIMIT),
        name="inproj",
    )(x2, gpre, w, bf, qg, kg, cos, sin, tri, ones)


_M_CUM, _M_AFTER, _M_DIAG, _M_MERGE0 = 0, 1, 2, 3
_MERGE_BLOCKS = (2 * SAFE_DIAG, 4 * SAFE_DIAG, 8 * SAFE_DIAG)
assert _MERGE_BLOCKS[-1] == SUB_HGRN


def _hgrn_matrices(tt):
    t = jnp.arange(tt)[:, None]
    u = jnp.arange(tt)[None, :]
    same = lambda m: (t // m) == (u // m)
    mats = [same(SUB_HGRN) & (u <= t), same(SUB_HGRN) & (u > t),
            same(SAFE_DIAG) & (u <= t) & (u > (t // SAFE_DIAG) * SAFE_DIAG)]
    for m in _MERGE_BLOCKS:
        mid = (t // m) * m + m // 2 - 1
        mats.append(same(m) & (((t > mid) & (u > mid) & (u <= t)) | ((t <= mid) & (u > t) & (u <= mid))))
    return jnp.stack(mats).astype(BF16)


def _hgrn_kernel(q_ref, v_ref, g_ref, z_ref, lb_ref, gain_ref, mats_ref, ones_ref, o_ref, st_ref,
                 logf_scr, k_scr, cum_scr, *, layer):
    @pl.when(pl.program_id(1) == 0)
    def _():
        st_ref[...] = jnp.zeros_like(st_ref)

    nb, tt = q_ref.shape[0], q_ref.shape[1]
    sub = SUB_HGRN

    def range_sum(which, log_f):
        return jnp.concatenate([_dot_01_lhs(mats_ref[which], log_f[r:r + MAT_BLOCK])
                                for r in range(0, tt, MAT_BLOCK)], axis=0)

    lb = lb_ref[...]
    e = jnp.exp(lb - jnp.max(lb, axis=0, keepdims=True))
    p = e / jnp.sum(e, axis=0, keepdims=True)
    run = p[0:1]
    for i in range(1, layer + 1):
        run = run + p[i:i + 1]
    lower = run - p[0:1]

    lowest = None
    for b in range(nb):
        sg = _sigmoid(z_ref[b])
        f = lower + (1.0 - lower) * sg
        log_f = jnp.log(jnp.maximum(f, MIN_FORGET))
        cum = range_sum(_M_CUM, log_f)
        logf_scr[b] = log_f
        k_scr[b] = (1.0 - lower) * (1.0 - sg)
        cum_scr[b] = cum
        low_b = jnp.min(cum)
        lowest = low_b if lowest is None else jnp.minimum(lowest, low_b)
    exact_fallback = lowest < -FAST_PATH_MAX_DECAY

    row = lax.broadcasted_iota(jnp.int32, (MAT_BLOCK, MAT_BLOCK), 0)
    col = lax.broadcasted_iota(jnp.int32, (MAT_BLOCK, MAT_BLOCK), 1)

    def diag_mask(m):
        in_block = row & (m - 1)
        dist = row - col
        return ((in_block - dist) | dist) >= 0

    def merge_mask(m):
        half = m // 2
        return (((row ^ col) & ~(m - 1)) | ((row & half) ^ half) | (col & half)) == 0

    first_blk = _first_head_lanes((MAT_BLOCK, LANES))
    rr = lax.broadcasted_iota(jnp.int32, (LANES, LANES), 0)
    cc = lax.broadcasted_iota(jnp.int32, (LANES, LANES), 1)
    same_head = ((rr ^ cc) & HEAD_DIM) == 0
    ones = ones_ref[...]

    def factors(c, exact):
        b, sl = c["b"], c["sl"]
        q = q_ref[b, :, sl].astype(F32)
        k = k_scr[b, :, sl]
        cum = cum_scr[b, :, sl]
        c["v"] = v_ref[b, :, sl]
        c["e_q"] = jnp.exp(cum)
        c["qt"] = (q * c["e_q"]).astype(BF16)
        if exact:
            log_f = logf_scr[b, :, sl]
            since = range_sum(_M_DIAG, log_f)
            pieces = [((q * jnp.exp(since)).astype(BF16), (k * jnp.exp(-since)).astype(BF16),
                       diag_mask(SAFE_DIAG))]
            for idx, m in enumerate(_MERGE_BLOCKS):
                w = jnp.exp(range_sum(_M_MERGE0 + idx, log_f))
                pieces.append(((q * w).astype(BF16), (k * w).astype(BF16), merge_mask(m)))
            c["k_state"] = (k * jnp.exp(range_sum(_M_AFTER, log_f))).astype(BF16)
        else:
            c["k_state"] = (k * jnp.exp(-cum)).astype(BF16)
            pieces = [(c["qt"], c["k_state"], diag_mask(sub))]
        c["pieces"] = pieces

    def intra_scores(c):
        c["s"] = []
        for r0 in range(0, tt, MAT_BLOCK):
            r = slice(r0, r0 + MAT_BLOCK)
            per_head = []
            for keep in (first_blk, ~first_blk):
                s = jnp.zeros((MAT_BLOCK, MAT_BLOCK), F32)
                for q_f, k_f, mask in c["pieces"]:
                    q_h = jnp.where(keep, q_f[r], jnp.zeros_like(q_f[r]))
                    s = jnp.where(mask, _dot_nt(q_h, k_f[r]), s)
                per_head.append(s.astype(BF16))
            c["s"].append(per_head)

    def intra_values(c):
        blocks = []
        for i, r0 in enumerate(range(0, tt, MAT_BLOCK)):
            r = slice(r0, r0 + MAT_BLOCK)
            blocks.append(jnp.where(first_blk, _dot(c["s"][i][0], c["v"][r]),
                                    _dot(c["s"][i][1], c["v"][r])))
        c["o"] = jnp.concatenate(blocks, axis=0)

    def state_updates(c):
        c["upd"] = [jnp.where(same_head, _dot_tn(c["v"][j * sub:(j + 1) * sub],
                                                 c["k_state"][j * sub:(j + 1) * sub]), 0.0)
                    for j in range(tt // sub)]

    def state_chain(c, exact):
        st = st_ref[c["b"], c["pair"]]
        outs = []
        for j in range(tt // sub):
            r = slice(j * sub, (j + 1) * sub)
            outs.append(c["o"][r] + _dot_nt(c["qt"][r], st.astype(BF16)))
            decay = c["e_q"][(j + 1) * sub - 1:(j + 1) * sub, :]
            st = decay * st + c["upd"][j] if exact else decay * (st + c["upd"][j])
        st_ref[c["b"], c["pair"]] = st
        c["o"] = jnp.concatenate(outs, axis=0)

    def head_norm_and_gate(c):
        b, sl = c["b"], c["sl"]
        o = c["o"] * lax.rsqrt(_head_mean_sq(c["o"], ones) + EPS) * gain_ref[:, sl]
        g = g_ref[b, :, sl].astype(F32)
        o_ref[b, :, sl] = (o * (g * _sigmoid(g))).astype(BF16)

    def all_tiles(exact):
        tiles = [dict(b=b, pair=pair, sl=slice(pair * LANES, (pair + 1) * LANES))
                 for b in range(nb) for pair in range(H_HGRN // 2)]
        for stage in (functools.partial(factors, exact=exact), intra_scores, intra_values,
                      state_updates, functools.partial(state_chain, exact=exact),
                      head_norm_and_gate):
            for c in tiles:
                stage(c)

    pl.when(exact_fallback)(functools.partial(all_tiles, True))
    pl.when(jnp.logical_not(exact_fallback))(functools.partial(all_tiles, False))


def _hgrn(pa, za, lb, gain, mats, ones, batch, seq_len, layer):
    n = pa.shape[0]
    tt = TT_HGRN
    nb = NB_HGRN
    const = lambda b, t: (0, 0)
    tile_f32 = pltpu.VMEM((nb, tt, D_HGRN), F32)
    pa3 = pa.reshape(batch, seq_len, 3 * D_HGRN)
    blk = lambda j: pl.BlockSpec((nb, tt, D_HGRN), lambda b, t: (b, t, j))
    out = pl.pallas_call(
        functools.partial(_hgrn_kernel, layer=layer),
        grid=(batch // nb, seq_len // tt),
        in_specs=[
            blk(0), blk(1), blk(2), blk(0),
            pl.BlockSpec(lb.shape, const),
            pl.BlockSpec((1, D_HGRN), const),
            pl.BlockSpec(mats.shape, lambda b, t: (0, 0, 0)),
            pl.BlockSpec((LANES, LANES), const),
        ],
        out_specs=blk(0),
        out_shape=jax.ShapeDtypeStruct((batch, seq_len, D_HGRN), BF16),
        scratch_shapes=[pltpu.VMEM((nb, H_HGRN // 2, LANES, LANES), F32),
                        tile_f32, tile_f32, tile_f32],
        compiler_params=pltpu.CompilerParams(
            dimension_semantics=("arbitrary", "arbitrary"), vmem_limit_bytes=VMEM_LIMIT),
        name="hgrn",
    )(pa3, pa3, pa3, za.reshape(batch, seq_len, D_HGRN), lb, gain, mats, ones)
    return out.reshape(n, D_HGRN)


def _ret_kernel(q_ref, k_ref, v_ref, g_ref, dec_ref, qs_ref, ks_ref, cd_ref, gain_ref, ones_ref,
                o_ref, st_ref):
    @pl.when(pl.program_id(1) == 0)
    def _():
        st_ref[...] = jnp.zeros_like(st_ref)

    nb, c = q_ref.shape[0], q_ref.shape[1]
    first = _first_head_lanes((c, LANES))
    rr = lax.broadcasted_iota(jnp.int32, (LANES, LANES), 0)
    cc = lax.broadcasted_iota(jnp.int32, (LANES, LANES), 1)
    same_head = ((rr ^ cc) & HEAD_DIM) == 0
    ones = ones_ref[...]

    def scores(c):
        q, k = q_ref[c["b"], :, c["sl"]], k_ref[c["b"], :, c["sl"]]
        c["s"] = [(_dot_nt(jnp.where(keep, q, jnp.zeros_like(q)), k)
                   * dec_ref[2 * c["pair"] + hh]).astype(BF16)
                  for hh, keep in enumerate((first, ~first))]

    def values_and_state(c):
        b, pair, sl = c["b"], c["pair"], c["sl"]
        q, k, v = q_ref[b, :, sl], k_ref[b, :, sl], v_ref[b, :, sl]
        st = st_ref[b, pair]
        q_dec = (q.astype(F32) * qs_ref[:, sl]).astype(BF16)
        c["o"] = (jnp.where(first, _dot(c["s"][0], v), _dot(c["s"][1], v))
                  + _dot_nt(q_dec, st.astype(BF16)))
        k_dec = (k.astype(F32) * ks_ref[:, sl]).astype(BF16)
        st_ref[b, pair] = cd_ref[:, sl] * st + jnp.where(same_head, _dot_tn(v, k_dec), 0.0)

    def head_norm_and_gate(c):
        b, sl = c["b"], c["sl"]
        o = c["o"] * lax.rsqrt(_head_mean_sq(c["o"], ones) + EPS) * gain_ref[:, sl]
        g = g_ref[b, :, sl].astype(F32)
        o_ref[b, :, sl] = (o * (g * _sigmoid(g))).astype(BF16)

    tiles = [dict(b=b, pair=pair, sl=slice(pair * LANES, (pair + 1) * LANES))
             for b in range(nb) for pair in range(H_RET // 2)]
    for stage in (scores, values_and_state, head_norm_and_gate):
        for c in tiles:
            stage(c)


def _retention(pc, dec, qs, ks, cd, gain, ones, batch, seq_len):
    n = pc.shape[0]
    c = C_RET
    nb = NB_RET
    const2 = lambda b, t: (0, 0)
    pc3 = pc.reshape(batch, seq_len, 4 * D_RET)
    blk = lambda j: pl.BlockSpec((nb, c, D_RET), lambda b, t: (b, t, j))
    out = pl.pallas_call(
        _ret_kernel,
        grid=(batch // nb, seq_len // c),
        in_specs=[
            blk(0), blk(1), blk(2), blk(3),
            pl.BlockSpec((H_RET, c, c), lambda b, t: (0, 0, 0)),
            pl.BlockSpec((c, D_RET), const2),
            pl.BlockSpec((c, D_RET), const2),
            pl.BlockSpec((1, D_RET), const2),
            pl.BlockSpec((1, D_RET), const2),
            pl.BlockSpec((LANES, LANES), const2),
        ],
        out_specs=blk(0),
        out_shape=jax.ShapeDtypeStruct((batch, seq_len, D_RET), BF16),
        scratch_shapes=[pltpu.VMEM((nb, H_RET // 2, LANES, LANES), F32)],
        compiler_params=pltpu.CompilerParams(
            dimension_semantics=("arbitrary", "arbitrary"), vmem_limit_bytes=VMEM_LIMIT),
        name="retention",
    )(pc3, pc3, pc3, pc3, dec, qs, ks, cd, gain, ones)
    return out.reshape(n, D_RET)


def _fox_kernel(qt_ref, k_ref, vt_ref, g_ref, o_ref, s_scr, p_scr, acc_scr):
    tq = TQ_FOX
    nq = o_ref.shape[0] // tq
    blocks = [(qi, kj) for qi in range(nq) for kj in range(qi + 1)]
    key = lax.broadcasted_iota(jnp.int32, (tq, tq), 0)
    qry = lax.broadcasted_iota(jnp.int32, (tq, tq), 1)
    causal = qry >= key

    def scores(t):
        qi, kj = blocks[t]
        for hh in range(2):
            slot = slice(hh * LANES, (hh + 1) * LANES)
            s_scr[t % 2, hh] = _dot(k_ref[kj * tq:(kj + 1) * tq, slot],
                                    qt_ref[0, slot, qi * tq:(qi + 1) * tq])

    def softmax(t, m_old):
        qi, kj = blocks[t]
        m_new, alpha = [], []
        for hh in range(2):
            s = s_scr[t % 2, hh]
            if kj == qi:
                s = jnp.where(causal, s, MASK_VALUE)
            m_blk = jnp.max(s, axis=0, keepdims=True)
            if kj > 0:
                m_blk = jnp.maximum(m_old[hh], m_blk)
                alpha.append(jnp.exp2(m_old[hh] - m_blk))
            p_scr[t % 2, hh] = jnp.exp2(s - m_blk).astype(BF16)
            m_new.append(m_blk)
        return m_new, alpha

    def accumulate(t, alpha):
        qi, kj = blocks[t]
        for hh in range(2):
            pv = _dot(vt_ref[0, hh * VT_ROWS:(hh + 1) * VT_ROWS, kj * tq:(kj + 1) * tq],
                      p_scr[t % 2, hh])
            acc_scr[qi % 2, hh] = pv if kj == 0 else alpha[hh] * acc_scr[qi % 2, hh] + pv
        if kj == qi:
            heads = []
            for hh in range(2):
                total = acc_scr[qi % 2, hh]
                heads.append(total[:HEAD_DIM] / total[HEAD_DIM:HEAD_DIM + 1])
            o_t = jnp.concatenate(heads, axis=0)
            rows = slice(qi * tq, (qi + 1) * tq)
            o_ref[rows, :] = (o_t.T * _sigmoid(g_ref[rows, :].astype(F32))).astype(BF16)

    scores(0)
    m_run, alpha_prev = None, None
    for t in range(len(blocks)):
        m_run, alpha_t = softmax(t, m_run)
        if t + 1 < len(blocks):
            scores(t + 1)
        if t > 0:
            accumulate(t - 1, alpha_prev)
        alpha_prev = alpha_t
    accumulate(len(blocks) - 1, alpha_prev)


def _fox(qt, ka, vt, gb, batch, seq_len):
    n = ka.shape[0]
    tq = TQ_FOX
    pairs = H_FOX // 2
    return pl.pallas_call(
        _fox_kernel,
        grid=(batch, pairs),
        in_specs=[
            pl.BlockSpec((1, 2 * LANES, seq_len), lambda b, p: (b, p, 0)),
            pl.BlockSpec((seq_len, 2 * LANES), lambda b, p: (b, p)),
            pl.BlockSpec((1, 2 * VT_ROWS, seq_len), lambda b, p: (b, p, 0)),
            pl.BlockSpec((seq_len, LANES), lambda b, p: (b, p)),
        ],
        out_specs=pl.BlockSpec((seq_len, LANES), lambda b, p: (b, p)),
        out_shape=jax.ShapeDtypeStruct((n, D_FOX), BF16),
        scratch_shapes=[
            pltpu.VMEM((2, 2, tq, tq), F32),
            pltpu.VMEM((2, 2, tq, tq), BF16),
            pltpu.VMEM((2, 2, VT_ROWS, tq), F32),
        ],
        compiler_params=pltpu.CompilerParams(
            dimension_semantics=("arbitrary", "arbitrary"), vmem_limit_bytes=VMEM_LIMIT),
        name="fox",
    )(qt, ka, vt, gb)


def _mix_ffn_kernel(oa_ref, ob_ref, oc_ref, wo_ref, gmix_ref, x_ref, gpre_ref,
                    wup_ref, cw_ref, cb_ref, wd_ref, gpost_ref, out_ref, xp_scr, act_scr, tail_scr,
                    *, tiles_per_seq):
    i = pl.program_id(0)
    tm = x_ref.shape[0]
    fc = FC_FFN

    mixer_out = jnp.concatenate([oa_ref[...], ob_ref[...], oc_ref[...]], axis=1)
    x_mid = x_ref[...] + _rmsnorm_rows(_dot(mixer_out, wo_ref[...]), gmix_ref[...])
    xp_scr[...] = pltpu.einshape("(ab)f->(ba)f", x_mid, a=SUBLANES)
    h = _rmsnorm_rows(xp_scr[...], gpre_ref[...]).astype(BF16)

    @pl.when(i == 0)
    def _():
        tail_scr[...] = jnp.zeros_like(tail_scr)

    seq_start = i % tiles_per_seq == 0
    first_sublane = lax.broadcasted_iota(jnp.int32, (SUBLANES, fc), 0) == 0

    def causal_conv(u, col, slot):
        prev = jnp.where(seq_start, 0.0, tail_scr[slot])
        last = u[tm - 2 * SUBLANES:, :]
        tail_scr[slot] = last
        head = []
        for g in range(2):
            rows = slice(g * SUBLANES, (g + 1) * SUBLANES)
            entering = prev[(g + 1) * SUBLANES - 1:(g + 1) * SUBLANES, :]
            head.append(jnp.where(first_sublane, entering, pltpu.roll(last[rows], 1, 0)))
        u1 = jnp.concatenate([head[1], u[:tm - SUBLANES]], axis=0)
        u2 = jnp.concatenate([head[0], head[1], u[:tm - 2 * SUBLANES]], axis=0)
        cols = slice(col, col + fc)
        return (cb_ref[:, cols] + cw_ref[0:1, cols] * u2 + cw_ref[1:2, cols] * u1
                + cw_ref[2:3, cols] * u)

    for c in range(D_FF // fc):
        gate = causal_conv(_dot(h, wup_ref[:, c * fc:(c + 1) * fc]), c * fc, 2 * c)
        val = causal_conv(_dot(h, wup_ref[:, D_FF + c * fc:D_FF + (c + 1) * fc]),
                          D_FF + c * fc, 2 * c + 1)
        act_scr[:, c * fc:(c + 1) * fc] = (gate * _sigmoid(gate) * val).astype(BF16)

    y = _dot(act_scr[...], wd_ref[...])
    res = xp_scr[...] + _rmsnorm_rows(y, gpost_ref[...])
    out_ref[...] = pltpu.einshape("(ba)f->(ab)f", res, a=SUBLANES)


def _mix_ffn(oa, ob, oc, wo, gmix, x2, gpre, w_up, conv_w, conv_b, w_down, gpost, seq_len):
    n = x2.shape[0]
    tm = TM_FFN
    nf = D_FF // FC_FFN
    tiles_per_seq = seq_len // tm
    row = lambda i: (i, 0)
    resident = lambda shape: pl.BlockSpec(shape, lambda i: (0, 0), pipeline_mode=pl.Buffered(1))
    return pl.pallas_call(
        functools.partial(_mix_ffn_kernel, tiles_per_seq=tiles_per_seq),
        grid=(n // tm,),
        in_specs=[
            pl.BlockSpec((tm, D_HGRN), row),
            pl.BlockSpec((tm, D_FOX), row),
            pl.BlockSpec((tm, D_RET), row),
            resident((D_HGRN + D_FOX + D_RET, D_MODEL)),
            resident((1, D_MODEL)),
            pl.BlockSpec((tm, D_MODEL), row),
            resident((1, D_MODEL)),
            resident((D_MODEL, 2 * D_FF)),
            resident((CONV_W, 2 * D_FF)),
            resident((1, 2 * D_FF)),
            resident((D_FF, D_MODEL)),
            resident((1, D_MODEL)),
        ],
        out_specs=pl.BlockSpec((tm, D_MODEL), row),
        out_shape=jax.ShapeDtypeStruct((n, D_MODEL), F32),
        scratch_shapes=[
            pltpu.VMEM((tm, D_MODEL), F32),
            pltpu.VMEM((tm, D_FF), BF16),
            pltpu.VMEM((2 * nf, 2 * SUBLANES, FC_FFN), F32),
        ],
        compiler_params=pltpu.CompilerParams(
            dimension_semantics=("arbitrary",), vmem_limit_bytes=VMEM_LIMIT),
        name="mix_ffn",
    )(oa, ob, oc, wo, gmix, x2, gpre, w_up, conv_w, conv_b, w_down, gpost)


def _block_ones(n):
    idx = jnp.arange(n) // HEAD_DIM
    return (idx[:, None] == idx[None, :]).astype(BF16)


def _lower_tri(n, block):
    r = jnp.arange(n)
    return ((r[:, None] >= r[None, :]) & (r[:, None] // block == r[None, :] // block)).astype(BF16)


def _rotary_tables(seq_len):
    inv_freq = 1.0 / (ROPE_BASE ** (jnp.arange(0, HEAD_DIM, 2, dtype=F32) / HEAD_DIM))
    ang = jnp.arange(seq_len, dtype=F32)[:, None] * inv_freq[None, :]
    cos, sin = jnp.cos(ang), jnp.sin(ang)
    cos_full = jnp.tile(jnp.concatenate([cos, cos], axis=-1), (1, H_RET))
    sin_signed = jnp.tile(jnp.concatenate([-sin, sin], axis=-1), (1, H_RET))
    return cos_full, sin_signed


def _retention_tables(c):
    log_gamma = jnp.log1p(-jnp.exp2(-5.0 - jnp.arange(H_RET, dtype=F32)))
    pos = jnp.arange(c, dtype=F32)
    rel = pos[:, None] - pos[None, :]
    dec = jnp.where(rel >= 0, jnp.exp(log_gamma[:, None, None] * jnp.maximum(rel, 0.0)), 0.0)
    per_lane = lambda a: jnp.repeat(a, HEAD_DIM, axis=-1)
    qs = per_lane(jnp.exp(log_gamma[None, :] * (pos[:, None] + 1.0)))
    ks = per_lane(jnp.exp(log_gamma[None, :] * (c - 1.0 - pos[:, None])))
    cd = per_lane(jnp.exp(log_gamma * c)[None, :])
    return dec, qs, ks, cd


def _reorder_in_weight_rows(w_t):
    widths = [D_HGRN] * 4 + [D_FOX] * 3 + [H_FOX] + [D_FOX] + [D_RET] * 4
    offs = [0]
    for wd in widths:
        offs.append(offs[-1] + wd)
    part = lambda j: w_t[offs[j]:offs[j + 1]]
    a_q, a_f, a_i, a_g, b_q, b_k, b_v, b_f, b_g, c_q, c_k, c_v, c_g = (part(j) for j in range(13))
    b_f = jnp.pad(b_f, ((0, FGATE_PAD - H_FOX), (0, 0)))
    return jnp.concatenate(
        [a_q, a_i, a_g, a_f, b_q, b_k, b_v, b_g, b_f, c_q, c_k, c_v, c_g], axis=0)


def _transpose_cast_kernel(w_ref, o_ref):
    o_ref[...] = w_ref[...].T.astype(BF16)


def _transpose_cast(w_t):
    rows = w_t.shape[0]
    blk = D_HGRN
    return pl.pallas_call(
        _transpose_cast_kernel,
        grid=(rows // blk,),
        in_specs=[pl.BlockSpec((blk, D_MODEL), lambda i: (i, 0))],
        out_specs=pl.BlockSpec((D_MODEL, blk), lambda i: (0, i)),
        out_shape=jax.ShapeDtypeStruct((D_MODEL, rows), BF16),
        compiler_params=pltpu.CompilerParams(
            dimension_semantics=("arbitrary",), vmem_limit_bytes=VMEM_LIMIT),
        name="wprep",
    )(w_t)


def kernel(x, w_in, b_fox_f, fox_q_gain, fox_k_gain, hgrn_lb, hgrn_out_gain, ret_out_gain, w_out,
           g_mix_pre, g_mix_post, w_up, conv_w, conv_b, w_down, g_ffn_pre, g_ffn_post):
    batch, seq_len, d_model = x.shape
    assert d_model == D_MODEL
    assert seq_len % TM_FFN == 0 and seq_len % TQ_FOX == 0
    assert batch % NB_HGRN == 0 and batch % NB_RET == 0
    depth = w_in.shape[0]
    n = batch * seq_len

    cos_full, sin_signed = _rotary_tables(seq_len)
    dec, qs, ks, cd = _retention_tables(C_RET)
    tri_proj = _lower_tri(CUM_BLOCK, CUM_BLOCK)
    mats_hgrn = _hgrn_matrices(MAT_BLOCK)
    ones_pair = _block_ones(LANES)
    lb = hgrn_lb.astype(F32)

    w_in_t = jnp.transpose(w_in, (2, 0, 1))

    x2 = x.reshape(n, D_MODEL)
    for l in range(depth):
        w = _transpose_cast(_reorder_in_weight_rows(w_in_t[:, l, :]))
        bf = jnp.pad(b_fox_f[l].astype(F32), (0, FGATE_PAD - H_FOX))[None, :]
        qg = jnp.tile(fox_q_gain[l].astype(F32), H_FOX)[None, :]
        kg = jnp.tile(fox_k_gain[l].astype(F32), H_FOX)[None, :]
        pa, za, qt, ka, vt, gb, pc = _inproj(x2, g_mix_pre[l][None, :], w, bf, qg, kg, cos_full,
                                             sin_signed, tri_proj, ones_pair, seq_len)

        o_a = _hgrn(pa, za, lb, hgrn_out_gain[l].reshape(1, D_HGRN), mats_hgrn, ones_pair,
                    batch, seq_len, l)
        o_b = _fox(qt, ka, vt, gb, batch, seq_len)
        o_c = _retention(pc, dec, qs, ks, cd, ret_out_gain[l].reshape(1, D_RET), ones_pair,
                         batch, seq_len)

        x2 = _mix_ffn(o_a, o_b, o_c, w_out[l].astype(BF16),
                      g_mix_post[l][None, :], x2, g_ffn_pre[l][None, :], w_up[l].astype(BF16),
                      conv_w[l], conv_b[l][None, :], w_down[l].astype(BF16),
                      g_ffn_post[l][None, :], seq_len)
    return x2.reshape(batch, seq_len, D_MODEL)
```

```python
import functools

import jax
import jax.numpy as jnp
from jax import lax
from jax.experimental import pallas as pl
from jax.experimental.pallas import tpu as pltpu

F32 = jnp.float32
BF16 = jnp.bfloat16

D_MODEL = 1024
HEAD_DIM = 64
H_HGRN, H_FOX, H_RET = 6, 6, 4
D_HGRN, D_FOX, D_RET = H_HGRN * HEAD_DIM, H_FOX * HEAD_DIM, H_RET * HEAD_DIM
D_FF = 2816
CONV_W = 3
ROPE_BASE = 10000.0
EPS = 1e-6
MIN_FORGET = 1e-12
MASK_VALUE = -1e30

LANES = 128
SUBLANES = 8
FGATE_PAD = LANES
BF16_SUBLANES = 16
VT_ROWS = HEAD_DIM + BF16_SUBLANES
LOG2E = 1.4426950408889634

_A_MAIN = (0, 3 * D_HGRN)
_A_GATE = (_A_MAIN[1], _A_MAIN[1] + D_HGRN)
_B_Q = (_A_GATE[1], _A_GATE[1] + D_FOX)
_B_K = (_B_Q[1], _B_Q[1] + D_FOX)
_B_V = (_B_K[1], _B_K[1] + D_FOX)
_B_G = (_B_V[1], _B_V[1] + D_FOX)
_B_F = (_B_G[1], _B_G[1] + FGATE_PAD)
_C_Q = (_B_F[1], _B_F[1] + D_RET)
_C_K = (_C_Q[1], _C_Q[1] + D_RET)
_C_VG = (_C_K[1], _C_K[1] + 2 * D_RET)
IN_COLS_PADDED = _C_VG[1]

TM_PROJ = 512
CUM_BLOCK = 128
TM_FFN = 1024
FC_FFN = 256
TT_HGRN = 256
NB_HGRN = 8
SUB_HGRN = 32
MAT_BLOCK = 128
SAFE_DIAG = 4
FAST_PATH_MAX_DECAY = 60.0
C_RET = 256
NB_RET = 8
TQ_FOX = 512
VMEM_LIMIT = 56 * 1024 * 1024


def _dot(a, b):
    return jnp.dot(a, b, preferred_element_type=F32)


def _dot_nt(a, b):
    return lax.dot_general(a, b, (((1,), (1,)), ((), ())), preferred_element_type=F32)


def _dot_tn(a, b):
    return lax.dot_general(a, b, (((0,), (0,)), ((), ())), preferred_element_type=F32)


def _dot_01_lhs(m01, x):
    hi = x.astype(BF16)
    r = x - hi.astype(F32)
    mid = r.astype(BF16)
    lo = (r - mid.astype(F32)).astype(BF16)
    return _dot(m01, hi) + _dot(m01, mid) + _dot(m01, lo)


def _head_mean_sq(y, ones_bd):
    sq = y * y
    hi = sq.astype(BF16)
    lo = (sq - hi.astype(F32)).astype(BF16)
    return (_dot(hi, ones_bd) + _dot(lo, ones_bd)) * (1.0 / HEAD_DIM)


def _rmsnorm_rows(x, gain):
    return x * lax.rsqrt(jnp.mean(x * x, axis=-1, keepdims=True) + EPS) * gain


def _sigmoid(x):
    return 1.0 / (1.0 + jnp.exp(-x))


def _first_head_lanes(shape):
    return (lax.broadcasted_iota(jnp.int32, shape, len(shape) - 1) & HEAD_DIM) == 0


def _bf16_terms(x):
    hi = x.astype(BF16).astype(F32)
    mid = (x - hi).astype(BF16).astype(F32)
    lo = (x - hi - mid).astype(BF16).astype(F32)
    return hi, mid, lo


def _inproj_kernel(x_ref, gpre_ref, w_ref, bf_ref, qg_ref, kg_ref, cos_ref, sin_ref, tri_ref,
                   ones_ref, pa_ref, za_ref, qt_ref, ka_ref, vt_ref, gb_ref, pc_ref, carry_ref,
                   *, tiles_per_seq):
    @pl.when(pl.program_id(0) % tiles_per_seq == 0)
    def _():
        carry_ref[...] = jnp.zeros_like(carry_ref)

    h = _rmsnorm_rows(x_ref[...], gpre_ref[...]).astype(BF16)
    tm = h.shape[0]

    group_ends = (_A_GATE[1], _B_K[1], _B_G[1], IN_COLS_PADDED)
    group_result = {}

    def proj(cols):
        start = 0
        for end in group_ends:
            if cols[1] <= end:
                break
            start = end
        if start not in group_result:
            group_result[start] = _dot(h, w_ref[:, start:end])
        return group_result[start][:, cols[0] - start:cols[1] - start]

    zf = proj(_B_F) + bf_ref[...]
    log_f = -(jnp.maximum(-zf, 0.0) + jnp.log1p(jnp.exp(-jnp.abs(zf))))
    running = carry_ref[...]
    blocks = []
    for r in range(0, tm, CUM_BLOCK):
        blocks.append(_dot_01_lhs(tri_ref[...], log_f[r:r + CUM_BLOCK]) + running)
        running = blocks[-1][CUM_BLOCK - 1:, :]
    cum = jnp.concatenate(blocks, axis=0)
    carry_ref[...] = running

    ones = ones_ref[...]
    bq = proj(_B_Q)
    bk = proj(_B_K)
    lane = lax.broadcasted_iota(jnp.int32, (tm, LANES), 1)
    is_feature = lane < HEAD_DIM
    is_one_q = (lane >= HEAD_DIM + 3) & (lane < HEAD_DIM + 6)
    is_one_k = (lane >= HEAD_DIM) & (lane < HEAD_DIM + 3)
    for pair in range(H_FOX // 2):
        sl = slice(pair * LANES, (pair + 1) * LANES)
        q2 = bq[:, sl]
        q2 = (q2 * lax.rsqrt(_head_mean_sq(q2, ones) + EPS) * qg_ref[:, sl]
              * (HEAD_DIM ** -0.5 * LOG2E))
        k2 = bk[:, sl]
        k2 = k2 * lax.rsqrt(_head_mean_sq(k2, ones) + EPS) * kg_ref[:, sl]
        for odd in range(2):
            head = 2 * pair + odd
            slot = slice(head * LANES, (head + 1) * LANES)
            qh = pltpu.roll(q2, HEAD_DIM, 1) if odd else q2
            kh = pltpu.roll(k2, HEAD_DIM, 1) if odd else k2
            hi, mid, lo = _bf16_terms(
                jnp.broadcast_to(cum[:, head:head + 1], (tm, LANES)) * LOG2E)
            q_bias = jnp.where(lane == HEAD_DIM, hi, jnp.where(lane == HEAD_DIM + 1, mid, jnp.where(
                lane == HEAD_DIM + 2, lo, jnp.where(is_one_q, 1.0, 0.0))))
            k_bias = jnp.where(lane == HEAD_DIM + 3, -hi, jnp.where(lane == HEAD_DIM + 4, -mid, jnp.where(
                lane == HEAD_DIM + 5, -lo, jnp.where(is_one_k, 1.0, 0.0))))
            qt_ref[0, slot, :] = jnp.where(is_feature, qh, q_bias).T.astype(BF16)
            ka_ref[:, slot] = jnp.where(is_feature, kh, k_bias).astype(BF16)
    pa_ref[...] = proj(_A_MAIN).astype(BF16)
    za_ref[...] = proj(_A_GATE)

    v_t = proj(_B_V).T.astype(BF16)
    ones_row = jnp.where(lax.broadcasted_iota(jnp.int32, (BF16_SUBLANES, tm), 0) == 0,
                         1.0, 0.0).astype(BF16)
    for head in range(H_FOX):
        vt_ref[0, head * VT_ROWS:head * VT_ROWS + HEAD_DIM, :] = (
            v_t[head * HEAD_DIM:(head + 1) * HEAD_DIM, :])
        vt_ref[0, head * VT_ROWS + HEAD_DIM:(head + 1) * VT_ROWS, :] = ones_row
    gb_ref[...] = proj(_B_G).astype(BF16)

    cos = cos_ref[...]
    sin = sin_ref[...]
    low_half = (lax.broadcasted_iota(jnp.int32, cos.shape, 1) & (HEAD_DIM // 2)) == 0

    def rotary(y):
        swapped = jnp.where(low_half, pltpu.roll(y, D_RET - HEAD_DIM // 2, 1),
                            pltpu.roll(y, HEAD_DIM // 2, 1))
        return y * cos + swapped * sin

    pc_ref[:, 0:D_RET] = rotary(proj(_C_Q)).astype(BF16)
    pc_ref[:, D_RET:2 * D_RET] = (rotary(proj(_C_K)) * (HEAD_DIM ** -0.5)).astype(BF16)
    pc_ref[:, 2 * D_RET:4 * D_RET] = proj(_C_VG).astype(BF16)


def _inproj(x2, gpre, w, bf, qg, kg, cos, sin, tri, ones, seq_len):
    n = x2.shape[0]
    batch = n // seq_len
    tm = TM_PROJ
    tiles_per_seq = seq_len // tm
    const = lambda i: (0, 0)
    row = lambda i: (i, 0)
    seq_t = lambda i: (i // tiles_per_seq, 0, i % tiles_per_seq)
    return pl.pallas_call(
        functools.partial(_inproj_kernel, tiles_per_seq=tiles_per_seq),
        grid=(n // tm,),
        in_specs=[
            pl.BlockSpec((tm, D_MODEL), row),
            pl.BlockSpec((1, D_MODEL), const),
            pl.BlockSpec((D_MODEL, IN_COLS_PADDED), const),
            pl.BlockSpec((1, FGATE_PAD), const),
            pl.BlockSpec((1, D_FOX), const),
            pl.BlockSpec((1, D_FOX), const),
            pl.BlockSpec((tm, D_RET), lambda i: (i % tiles_per_seq, 0)),
            pl.BlockSpec((tm, D_RET), lambda i: (i % tiles_per_seq, 0)),
            pl.BlockSpec((CUM_BLOCK, CUM_BLOCK), const),
            pl.BlockSpec((LANES, LANES), const),
        ],
        out_specs=[
            pl.BlockSpec((tm, 3 * D_HGRN), row),
            pl.BlockSpec((tm, D_HGRN), row),
            pl.BlockSpec((1, H_FOX * LANES, tm), seq_t),
            pl.BlockSpec((tm, H_FOX * LANES), row),
            pl.BlockSpec((1, H_FOX * VT_ROWS, tm), seq_t),
            pl.BlockSpec((tm, D_FOX), row),
            pl.BlockSpec((tm, 4 * D_RET), row),
        ],
        out_shape=[
            jax.ShapeDtypeStruct((n, 3 * D_HGRN), BF16),
            jax.ShapeDtypeStruct((n, D_HGRN), F32),
            jax.ShapeDtypeStruct((batch, H_FOX * LANES, seq_len), BF16),
            jax.ShapeDtypeStruct((n, H_FOX * LANES), BF16),
            jax.ShapeDtypeStruct((batch, H_FOX * VT_ROWS, seq_len), BF16),
            jax.ShapeDtypeStruct((n, D_FOX), BF16),
            jax.ShapeDtypeStruct((n, 4 * D_RET), BF16),
        ],
        scratch_shapes=[pltpu.VMEM((1, FGATE_PAD), F32)],
        compiler_params=pltpu.CompilerParams(
            dimension_semantics=("arbitrary",), vmem_limit_bytes=VMEM_LIMIT),
        name="inproj",
    )(x2, gpre, w, bf, qg, kg, cos, sin, tri, ones)


_M_CUM, _M_AFTER, _M_DIAG, _M_MERGE0 = 0, 1, 2, 3
_MERGE_BLOCKS = (2 * SAFE_DIAG, 4 * SAFE_DIAG, 8 * SAFE_DIAG)
assert _MERGE_BLOCKS[-1] == SUB_HGRN


def _hgrn_matrices(tt):
    t = jnp.arange(tt)[:, None]
    u = jnp.arange(tt)[None, :]
    same = lambda m: (t // m) == (u // m)
    mats = [same(SUB_HGRN) & (u <= t), same(SUB_HGRN) & (u > t),
            same(SAFE_DIAG) & (u <= t) & (u > (t // SAFE_DIAG) * SAFE_DIAG)]
    for m in _MERGE_BLOCKS:
        mid = (t // m) * m + m // 2 - 1
        mats.append(same(m) & (((t > mid) & (u > mid) & (u <= t)) | ((t <= mid) & (u > t) & (u <= mid))))
    return jnp.stack(mats).astype(BF16)


def _hgrn_kernel(q_ref, v_ref, g_ref, z_ref, lb_ref, gain_ref, mats_ref, ones_ref, o_ref, st_ref,
                 logf_scr, k_scr, cum_scr, *, layer):
    @pl.when(pl.program_id(1) == 0)
    def _():
        st_ref[...] = jnp.zeros_like(st_ref)

    nb, tt = q_ref.shape[0], q_ref.shape[1]
    sub = SUB_HGRN

    def range_sum(which, log_f):
        return jnp.concatenate([_dot_01_lhs(mats_ref[which], log_f[r:r + MAT_BLOCK])
                                for r in range(0, tt, MAT_BLOCK)], axis=0)

    lb = lb_ref[...]
    e = jnp.exp(lb - jnp.max(lb, axis=0, keepdims=True))
    p = e / jnp.sum(e, axis=0, keepdims=True)
    run = p[0:1]
    for i in range(1, layer + 1):
        run = run + p[i:i + 1]
    lower = run - p[0:1]

    lowest = None
    for b in range(nb):
        sg = _sigmoid(z_ref[b])
        f = lower + (1.0 - lower) * sg
        log_f = jnp.log(jnp.maximum(f, MIN_FORGET))
        cum = range_sum(_M_CUM, log_f)
        logf_scr[b] = log_f
        k_scr[b] = (1.0 - lower) * (1.0 - sg)
        cum_scr[b] = cum
        low_b = jnp.min(cum)
        lowest = low_b if lowest is None else jnp.minimum(lowest, low_b)
    exact_fallback = lowest < -FAST_PATH_MAX_DECAY

    row = lax.broadcasted_iota(jnp.int32, (MAT_BLOCK, MAT_BLOCK), 0)
    col = lax.broadcasted_iota(jnp.int32, (MAT_BLOCK, MAT_BLOCK), 1)

    def diag_mask(m):
        in_block = row & (m - 1)
        dist = row - col
        return ((in_block - dist) | dist) >= 0

    def merge_mask(m):
        half = m // 2
        return (((row ^ col) & ~(m - 1)) | ((row & half) ^ half) | (col & half)) == 0

    first_blk = _first_head_lanes((MAT_BLOCK, LANES))
    rr = lax.broadcasted_iota(jnp.int32, (LANES, LANES), 0)
    cc = lax.broadcasted_iota(jnp.int32, (LANES, LANES), 1)
    same_head = ((rr ^ cc) & HEAD_DIM) == 0
    ones = ones_ref[...]

    def factors(c, exact):
        b, sl = c["b"], c["sl"]
        q = q_ref[b, :, sl].astype(F32)
        k = k_scr[b, :, sl]
        cum = cum_scr[b, :, sl]
        c["v"] = v_ref[b, :, sl]
        c["e_q"] = jnp.exp(cum)
        c["qt"] = (q * c["e_q"]).astype(BF16)
        if exact:
            log_f = logf_scr[b, :, sl]
            since = range_sum(_M_DIAG, log_f)
            pieces = [((q * jnp.exp(since)).astype(BF16), (k * jnp.exp(-since)).astype(BF16),
                       diag_mask(SAFE_DIAG))]
            for idx, m in enumerate(_MERGE_BLOCKS):
                w = jnp.exp(range_sum(_M_MERGE0 + idx, log_f))
                pieces.append(((q * w).astype(BF16), (k * w).astype(BF16), merge_mask(m)))
            c["k_state"] = (k * jnp.exp(range_sum(_M_AFTER, log_f))).astype(BF16)
        else:
            c["k_state"] = (k * jnp.exp(-cum)).astype(BF16)
            pieces = [(c["qt"], c["k_state"], diag_mask(sub))]
        c["pieces"] = pieces

    def intra_scores(c):
        c["s"] = []
        for r0 in range(0, tt, MAT_BLOCK):
            r = slice(r0, r0 + MAT_BLOCK)
            per_head = []
            for keep in (first_blk, ~first_blk):
                s = jnp.zeros((MAT_BLOCK, MAT_BLOCK), F32)
                for q_f, k_f, mask in c["pieces"]:
                    q_h = jnp.where(keep, q_f[r], jnp.zeros_like(q_f[r]))
                    s = jnp.where(mask, _dot_nt(q_h, k_f[r]), s)
                per_head.append(s.astype(BF16))
            c["s"].append(per_head)

    def intra_values(c):
        blocks = []
        for i, r0 in enumerate(range(0, tt, MAT_BLOCK)):
            r = slice(r0, r0 + MAT_BLOCK)
            blocks.append(jnp.where(first_blk, _dot(c["s"][i][0], c["v"][r]),
                                    _dot(c["s"][i][1], c["v"][r])))
        c["o"] = jnp.concatenate(blocks, axis=0)

    def state_updates(c):
        c["upd"] = [jnp.where(same_head, _dot_tn(c["v"][j * sub:(j + 1) * sub],
                                                 c["k_state"][j * sub:(j + 1) * sub]), 0.0)
                    for j in range(tt // sub)]

    def state_chain(c, exact):
        st = st_ref[c["b"], c["pair"]]
        outs = []
        for j in range(tt // sub):
            r = slice(j * sub, (j + 1) * sub)
            outs.append(c["o"][r] + _dot_nt(c["qt"][r], st.astype(BF16)))
            decay = c["e_q"][(j + 1) * sub - 1:(j + 1) * sub, :]
            st = decay * st + c["upd"][j] if exact else decay * (st + c["upd"][j])
        st_ref[c["b"], c["pair"]] = st
        c["o"] = jnp.concatenate(outs, axis=0)

    def head_norm_and_gate(c):
        b, sl = c["b"], c["sl"]
        o = c["o"] * lax.rsqrt(_head_mean_sq(c["o"], ones) + EPS) * gain_ref[:, sl]
        g = g_ref[b, :, sl].astype(F32)
        o_ref[b, :, sl] = (o * (g * _sigmoid(g))).astype(BF16)

    def all_tiles(exact):
        tiles = [dict(b=b, pair=pair, sl=slice(pair * LANES, (pair + 1) * LANES))
                 for b in range(nb) for pair in range(H_HGRN // 2)]
        for stage in (functools.partial(factors, exact=exact), intra_scores, intra_values,
                      state_updates, functools.partial(state_chain, exact=exact),
                      head_norm_and_gate):
            for c in tiles:
                stage(c)

    pl.when(exact_fallback)(functools.partial(all_tiles, True))
    pl.when(jnp.logical_not(exact_fallback))(functools.partial(all_tiles, False))


def _hgrn(pa, za, lb, gain, mats, ones, batch, seq_len, layer):
    n = pa.shape[0]
    tt = TT_HGRN
    nb = NB_HGRN
    const = lambda b, t: (0, 0)
    tile_f32 = pltpu.VMEM((nb, tt, D_HGRN), F32)
    pa3 = pa.reshape(batch, seq_len, 3 * D_HGRN)
    blk = lambda j: pl.BlockSpec((nb, tt, D_HGRN), lambda b, t: (b, t, j))
    out = pl.pallas_call(
        functools.partial(_hgrn_kernel, layer=layer),
        grid=(batch // nb, seq_len // tt),
        in_specs=[
            blk(0), blk(1), blk(2), blk(0),
            pl.BlockSpec(lb.shape, const),
            pl.BlockSpec((1, D_HGRN), const),
            pl.BlockSpec(mats.shape, lambda b, t: (0, 0, 0)),
            pl.BlockSpec((LANES, LANES), const),
        ],
        out_specs=blk(0),
        out_shape=jax.ShapeDtypeStruct((batch, seq_len, D_HGRN), BF16),
        scratch_shapes=[pltpu.VMEM((nb, H_HGRN // 2, LANES, LANES), F32),
                        tile_f32, tile_f32, tile_f32],
        compiler_params=pltpu.CompilerParams(
            dimension_semantics=("arbitrary", "arbitrary"), vmem_limit_bytes=VMEM_LIMIT),
        name="hgrn",
    )(pa3, pa3, pa3, za.reshape(batch, seq_len, D_HGRN), lb, gain, mats, ones)
    return out.reshape(n, D_HGRN)


def _ret_kernel(q_ref, k_ref, v_ref, g_ref, dec_ref, qs_ref, ks_ref, cd_ref, gain_ref, ones_ref,
                o_ref, st_ref):
    @pl.when(pl.program_id(1) == 0)
    def _():
        st_ref[...] = jnp.zeros_like(st_ref)

    nb, c = q_ref.shape[0], q_ref.shape[1]
    first = _first_head_lanes((c, LANES))
    rr = lax.broadcasted_iota(jnp.int32, (LANES, LANES), 0)
    cc = lax.broadcasted_iota(jnp.int32, (LANES, LANES), 1)
    same_head = ((rr ^ cc) & HEAD_DIM) == 0
    ones = ones_ref[...]

    def scores(c):
        q, k = q_ref[c["b"], :, c["sl"]], k_ref[c["b"], :, c["sl"]]
        c["s"] = [(_dot_nt(jnp.where(keep, q, jnp.zeros_like(q)), k)
                   * dec_ref[2 * c["pair"] + hh]).astype(BF16)
                  for hh, keep in enumerate((first, ~first))]

    def values_and_state(c):
        b, pair, sl = c["b"], c["pair"], c["sl"]
        q, k, v = q_ref[b, :, sl], k_ref[b, :, sl], v_ref[b, :, sl]
        st = st_ref[b, pair]
        q_dec = (q.astype(F32) * qs_ref[:, sl]).astype(BF16)
        c["o"] = (jnp.where(first, _dot(c["s"][0], v), _dot(c["s"][1], v))
                  + _dot_nt(q_dec, st.astype(BF16)))
        k_dec = (k.astype(F32) * ks_ref[:, sl]).astype(BF16)
        st_ref[b, pair] = cd_ref[:, sl] * st + jnp.where(same_head, _dot_tn(v, k_dec), 0.0)

    def head_norm_and_gate(c):
        b, sl = c["b"], c["sl"]
        o = c["o"] * lax.rsqrt(_head_mean_sq(c["o"], ones) + EPS) * gain_ref[:, sl]
        g = g_ref[b, :, sl].astype(F32)
        o_ref[b, :, sl] = (o * (g * _sigmoid(g))).astype(BF16)

    tiles = [dict(b=b, pair=pair, sl=slice(pair * LANES, (pair + 1) * LANES))
             for b in range(nb) for pair in range(H_RET // 2)]
    for stage in (scores, values_and_state, head_norm_and_gate):
        for c in tiles:
            stage(c)


def _retention(pc, dec, qs, ks, cd, gain, ones, batch, seq_len):
    n = pc.shape[0]
    c = C_RET
    nb = NB_RET
    const2 = lambda b, t: (0, 0)
    pc3 = pc.reshape(batch, seq_len, 4 * D_RET)
    blk = lambda j: pl.BlockSpec((nb, c, D_RET), lambda b, t: (b, t, j))
    out = pl.pallas_call(
        _ret_kernel,
        grid=(batch // nb, seq_len // c),
        in_specs=[
            blk(0), blk(1), blk(2), blk(3),
            pl.BlockSpec((H_RET, c, c), lambda b, t: (0, 0, 0)),
            pl.BlockSpec((c, D_RET), const2),
            pl.BlockSpec((c, D_RET), const2),
            pl.BlockSpec((1, D_RET), const2),
            pl.BlockSpec((1, D_RET), const2),
            pl.BlockSpec((LANES, LANES), const2),
        ],
        out_specs=blk(0),
        out_shape=jax.ShapeDtypeStruct((batch, seq_len, D_RET), BF16),
        scratch_shapes=[pltpu.VMEM((nb, H_RET // 2, LANES, LANES), F32)],
        compiler_params=pltpu.CompilerParams(
            dimension_semantics=("arbitrary", "arbitrary"), vmem_limit_bytes=VMEM_LIMIT),
        name="retention",
    )(pc3, pc3, pc3, pc3, dec, qs, ks, cd, gain, ones)
    return out.reshape(n, D_RET)


def _fox_kernel(qt_ref, k_ref, vt_ref, g_ref, o_ref, s_scr, p_scr, acc_scr):
    tq = TQ_FOX
    nq = o_ref.shape[0] // tq
    blocks = [(qi, kj) for qi in range(nq) for kj in range(qi + 1)]
    key = lax.broadcasted_iota(jnp.int32, (tq, tq), 0)
    qry = lax.broadcasted_iota(jnp.int32, (tq, tq), 1)
    causal = qry >= key

    def scores(t):
        qi, kj = blocks[t]
        for hh in range(2):
            slot = slice(hh * LANES, (hh + 1) * LANES)
            s_scr[t % 2, hh] = _dot(k_ref[kj * tq:(kj + 1) * tq, slot],
                                    qt_ref[0, slot, qi * tq:(qi + 1) * tq])

    def softmax(t, m_old):
        qi, kj = blocks[t]
        m_new, alpha = [], []
        for hh in range(2):
            s = s_scr[t % 2, hh]
            if kj == qi:
                s = jnp.where(causal, s, MASK_VALUE)
            m_blk = jnp.max(s, axis=0, keepdims=True)
            if kj > 0:
                m_blk = jnp.maximum(m_old[hh], m_blk)
                alpha.append(jnp.exp2(m_old[hh] - m_blk))
            p_scr[t % 2, hh] = jnp.exp2(s - m_blk).astype(BF16)
            m_new.append(m_blk)
        return m_new, alpha

    def accumulate(t, alpha):
        qi, kj = blocks[t]
        for hh in range(2):
            pv = _dot(vt_ref[0, hh * VT_ROWS:(hh + 1) * VT_ROWS, kj * tq:(kj + 1) * tq],
                      p_scr[t % 2, hh])
            acc_scr[qi % 2, hh] = pv if kj == 0 else alpha[hh] * acc_scr[qi % 2, hh] + pv
        if kj == qi:
            heads = []
            for hh in range(2):
                total = acc_scr[qi % 2, hh]
                heads.append(total[:HEAD_DIM] / total[HEAD_DIM:HEAD_DIM + 1])
            o_t = jnp.concatenate(heads, axis=0)
            rows = slice(qi * tq, (qi + 1) * tq)
            o_ref[rows, :] = (o_t.T * _sigmoid(g_ref[rows, :].astype(F32))).astype(BF16)

    scores(0)
    m_run, alpha_prev = None, None
    for t in range(len(blocks)):
        m_run, alpha_t = softmax(t, m_run)
        if t + 1 < len(blocks):
            scores(t + 1)
        if t > 0:
            accumulate(t - 1, alpha_prev)
        alpha_prev = alpha_t
    accumulate(len(blocks) - 1, alpha_prev)


def _fox(qt, ka, vt, gb, batch, seq_len):
    n = ka.shape[0]
    tq = TQ_FOX
    pairs = H_FOX // 2
    return pl.pallas_call(
        _fox_kernel,
        grid=(batch, pairs),
        in_specs=[
            pl.BlockSpec((1, 2 * LANES, seq_len), lambda b, p: (b, p, 0)),
            pl.BlockSpec((seq_len, 2 * LANES), lambda b, p: (b, p)),
            pl.BlockSpec((1, 2 * VT_ROWS, seq_len), lambda b, p: (b, p, 0)),
            pl.BlockSpec((seq_len, LANES), lambda b, p: (b, p)),
        ],
        out_specs=pl.BlockSpec((seq_len, LANES), lambda b, p: (b, p)),
        out_shape=jax.ShapeDtypeStruct((n, D_FOX), BF16),
        scratch_shapes=[
            pltpu.VMEM((2, 2, tq, tq), F32),
            pltpu.VMEM((2, 2, tq, tq), BF16),
            pltpu.VMEM((2, 2, VT_ROWS, tq), F32),
        ],
        compiler_params=pltpu.CompilerParams(
            dimension_semantics=("arbitrary", "arbitrary"), vmem_limit_bytes=VMEM_LIMIT),
        name="fox",
    )(qt, ka, vt, gb)


def _mix_ffn_kernel(oa_ref, ob_ref, oc_ref, wo_ref, gmix_ref, x_ref, gpre_ref,
                    wup_ref, cw_ref, cb_ref, wd_ref, gpost_ref, out_ref, xp_scr, act_scr, tail_scr,
                    *, tiles_per_seq):
    i = pl.program_id(0)
    tm = x_ref.shape[0]
    fc = FC_FFN

    mixer_out = jnp.concatenate([oa_ref[...], ob_ref[...], oc_ref[...]], axis=1)
    x_mid = x_ref[...] + _rmsnorm_rows(_dot(mixer_out, wo_ref[...]), gmix_ref[...])
    xp_scr[...] = pltpu.einshape("(ab)f->(ba)f", x_mid, a=SUBLANES)
    h = _rmsnorm_rows(xp_scr[...], gpre_ref[...]).astype(BF16)

    @pl.when(i == 0)
    def _():
        tail_scr[...] = jnp.zeros_like(tail_scr)

    seq_start = i % tiles_per_seq == 0
    first_sublane = lax.broadcasted_iota(jnp.int32, (SUBLANES, fc), 0) == 0

    def causal_conv(u, col, slot):
        prev = jnp.where(seq_start, 0.0, tail_scr[slot])
        last = u[tm - 2 * SUBLANES:, :]
        tail_scr[slot] = last
        head = []
        for g in range(2):
            rows = slice(g * SUBLANES, (g + 1) * SUBLANES)
            entering = prev[(g + 1) * SUBLANES - 1:(g + 1) * SUBLANES, :]
            head.append(jnp.where(first_sublane, entering, pltpu.roll(last[rows], 1, 0)))
        u1 = jnp.concatenate([head[1], u[:tm - SUBLANES]], axis=0)
        u2 = jnp.concatenate([head[0], head[1], u[:tm - 2 * SUBLANES]], axis=0)
        cols = slice(col, col + fc)
        return (cb_ref[:, cols] + cw_ref[0:1, cols] * u2 + cw_ref[1:2, cols] * u1
                + cw_ref[2:3, cols] * u)

    for c in range(D_FF // fc):
        gate = causal_conv(_dot(h, wup_ref[:, c * fc:(c + 1) * fc]), c * fc, 2 * c)
        val = causal_conv(_dot(h, wup_ref[:, D_FF + c * fc:D_FF + (c + 1) * fc]),
                          D_FF + c * fc, 2 * c + 1)
        act_scr[:, c * fc:(c + 1) * fc] = (gate * _sigmoid(gate) * val).astype(BF16)

    y = _dot(act_scr[...], wd_ref[...])
    res = xp_scr[...] + _rmsnorm_rows(y, gpost_ref[...])
    out_ref[...] = pltpu.einshape("(ba)f->(ab)f", res, a=SUBLANES)


def _mix_ffn(oa, ob, oc, wo, gmix, x2, gpre, w_up, conv_w, conv_b, w_down, gpost, seq_len):
    n = x2.shape[0]
    tm = TM_FFN
    nf = D_FF // FC_FFN
    tiles_per_seq = seq_len // tm
    row = lambda i: (i, 0)
    resident = lambda shape: pl.BlockSpec(shape, lambda i: (0, 0), pipeline_mode=pl.Buffered(1))
    return pl.pallas_call(
        functools.partial(_mix_ffn_kernel, tiles_per_seq=tiles_per_seq),
        grid=(n // tm,),
        in_specs=[
            pl.BlockSpec((tm, D_HGRN), row),
            pl.BlockSpec((tm, D_FOX), row),
            pl.BlockSpec((tm, D_RET), row),
            resident((D_HGRN + D_FOX + D_RET, D_MODEL)),
            resident((1, D_MODEL)),
            pl.BlockSpec((tm, D_MODEL), row),
            resident((1, D_MODEL)),
            resident((D_MODEL, 2 * D_FF)),
            resident((CONV_W, 2 * D_FF)),
            resident((1, 2 * D_FF)),
            resident((D_FF, D_MODEL)),
            resident((1, D_MODEL)),
        ],
        out_specs=pl.BlockSpec((tm, D_MODEL), row),
        out_shape=jax.ShapeDtypeStruct((n, D_MODEL), F32),
        scratch_shapes=[
            pltpu.VMEM((tm, D_MODEL), F32),
            pltpu.VMEM((tm, D_FF), BF16),
            pltpu.VMEM((2 * nf, 2 * SUBLANES, FC_FFN), F32),
        ],
        compiler_params=pltpu.CompilerParams(
            dimension_semantics=("arbitrary",), vmem_limit_bytes=VMEM_LIMIT),
        name="mix_ffn",
    )(oa, ob, oc, wo, gmix, x2, gpre, w_up, conv_w, conv_b, w_down, gpost)


def _block_ones(n):
    idx = jnp.arange(n) // HEAD_DIM
    return (idx[:, None] == idx[None, :]).astype(BF16)


def _lower_tri(n, block):
    r = jnp.arange(n)
    return ((r[:, None] >= r[None, :]) & (r[:, None] // block == r[None, :] // block)).astype(BF16)


def _rotary_tables(seq_len):
    inv_freq = 1.0 / (ROPE_BASE ** (jnp.arange(0, HEAD_DIM, 2, dtype=F32) / HEAD_DIM))
    ang = jnp.arange(seq_len, dtype=F32)[:, None] * inv_freq[None, :]
    cos, sin = jnp.cos(ang), jnp.sin(ang)
    cos_full = jnp.tile(jnp.concatenate([cos, cos], axis=-1), (1, H_RET))
    sin_signed = jnp.tile(jnp.concatenate([-sin, sin], axis=-1), (1, H_RET))
    return cos_full, sin_signed


def _retention_tables(c):
    log_gamma = jnp.log1p(-jnp.exp2(-5.0 - jnp.arange(H_RET, dtype=F32)))
    pos = jnp.arange(c, dtype=F32)
    rel = pos[:, None] - pos[None, :]
    dec = jnp.where(rel >= 0, jnp.exp(log_gamma[:, None, None] * jnp.maximum(rel, 0.0)), 0.0)
    per_lane = lambda a: jnp.repeat(a, HEAD_DIM, axis=-1)
    qs = per_lane(jnp.exp(log_gamma[None, :] * (pos[:, None] + 1.0)))
    ks = per_lane(jnp.exp(log_gamma[None, :] * (c - 1.0 - pos[:, None])))
    cd = per_lane(jnp.exp(log_gamma * c)[None, :])
    return dec, qs, ks, cd


def _reorder_in_weight_rows(w_t):
    widths = [D_HGRN] * 4 + [D_FOX] * 3 + [H_FOX] + [D_FOX] + [D_RET] * 4
    offs = [0]
    for wd in widths:
        offs.append(offs[-1] + wd)
    part = lambda j: w_t[offs[j]:offs[j + 1]]
    a_q, a_f, a_i, a_g, b_q, b_k, b_v, b_f, b_g, c_q, c_k, c_v, c_g = (part(j) for j in range(13))
    b_f = jnp.pad(b_f, ((0, FGATE_PAD - H_FOX), (0, 0)))
    return jnp.concatenate(
        [a_q, a_i, a_g, a_f, b_q, b_k, b_v, b_g, b_f, c_q, c_k, c_v, c_g], axis=0)


def _transpose_cast_kernel(w_ref, o_ref):
    o_ref[...] = w_ref[...].T.astype(BF16)


def _transpose_cast(w_t):
    rows = w_t.shape[0]
    blk = D_HGRN
    return pl.pallas_call(
        _transpose_cast_kernel,
        grid=(rows // blk,),
        in_specs=[pl.BlockSpec((blk, D_MODEL), lambda i: (i, 0))],
        out_specs=pl.BlockSpec((D_MODEL, blk), lambda i: (0, i)),
        out_shape=jax.ShapeDtypeStruct((D_MODEL, rows), BF16),
        compiler_params=pltpu.CompilerParams(
            dimension_semantics=("arbitrary",), vmem_limit_bytes=VMEM_LIMIT),
        name="wprep",
    )(w_t)


def kernel(x, w_in, b_fox_f, fox_q_gain, fox_k_gain, hgrn_lb, hgrn_out_gain, ret_out_gain, w_out,
           g_mix_pre, g_mix_post, w_up, conv_w, conv_b, w_down, g_ffn_pre, g_ffn_post):
    batch, seq_len, d_model = x.shape
    assert d_model == D_MODEL
    assert seq_len % TM_FFN == 0 and seq_len % TQ_FOX == 0
    assert batch % NB_HGRN == 0 and batch % NB_RET == 0
    depth = w_in.shape[0]
    n = batch * seq_len

    cos_full, sin_signed = _rotary_tables(seq_len)
    dec, qs, ks, cd = _retention_tables(C_RET)
    tri_proj = _lower_tri(CUM_BLOCK, CUM_BLOCK)
    mats_hgrn = _hgrn_matrices(MAT_BLOCK)
    ones_pair = _block_ones(LANES)
    lb = hgrn_lb.astype(F32)

    w_in_t = jnp.transpose(w_in, (2, 0, 1))

    x2 = x.reshape(n, D_MODEL)
    for l in range(depth):
        w = _transpose_cast(_reorder_in_weight_rows(w_in_t[:, l, :]))
        bf = jnp.pad(b_fox_f[l].astype(F32), (0, FGATE_PAD - H_FOX))[None, :]
        qg = jnp.tile(fox_q_gain[l].astype(F32), H_FOX)[None, :]
        kg = jnp.tile(fox_k_gain[l].astype(F32), H_FOX)[None, :]
        pa, za, qt, ka, vt, gb, pc = _inproj(x2, g_mix_pre[l][None, :], w, bf, qg, kg, cos_full,
                                             sin_signed, tri_proj, ones_pair, seq_len)

        o_a = _hgrn(pa, za, lb, hgrn_out_gain[l].reshape(1, D_HGRN), mats_hgrn, ones_pair,
                    batch, seq_len, l)
        o_b = _fox(qt, ka, vt, gb, batch, seq_len)
        o_c = _retention(pc, dec, qs, ks, cd, ret_out_gain[l].reshape(1, D_RET), ones_pair,
                         batch, seq_len)

        x2 = _mix_ffn(o_a, o_b, o_c, w_out[l].astype(BF16),
                      g_mix_post[l][None, :], x2, g_ffn_pre[l][None, :], w_up[l].astype(BF16),
                      conv_w[l], conv_b[l][None, :], w_down[l].astype(BF16),
                      g_ffn_post[l][None, :], seq_len)
    return x2.reshape(batch, seq_len, D_MODEL)
```
